```python
import jax, jax.numpy as jnp
from jax import lax
import numpy as np

D_MODEL = 2048
BATCH = 4
SEQ = 2048
DEPTH = 1
DEC_BATCH = 128
DEC_SEQ = 8
PAST_LEN = 16384
PAGE_SIZE = 128

HEAD_DIM = 128
C_A = D_MODEL // 2
C_B = D_MODEL - C_A
H_A = C_A // HEAD_DIM
H_B = C_B // HEAD_DIM
CONV_A = 31
CHUNK = 128
FFN_CONV = 3
D_FF = ((8 * D_MODEL // 3 + 255) // 256) * 256
D_PLE = 256
EPS = 1e-6

kernel_name = "hybrid_conformer_gmlp_decoder_step"


def rmsnorm(x, g):
    x32 = x.astype(jnp.float32)
    y = x32 * lax.rsqrt(jnp.mean(x32 * x32, axis=-1, keepdims=True) + EPS)
    return (y * g.astype(jnp.float32)).astype(x.dtype)


def head_layernorm(x, n_heads, g, b):
    shp = x.shape
    xh = x.astype(jnp.float32).reshape(shp[:-1] + (n_heads, shp[-1] // n_heads))
    mu = jnp.mean(xh, axis=-1, keepdims=True)
    xc = xh - mu
    var = jnp.mean(xc * xc, axis=-1, keepdims=True)
    y = (xc * lax.rsqrt(var + EPS)).reshape(shp)
    return (y * g.astype(jnp.float32) + b.astype(jnp.float32)).astype(x.dtype)


def causal_dwconv(x_ext, w, bias):
    c = x_ext.shape[-1]
    out = lax.conv_general_dilated(
        x_ext, w[:, None, :].astype(x_ext.dtype), window_strides=(1,), padding='VALID',
        dimension_numbers=('NWC', 'WIO', 'NWC'), feature_group_count=c)
    return out + bias.astype(out.dtype)


def chunk_spatial_gate(u, v, w_s, b_s):
    bsz, t, c = v.shape
    lc = min(t, CHUNK)
    n = -(-t // lc)
    pad = n * lc - t
    vp = jnp.pad(v, ((0, 0), (0, pad), (0, 0))).reshape(bsz, n, lc, H_B, HEAD_DIM)
    mask = jnp.tril(jnp.ones((lc, lc), dtype=bool))
    w = jnp.where(mask, w_s[:, :lc, :lc], 0).astype(v.dtype)
    mixed = jnp.einsum('hts,bnshd->bnthd', w, vp)
    mixed = mixed + b_s[:, :lc].T.astype(v.dtype)[None, None, :, :, None]
    mixed = mixed.reshape(bsz, n * lc, c)[:, :t]
    return u * mixed


def layer(x, p, conv_hist, ffn_hist, g_mix, w_in, w_dw_a, b_dw_a, g_ln_a, b_ln_a,
          g_ln_v, b_ln_v, w_s, b_s, w_out, g_ffn, w_up, w_dw_f, b_dw_f, w_down,
          g_ple, w_ple_gate, w_ple_proj):
    t = x.shape[1]
    h = rmsnorm(x, g_mix)
    z = h @ w_in
    a_val = z[..., :C_A]
    a_gate = z[..., C_A:2 * C_A]
    u = z[..., 2 * C_A:2 * C_A + C_B]
    v = z[..., 2 * C_A + C_B:]
    a = a_val * jax.nn.sigmoid(a_gate)
    a_ext = jnp.concatenate([conv_hist.astype(a.dtype), a], axis=1)
    a_out = jax.nn.silu(head_layernorm(causal_dwconv(a_ext, w_dw_a, b_dw_a), H_A, g_ln_a, b_ln_a))
    new_conv_hist = a_ext[:, -(CONV_A - 1):]
    u = jax.nn.gelu(u)
    v_n = head_layernorm(jax.nn.gelu(v), H_B, g_ln_v, b_ln_v)
    b_out = chunk_spatial_gate(u, v_n, w_s, b_s)
    last_start = ((t - 1) // CHUNK) * CHUNK
    chunk_v = v_n[:, last_start:]
    x = x + jnp.concatenate([a_out, b_out], axis=-1) @ w_out
    up = rmsnorm(x, g_ffn) @ w_up
    up_ext = jnp.concatenate([ffn_hist.astype(up.dtype), up], axis=1)
    upc = causal_dwconv(up_ext, w_dw_f, b_dw_f)
    x = x + (jax.nn.silu(upc[..., :D_FF]) * upc[..., D_FF:]) @ w_down
    new_ffn_hist = up_ext[:, -(FFN_CONV - 1):]
    gate = jax.nn.sigmoid(rmsnorm(x, g_ple) @ w_ple_gate)
    x = x + gate * (p @ w_ple_proj)
    return x, new_conv_hist, new_ffn_hist, chunk_v


def setup_inputs(seed: int = 0) -> dict:
    key = jax.random.key(seed)
    ks = jax.random.split(key, 32)
    f32 = jnp.float32
    nrm = lambda k, shp, s: jax.random.normal(k, shp, f32) * s
    return {
        "x_prompt": nrm(ks[0], (BATCH, SEQ, D_MODEL), 1.0),
        "x_sample": nrm(ks[1], (DEC_BATCH, DEC_SEQ, D_MODEL), 1.0),
        "p_prompt": nrm(ks[2], (DEPTH, BATCH, SEQ, D_PLE), 1.0),
        "p_sample": nrm(ks[3], (DEPTH, DEC_BATCH, DEC_SEQ, D_PLE), 1.0),
        "state_conv_a": nrm(ks[4], (DEPTH, DEC_BATCH, CONV_A - 1, C_A), 0.5),
        "state_ffn_conv": nrm(ks[5], (DEPTH, DEC_BATCH, FFN_CONV - 1, 2 * D_FF), 1.0),
        "g_mix": 1.0 + nrm(ks[6], (DEPTH, D_MODEL), 0.1),
        "w_in": nrm(ks[7], (DEPTH, D_MODEL, 2 * C_A + 2 * C_B), D_MODEL ** -0.5),
        "w_dw_a": nrm(ks[8], (DEPTH, CONV_A, C_A), CONV_A ** -0.5),
        "b_dw_a": nrm(ks[9], (DEPTH, C_A), 0.02),
        "g_ln_a": 1.0 + nrm(ks[10], (DEPTH, C_A), 0.1),
        "b_ln_a": nrm(ks[11], (DEPTH, C_A), 0.02),
        "g_ln_v": 1.0 + nrm(ks[12], (DEPTH, C_B), 0.1),
        "b_ln_v": nrm(ks[13], (DEPTH, C_B), 0.02),
        "w_s": nrm(ks[14], (DEPTH, H_B, CHUNK, CHUNK), CHUNK ** -0.5),
        "b_s": nrm(ks[15], (DEPTH, H_B, CHUNK), 0.02),
        "w_out": nrm(ks[16], (DEPTH, C_A + C_B, D_MODEL), (C_A + C_B) ** -0.5),
        "g_ffn": 1.0 + nrm(ks[17], (DEPTH, D_MODEL), 0.1),
        "w_up": nrm(ks[18], (DEPTH, D_MODEL, 2 * D_FF), D_MODEL ** -0.5),
        "w_dw_f": nrm(ks[19], (DEPTH, FFN_CONV, 2 * D_FF), FFN_CONV ** -0.5),
        "b_dw_f": nrm(ks[20], (DEPTH, 2 * D_FF), 0.02),
        "w_down": nrm(ks[21], (DEPTH, D_FF, D_MODEL), D_FF ** -0.5),
        "g_ple": 1.0 + nrm(ks[22], (DEPTH, D_MODEL), 0.1),
        "w_ple_gate": nrm(ks[23], (DEPTH, D_MODEL, D_MODEL), D_MODEL ** -0.5),
        "w_ple_proj": nrm(ks[24], (DEPTH, D_PLE, D_MODEL), D_PLE ** -0.5),
        "g_final": 1.0 + nrm(ks[25], (D_MODEL,), 0.1),
    }


def reference(x_prompt, x_sample, p_prompt, p_sample, state_conv_a, state_ffn_conv,
              g_mix, w_in, w_dw_a, b_dw_a, g_ln_a, b_ln_a, g_ln_v, b_ln_v, w_s, b_s,
              w_out, g_ffn, w_up, w_dw_f, b_dw_f, w_down, g_ple, w_ple_gate, w_ple_proj,
              g_final):
    xp, xs = x_prompt, x_sample
    conv_p, conv_s, ffn_p, ffn_s, cv_p, cv_s = [], [], [], [], [], []
    for i in range(DEPTH):
        params = (g_mix[i], w_in[i], w_dw_a[i], b_dw_a[i], g_ln_a[i], b_ln_a[i],
                  g_ln_v[i], b_ln_v[i], w_s[i], b_s[i], w_out[i], g_ffn[i], w_up[i],
                  w_dw_f[i], b_dw_f[i], w_down[i], g_ple[i], w_ple_gate[i], w_ple_proj[i])
        hist_a0 = jnp.zeros((xp.shape[0], CONV_A - 1, C_A), xp.dtype)
        hist_f0 = jnp.zeros((xp.shape[0], FFN_CONV - 1, 2 * D_FF), xp.dtype)
        xp, ca, cf, cv = layer(xp, p_prompt[i], hist_a0, hist_f0, *params)
        conv_p.append(ca); ffn_p.append(cf); cv_p.append(cv)
        xs, ca, cf, cv = layer(xs, p_sample[i], state_conv_a[i], state_ffn_conv[i], *params)
        conv_s.append(ca); ffn_s.append(cf); cv_s.append(cv)
    y_prompt = rmsnorm(xp, g_final)
    y_sample = rmsnorm(xs, g_final)
    return (y_prompt, y_sample, jnp.stack(conv_p), jnp.stack(conv_s), jnp.stack(ffn_p),
            jnp.stack(ffn_s), jnp.stack(cv_p), jnp.stack(cv_s))
```

```python
import functools

import jax
import jax.numpy as jnp
from jax import lax
from jax.experimental import pallas as pl
from jax.experimental.pallas import tpu as pltpu

D_MODEL = 2048
HEAD_DIM = 128
C_A = D_MODEL // 2
C_B = D_MODEL - C_A
N_HEADS = C_B // HEAD_DIM
CONV_A = 31
HIST_A = 32
CHUNK = 128
FFN_CONV = 3
D_FF = 5632
D_PLE = 256
EPS = 1e-6

SUBLANES = 8
ROWS = 32
VMEM_LIMIT = 56 * 1024 * 1024

F32 = jnp.float32
BF16 = jnp.bfloat16


def _resident(shape):
    return pl.BlockSpec(shape, lambda *_: (0,) * len(shape), pipeline_mode=pl.Buffered(1))


def _row_loop(n_rows, step, fn):
    def body(c, carry):
        fn(pl.multiple_of(c * step, step))
        return carry
    lax.fori_loop(0, n_rows // step, body, 0)


def _rms_to(x_ref, g_ref, dst_ref, n_rows):
    def piece(r):
        xv = x_ref[pl.ds(r, ROWS), :]
        ms = jnp.mean(xv * xv, axis=-1, keepdims=True)
        dst_ref[pl.ds(r, ROWS), :] = (xv * lax.rsqrt(ms + EPS) * g_ref[...]).astype(BF16)
    _row_loop(n_rows, ROWS, piece)


def _head_ln(x, g, b):
    outs = []
    for h in range(x.shape[-1] // HEAD_DIM):
        xh = x[:, h * HEAD_DIM:(h + 1) * HEAD_DIM]
        mu = jnp.mean(xh, axis=-1, keepdims=True)
        xc = xh - mu
        var = jnp.mean(xc * xc, axis=-1, keepdims=True)
        outs.append(xc * lax.rsqrt(var + EPS))
    return jnp.concatenate(outs, axis=-1) * g + b


def _mixer_body(sample, tm, seq_len, *refs):
    if sample:
        (x_ref, st_ref, gmix_ref, win_ref, wdw_ref, bdw_ref, glna_ref, blna_ref, glnv_ref,
         blnv_ref, ws_ref, bsb_ref, wout_ref,
         y_ref, aout_ref, vout_ref,
         h_ref, z_ref, cat_ref, a_ref, conv_ref, tail_ref, vn_ref, wm_ref) = refs
    else:
        (x_ref, gmix_ref, win_ref, wdw_ref, bdw_ref, glna_ref, blna_ref, glnv_ref,
         blnv_ref, ws_ref, bsb_ref, wout_ref,
         y_ref, aout_ref, vout_ref,
         h_ref, z_ref, cat_ref, a_ref, vn_ref, wm_ref) = refs

    _rms_to(x_ref, gmix_ref, h_ref, tm)
    z_ref[...] = jnp.dot(h_ref[...], win_ref[...], preferred_element_type=F32)

    ti = lax.broadcasted_iota(jnp.int32, (CHUNK, CHUNK), 0)
    si = lax.broadcasted_iota(jnp.int32, (CHUNK, CHUNK), 1)
    if seq_len >= CHUNK:
        mask = si <= ti
    else:
        mask = ((si // seq_len) == (ti // seq_len)) & ((si % seq_len) <= (ti % seq_len))
    for h in range(N_HEADS):
        wm_ref[h] = jnp.where(mask, ws_ref[h], 0.0).astype(BF16)

    if not sample:
        t = pl.program_id(1)

        @pl.when(t == 0)
        def _():
            a_ref[0:HIST_A, :] = jnp.zeros((HIST_A, C_A), F32)

        @pl.when(t > 0)
        def _():
            a_ref[0:HIST_A, :] = a_ref[tm:tm + HIST_A, :]

    a_base = 0 if sample else HIST_A

    def chunk(c, carry):
        r0 = pl.multiple_of(c * CHUNK, CHUNK)

        def glu_piece(r):
            zv = z_ref[pl.ds(r0 + r, ROWS), 0:C_A]
            zg = z_ref[pl.ds(r0 + r, ROWS), C_A:2 * C_A]
            a = zv * jax.nn.sigmoid(zg)
            a_ref[pl.ds(a_base + r0 + r, ROWS), :] = a
            if sample:
                aout_ref[pl.ds(r0 + r, ROWS), :] = a
        _row_loop(CHUNK, ROWS, glu_piece)

        def finish_a(r, conv):
            y = _head_ln(conv, glna_ref[...], blna_ref[...])
            cat_ref[pl.ds(r0 + r, ROWS), 0:C_A] = (y * jax.nn.sigmoid(y)).astype(BF16)

        if sample:
            def seq_conv(q, carry2):
                rs = pl.multiple_of(r0 + q * seq_len, seq_len)
                s = c * (CHUNK // seq_len) + q
                tail_ref[0:8, :] = st_ref[s, CONV_A - 1 - 8:CONV_A - 1, :]
                tail_ref[8:16, :] = a_ref[pl.ds(rs, seq_len), :]
                acc = jnp.broadcast_to(bdw_ref[...], (seq_len, C_A))
                for k in range(CONV_A):
                    if k + seq_len <= CONV_A - 1:
                        win = st_ref[s, k:k + seq_len, :]
                    else:
                        o = k - (CONV_A - 1 - 8)
                        win = tail_ref[o:o + seq_len, :]
                    acc = acc + wdw_ref[k:k + 1, :] * win
                conv_ref[pl.ds(rs, seq_len), :] = acc
                return carry2
            lax.fori_loop(0, CHUNK // seq_len, seq_conv, 0)

            def ln_piece(r):
                finish_a(r, conv_ref[pl.ds(r0 + r, ROWS), :])
            _row_loop(CHUNK, ROWS, ln_piece)
        else:
            def conv_piece(r):
                win = a_ref[pl.ds(r0 + r, ROWS + HIST_A), :]
                acc = jnp.broadcast_to(bdw_ref[...], (ROWS, C_A))
                off = HIST_A - (CONV_A - 1)
                for k in range(CONV_A):
                    acc = acc + wdw_ref[k:k + 1, :] * win[off + k:off + k + ROWS, :]
                finish_a(r, acc)
            _row_loop(CHUNK, ROWS, conv_piece)

        def v_piece(r):
            v = jax.nn.gelu(z_ref[pl.ds(r0 + r, ROWS), 2 * C_A + C_B:2 * C_A + 2 * C_B])
            vn = _head_ln(v, glnv_ref[...], blnv_ref[...])
            vn_ref[pl.ds(r, ROWS), :] = vn.astype(BF16)
            if sample:
                vout_ref[pl.ds(r0 + r, ROWS), :] = vn
            else:
                vout_ref[0, pl.ds(r, ROWS), :] = vn
        _row_loop(CHUNK, ROWS, v_piece)

        for h in range(N_HEADS):
            lo = h * HEAD_DIM
            mixed = jnp.dot(wm_ref[h], vn_ref[:, lo:lo + HEAD_DIM], preferred_element_type=F32)
            mixed = mixed + bsb_ref[:, lo:lo + HEAD_DIM]
            u = jax.nn.gelu(z_ref[pl.ds(r0, CHUNK), 2 * C_A + lo:2 * C_A + lo + HEAD_DIM])
            cat_ref[pl.ds(r0, CHUNK), C_A + lo:C_A + lo + HEAD_DIM] = (u * mixed).astype(BF16)
        return carry

    lax.fori_loop(0, tm // CHUNK, chunk, 0)

    if not sample:
        aout_ref[0] = a_ref[tm:tm + HIST_A, :]

    y_ref[...] = x_ref[...] + jnp.dot(cat_ref[...], wout_ref[...], preferred_element_type=F32)


def _mixer(x, state, seq_len, params, tm):
    (g_mix, w_in, w_dw, b_dw, g_ln_a, b_ln_a, g_ln_v, b_ln_v, w_s, bias_rows, w_out) = params
    rows = x.shape[0]
    sample = state is not None
    vec = lambda n: _resident((1, n))
    common_in = [vec(D_MODEL), _resident((D_MODEL, 2 * C_A + 2 * C_B)), _resident((HIST_A, C_A)),
                 vec(C_A), vec(C_A), vec(C_A), vec(C_B), vec(C_B),
                 _resident((N_HEADS, CHUNK, CHUNK)), _resident((CHUNK, C_B)),
                 _resident((C_A + C_B, D_MODEL))]
    common_scratch = [pltpu.VMEM((tm, D_MODEL), BF16),
                      pltpu.VMEM((tm, 2 * C_A + 2 * C_B), F32),
                      pltpu.VMEM((tm, C_A + C_B), BF16)]
    tail_scratch = [pltpu.VMEM((CHUNK, C_B), BF16),
                    pltpu.VMEM((N_HEADS, CHUNK, CHUNK), BF16)]
    if sample:
        n_seq = rows // seq_len
        spt = tm // seq_len
        grid = (rows // tm,)
        row_map = lambda i: (i, 0)
        in_specs = [pl.BlockSpec((tm, D_MODEL), row_map),
                    pl.BlockSpec((spt, CONV_A - 1, C_A), lambda i: (i, 0, 0))] + common_in
        out_specs = [pl.BlockSpec((tm, D_MODEL), row_map),
                     pl.BlockSpec((tm, C_A), row_map),
                     pl.BlockSpec((tm, C_B), row_map)]
        out_shape = [jax.ShapeDtypeStruct((rows, D_MODEL), F32),
                     jax.ShapeDtypeStruct((rows, C_A), F32),
                     jax.ShapeDtypeStruct((rows, C_B), F32)]
        scratch = common_scratch + [pltpu.VMEM((tm, C_A), F32),
                                    pltpu.VMEM((tm, C_A), F32),
                                    pltpu.VMEM((16, C_A), F32)] + tail_scratch
        args = (x, state)
        sem = ("arbitrary",)
        del n_seq
    else:
        n_seq = rows // seq_len
        nt = seq_len // tm
        grid = (n_seq, nt)
        row_map = lambda b, t: (b * nt + t, 0)
        in_specs = [pl.BlockSpec((tm, D_MODEL), row_map)] + common_in
        out_specs = [pl.BlockSpec((tm, D_MODEL), row_map),
                     pl.BlockSpec((1, HIST_A, C_A), lambda b, t: (b, 0, 0)),
                     pl.BlockSpec((1, CHUNK, C_B), lambda b, t: (b, 0, 0))]
        out_shape = [jax.ShapeDtypeStruct((rows, D_MODEL), F32),
                     jax.ShapeDtypeStruct((n_seq, HIST_A, C_A), F32),
                     jax.ShapeDtypeStruct((n_seq, CHUNK, C_B), F32)]
        scratch = common_scratch + [pltpu.VMEM((tm + HIST_A, C_A), F32)] + tail_scratch
        args = (x,)
        sem = ("arbitrary", "arbitrary")
    return pl.pallas_call(
        functools.partial(_mixer_body, sample, tm, seq_len),
        grid=grid, in_specs=in_specs, out_specs=out_specs, out_shape=out_shape,
        scratch_shapes=scratch,
        compiler_params=pltpu.CompilerParams(dimension_semantics=sem,
                                             vmem_limit_bytes=VMEM_LIMIT),
        name="mixer_sample" if sample else "mixer_prompt",
    )(*args, g_mix, w_in, w_dw, b_dw, g_ln_a, b_ln_a, g_ln_v, b_ln_v, w_s, bias_rows, w_out)


def _ffn_body(sample, tm, tf, blocks_per_seq, *refs):
    if sample:
        (x_ref, stv_ref, stg_ref, g_ref, wuv_ref, wug_ref, wdv_ref, wdg_ref, bdv_ref, bdg_ref,
         wdn_ref, y_ref, lastv_ref, lastg_ref, h_ref, upv_ref, upg_ref, gate_ref) = refs
    else:
        (x_ref, g_ref, wuv_ref, wug_ref, wdv_ref, wdg_ref, bdv_ref, bdg_ref,
         wdn_ref, y_ref, lastv_ref, lastg_ref, h_ref, upv_ref, upg_ref, gate_ref, carry_ref) = refs
    i = pl.program_id(0)
    j = pl.program_id(1)

    @pl.when(j == 0)
    def _():
        _rms_to(x_ref, g_ref, h_ref, tm)
        y_ref[...] = x_ref[...]

    hist = SUBLANES
    upv_ref[hist:hist + tm, :] = jnp.dot(h_ref[...], wuv_ref[...], preferred_element_type=F32)
    upg_ref[hist:hist + tm, :] = jnp.dot(h_ref[...], wug_ref[...], preferred_element_type=F32)

    if sample:
        lastv_ref[...] = upv_ref[hist:hist + tm, :]
        lastg_ref[...] = upg_ref[hist:hist + tm, :]
    else:
        first = (i % blocks_per_seq) == 0

        @pl.when(first)
        def _():
            upv_ref[0:hist, :] = jnp.zeros((hist, tf), F32)
            upg_ref[0:hist, :] = jnp.zeros((hist, tf), F32)

        @pl.when(jnp.logical_not(first))
        def _():
            upv_ref[0:hist, :] = carry_ref[j, 0]
            upg_ref[0:hist, :] = carry_ref[j, 1]

        for half, (up_ref, last_ref) in enumerate(((upv_ref, lastv_ref), (upg_ref, lastg_ref))):
            tail = up_ref[tm:tm + hist, :]
            carry_ref[j, half] = tail
            last_ref[0] = tail

    def conv(up_ref, st_ref, wd_ref, bd_ref, r):
        w0 = wd_ref[0:1, :]
        w1 = wd_ref[1:2, :]
        w2 = wd_ref[2:3, :]
        cur = up_ref[pl.ds(hist + r, ROWS), :]
        if sample:
            n = ROWS // SUBLANES
            s0 = pl.multiple_of(r // SUBLANES, n)
            cur3 = cur.reshape(n, SUBLANES, tf)
            sub = lax.broadcasted_iota(jnp.int32, (n, SUBLANES, tf), 1)
            st0 = st_ref[pl.ds(s0, n), 0:1, :]
            st1 = st_ref[pl.ds(s0, n), 1:2, :]
            m1 = jnp.where(sub == 0, st1, pltpu.roll(cur3, 1, 1))
            m2 = jnp.where(sub == 0, st0, jnp.where(sub == 1, st1, pltpu.roll(cur3, 2, 1)))
            out = w0 * m2 + w1 * m1 + w2 * cur3 + bd_ref[...]
            return out.reshape(ROWS, tf)
        win = up_ref[pl.ds(r, ROWS + hist), :]
        m1 = win[hist - 1:hist - 1 + ROWS, :]
        m2 = win[hist - 2:hist - 2 + ROWS, :]
        return w0 * m2 + w1 * m1 + w2 * cur + bd_ref[...]

    def piece(r):
        cv = conv(upv_ref, stv_ref if sample else None, wdv_ref, bdv_ref, r)
        cg = conv(upg_ref, stg_ref if sample else None, wdg_ref, bdg_ref, r)
        gate_ref[pl.ds(r, ROWS), :] = (cv * jax.nn.sigmoid(cv) * cg).astype(BF16)
    _row_loop(tm, ROWS, piece)

    y_ref[...] += jnp.dot(gate_ref[...], wdn_ref[...], preferred_element_type=F32)


def _ffn(x, state, seq_len, params, tm, tf):
    g_ffn, w_up, w_dw, b_dw, w_down = params
    rows = x.shape[0]
    sample = state is not None
    nj = D_FF // tf
    grid = (rows // tm, nj)
    row_map = lambda i, j: (i, 0)
    val_map = lambda i, j: (0, j)
    gate_map = lambda i, j: (0, nj + j)
    in_specs = [pl.BlockSpec((tm, D_MODEL), row_map)]
    args = [x]
    if sample:
        spt = tm // seq_len
        in_specs += [pl.BlockSpec((spt, FFN_CONV - 1, tf), lambda i, j: (i, 0, j)),
                     pl.BlockSpec((spt, FFN_CONV - 1, tf), lambda i, j: (i, 0, nj + j))]
        args += [state, state]
    in_specs += [pl.BlockSpec((1, D_MODEL), lambda i, j: (0, 0)),
                 pl.BlockSpec((D_MODEL, tf), val_map), pl.BlockSpec((D_MODEL, tf), gate_map),
                 pl.BlockSpec((FFN_CONV, tf), val_map), pl.BlockSpec((FFN_CONV, tf), gate_map),
                 pl.BlockSpec((1, tf), val_map), pl.BlockSpec((1, tf), gate_map),
                 pl.BlockSpec((tf, D_MODEL), lambda i, j: (j, 0))]
    args += [g_ffn, w_up, w_up, w_dw, w_dw, b_dw, b_dw, w_down]
    scratch = [pltpu.VMEM((tm, D_MODEL), BF16),
               pltpu.VMEM((tm + SUBLANES, tf), F32), pltpu.VMEM((tm + SUBLANES, tf), F32),
               pltpu.VMEM((tm, tf), BF16)]
    if sample:
        last_specs = [pl.BlockSpec((tm, tf), lambda i, j: (i, j))] * 2
        last_shape = [jax.ShapeDtypeStruct((rows, D_FF), F32)] * 2
        blocks_per_seq = 1
    else:
        blocks_per_seq = seq_len // tm
        last_specs = [pl.BlockSpec((1, SUBLANES, tf), lambda i, j: (i, 0, j))] * 2
        last_shape = [jax.ShapeDtypeStruct((rows // tm, SUBLANES, D_FF), F32)] * 2
        scratch += [pltpu.VMEM((nj, 2, SUBLANES, tf), F32)]
    return pl.pallas_call(
        functools.partial(_ffn_body, sample, tm, tf, blocks_per_seq),
        grid=grid, in_specs=in_specs,
        out_specs=[pl.BlockSpec((tm, D_MODEL), row_map)] + last_specs,
        out_shape=[jax.ShapeDtypeStruct((rows, D_MODEL), F32)] + last_shape,
        scratch_shapes=scratch,
        compiler_params=pltpu.CompilerParams(dimension_semantics=("arbitrary", "arbitrary"),
                                             vmem_limit_bytes=VMEM_LIMIT),
        name="ffn_sample" if sample else "ffn_prompt",
    )(*args)


def _ple_body(tm, x_ref, p_ref, g_ref, wg_ref, wp_ref, gf_ref, y_ref, h_ref, gate_ref, proj_ref):
    _rms_to(x_ref, g_ref, h_ref, tm)
    gate_ref[...] = jnp.dot(h_ref[...], wg_ref[...], preferred_element_type=F32)
    proj_ref[...] = jnp.dot(p_ref[...].astype(BF16), wp_ref[...], preferred_element_type=F32)

    def piece(r):
        sl = pl.ds(r, ROWS)
        xv = x_ref[sl, :] + jax.nn.sigmoid(gate_ref[sl, :]) * proj_ref[sl, :]
        ms = jnp.mean(xv * xv, axis=-1, keepdims=True)
        y_ref[sl, :] = xv * lax.rsqrt(ms + EPS) * gf_ref[...]
    _row_loop(tm, ROWS, piece)


def _ple(x, p, params, tm, name):
    g_ple, w_gate, w_proj, g_final = params
    rows = x.shape[0]
    row_map = lambda i: (i, 0)
    return pl.pallas_call(
        functools.partial(_ple_body, tm),
        grid=(rows // tm,),
        in_specs=[pl.BlockSpec((tm, D_MODEL), row_map), pl.BlockSpec((tm, D_PLE), row_map),
                  _resident((1, D_MODEL)), _resident((D_MODEL, D_MODEL)),
                  _resident((D_PLE, D_MODEL)), _resident((1, D_MODEL))],
        out_specs=pl.BlockSpec((tm, D_MODEL), row_map),
        out_shape=jax.ShapeDtypeStruct((rows, D_MODEL), F32),
        scratch_shapes=[pltpu.VMEM((tm, D_MODEL), BF16), pltpu.VMEM((tm, D_MODEL), F32),
                        pltpu.VMEM((tm, D_MODEL), F32)],
        compiler_params=pltpu.CompilerParams(dimension_semantics=("arbitrary",),
                                             vmem_limit_bytes=VMEM_LIMIT),
        name=name,
    )(x, p, g_ple, w_gate, w_proj, g_final)


def kernel(x_prompt, x_sample, p_prompt, p_sample, state_conv_a, state_ffn_conv, g_mix, w_in, w_dw_a, b_dw_a, g_ln_a, b_ln_a, g_ln_v, b_ln_v, w_s, b_s, w_out, g_ffn, w_up, w_dw_f, b_dw_f, w_down, g_ple, w_ple_gate, w_ple_proj, g_final):
    depth = w_in.shape[0]
    assert depth == 1, "single-layer step"
    batch, seq, _ = x_prompt.shape
    dec_batch, dec_seq, _ = x_sample.shape
    row = lambda v: v.reshape(1, -1)

    def bias_rows(length):
        b = jnp.tile(b_s[0][:, :length], (1, CHUNK // length))
        return jnp.repeat(b.T, HEAD_DIM, axis=1)

    def mix_weights(length):
        return jnp.tile(w_s[0][:, :length, :length], (1, CHUNK // length, CHUNK // length))

    w_dw_pad = jnp.pad(w_dw_a[0], ((0, HIST_A - CONV_A), (0, 0)))
    mixer_common = (row(g_mix[0]), w_in[0].astype(BF16), w_dw_pad, row(b_dw_a[0]),
                    row(g_ln_a[0]), row(b_ln_a[0]), row(g_ln_v[0]), row(b_ln_v[0]))
    w_out_b = w_out[0].astype(BF16)
    ffn_params = (row(g_ffn[0]), w_up[0].astype(BF16), w_dw_f[0], row(b_dw_f[0]),
                  w_down[0].astype(BF16))
    ple_params = (row(g_ple[0]), w_ple_gate[0].astype(BF16), w_ple_proj[0].astype(BF16),
                  row(g_final))

    xp = x_prompt.reshape(batch * seq, D_MODEL)
    lp = min(seq, CHUNK)
    xp, conv_p, cv_p = _mixer(xp, None, seq,
                              mixer_common + (mix_weights(lp), bias_rows(lp), w_out_b), tm=256)
    xp, lastv_p, lastg_p = _ffn(xp, None, seq, ffn_params, tm=512, tf=512)
    yp = _ple(xp, p_prompt[0].reshape(batch * seq, D_PLE), ple_params, 512, "ple_prompt")

    xs = x_sample.reshape(dec_batch * dec_seq, D_MODEL)
    ls = min(dec_seq, CHUNK)
    xs, a_s, cv_s = _mixer(xs, state_conv_a[0], dec_seq,
                           mixer_common + (mix_weights(ls), bias_rows(ls), w_out_b), tm=256)
    xs, upv_s, upg_s = _ffn(xs, state_ffn_conv[0], dec_seq, ffn_params, tm=512, tf=512)
    ys = _ple(xs, p_sample[0].reshape(dec_batch * dec_seq, D_PLE), ple_params, 512, "ple_sample")

    keep = FFN_CONV - 1
    conv_a_prompt = conv_p[:, HIST_A - (CONV_A - 1):]
    conv_a_sample = jnp.concatenate(
        [state_conv_a[0][:, dec_seq:], a_s.reshape(dec_batch, dec_seq, C_A)], axis=1)
    per_seq = lastv_p.shape[0] // batch
    ffn_prompt = jnp.concatenate([lastv_p, lastg_p], axis=-1)[per_seq - 1::per_seq, SUBLANES - keep:]
    ffn_sample = jnp.concatenate([upv_s, upg_s], axis=-1).reshape(
        dec_batch, dec_seq, 2 * D_FF)[:, dec_seq - keep:]
    return (yp.reshape(batch, seq, D_MODEL), ys.reshape(dec_batch, dec_seq, D_MODEL),
            conv_a_prompt[None], conv_a_sample[None], ffn_prompt[None], ffn_sample[None],
            cv_p[None], cv_s.reshape(dec_batch, dec_seq, C_B)[None])
```

```python
import functools

import jax
import jax.numpy as jnp
from jax import lax
from jax.experimental import pallas as pl
from jax.experimental.pallas import tpu as pltpu

D_MODEL = 2048
HEAD_DIM = 128
C_A = D_MODEL // 2
C_B = D_MODEL - C_A
N_HEADS = C_B // HEAD_DIM
CONV_A = 31
HIST_A = 32
CHUNK = 128
FFN_CONV = 3
D_FF = 5632
D_PLE = 256
EPS = 1e-6

SUBLANES = 8
ROWS = 32
VMEM_LIMIT = 56 * 1024 * 1024

F32 = jnp.float32
BF16 = jnp.bfloat16


def _resident(shape):
    return pl.BlockSpec(shape, lambda *_: (0,) * len(shape), pipeline_mode=pl.Buffered(1))


def _aligned(x, m):
    return x if isinstance(x, int) else pl.multiple_of(x, m)


def _row_loop(n_rows, step, fn):
    def body(c, carry):
        fn(pl.multiple_of(c * step, step))
        return carry
    lax.fori_loop(0, n_rows // step, body, 0)


def _rms_to(x_ref, g_ref, dst_ref, n_rows):
    def piece(r):
        xv = x_ref[pl.ds(r, ROWS), :]
        ms = jnp.mean(xv * xv, axis=-1, keepdims=True)
        dst_ref[pl.ds(r, ROWS), :] = (xv * lax.rsqrt(ms + EPS) * g_ref[...]).astype(BF16)
    _row_loop(n_rows, ROWS, piece)


def _head_ln(x, g, b):
    outs = []
    for h in range(x.shape[-1] // HEAD_DIM):
        xh = x[:, h * HEAD_DIM:(h + 1) * HEAD_DIM]
        mu = jnp.mean(xh, axis=-1, keepdims=True)
        xc = xh - mu
        var = jnp.mean(xc * xc, axis=-1, keepdims=True)
        outs.append(xc * lax.rsqrt(var + EPS))
    return jnp.concatenate(outs, axis=-1) * g + b


def _mixer_body(sample, tm, seq_len, *refs):
    if sample:
        (x_ref, st_ref, gmix_ref, win_ref, wdw_ref, bdw_ref, glna_ref, blna_ref, glnv_ref,
         blnv_ref, ws_ref, bsb_ref, wout_ref,
         y_ref, aout_ref, vout_ref,
         h_ref, z_ref, cat_ref, a_ref, conv_ref, tail_ref, vn_ref, wm_ref) = refs
    else:
        (x_ref, gmix_ref, win_ref, wdw_ref, bdw_ref, glna_ref, blna_ref, glnv_ref,
         blnv_ref, ws_ref, bsb_ref, wout_ref,
         y_ref, aout_ref, vout_ref,
         h_ref, z_ref, cat_ref, a_ref, vn_ref, wm_ref) = refs

    _rms_to(x_ref, gmix_ref, h_ref, tm)
    z_ref[...] = jnp.dot(h_ref[...], win_ref[...], preferred_element_type=F32)

    ti = lax.broadcasted_iota(jnp.int32, (CHUNK, CHUNK), 0)
    si = lax.broadcasted_iota(jnp.int32, (CHUNK, CHUNK), 1)
    if seq_len >= CHUNK:
        mask = si <= ti
    else:
        mask = ((si // seq_len) == (ti // seq_len)) & ((si % seq_len) <= (ti % seq_len))
    for h in range(N_HEADS):
        wm_ref[h] = jnp.where(mask, ws_ref[h], 0.0).astype(BF16)

    if not sample:
        t = pl.program_id(1)

        @pl.when(t == 0)
        def _():
            a_ref[0:HIST_A, :] = jnp.zeros((HIST_A, C_A), F32)

        @pl.when(t > 0)
        def _():
            a_ref[0:HIST_A, :] = a_ref[tm:tm + HIST_A, :]

    a_base = 0 if sample else HIST_A

    def chunk(c, carry):
        r0 = pl.multiple_of(c * CHUNK, CHUNK)

        def glu_piece(r):
            zv = z_ref[pl.ds(r0 + r, ROWS), 0:C_A]
            zg = z_ref[pl.ds(r0 + r, ROWS), C_A:2 * C_A]
            a = zv * jax.nn.sigmoid(zg)
            a_ref[pl.ds(a_base + r0 + r, ROWS), :] = a
            if sample:
                aout_ref[pl.ds(r0 + r, ROWS), :] = a
        _row_loop(CHUNK, ROWS, glu_piece)

        def finish_a(r, conv):
            y = _head_ln(conv, glna_ref[...], blna_ref[...])
            cat_ref[pl.ds(r0 + r, ROWS), 0:C_A] = (y * jax.nn.sigmoid(y)).astype(BF16)

        if sample:
            def seq_conv(q, carry2):
                rs = pl.multiple_of(r0 + q * seq_len, seq_len)
                s = c * (CHUNK // seq_len) + q
                tail_ref[0:8, :] = st_ref[s, CONV_A - 1 - 8:CONV_A - 1, :]
                tail_ref[8:16, :] = a_ref[pl.ds(rs, seq_len), :]
                acc = jnp.broadcast_to(bdw_ref[...], (seq_len, C_A))
                for k in range(CONV_A):
                    if k + seq_len <= CONV_A - 1:
                        win = st_ref[s, k:k + seq_len, :]
                    else:
                        o = k - (CONV_A - 1 - 8)
                        win = tail_ref[o:o + seq_len, :]
                    acc = acc + wdw_ref[k:k + 1, :] * win
                conv_ref[pl.ds(rs, seq_len), :] = acc
                return carry2
            lax.fori_loop(0, CHUNK // seq_len, seq_conv, 0)

            def ln_piece(r):
                finish_a(r, conv_ref[pl.ds(r0 + r, ROWS), :])
            _row_loop(CHUNK, ROWS, ln_piece)
        else:
            def conv_piece(r):
                win = a_ref[pl.ds(r0 + r, ROWS + HIST_A), :]
                acc = jnp.broadcast_to(bdw_ref[...], (ROWS, C_A))
                off = HIST_A - (CONV_A - 1)
                for k in range(CONV_A):
                    acc = acc + wdw_ref[k:k + 1, :] * win[off + k:off + k + ROWS, :]
                finish_a(r, acc)
            _row_loop(CHUNK, ROWS, conv_piece)

        def v_piece(r):
            v = jax.nn.gelu(z_ref[pl.ds(r0 + r, ROWS), 2 * C_A + C_B:2 * C_A + 2 * C_B])
            vn = _head_ln(v, glnv_ref[...], blnv_ref[...])
            vn_ref[pl.ds(r, ROWS), :] = vn.astype(BF16)
            if sample:
                vout_ref[pl.ds(r0 + r, ROWS), :] = vn
            else:
                vout_ref[0, pl.ds(r, ROWS), :] = vn
        _row_loop(CHUNK, ROWS, v_piece)

        for h in range(N_HEADS):
            lo = h * HEAD_DIM
            mixed = jnp.dot(wm_ref[h], vn_ref[:, lo:lo + HEAD_DIM], preferred_element_type=F32)
            mixed = mixed + bsb_ref[:, lo:lo + HEAD_DIM]
            u = jax.nn.gelu(z_ref[pl.ds(r0, CHUNK), 2 * C_A + lo:2 * C_A + lo + HEAD_DIM])
            cat_ref[pl.ds(r0, CHUNK), C_A + lo:C_A + lo + HEAD_DIM] = (u * mixed).astype(BF16)
        return carry

    lax.fori_loop(0, tm // CHUNK, chunk, 0)

    if not sample:
        aout_ref[0] = a_ref[tm:tm + HIST_A, :]

    y_ref[...] = x_ref[...] + jnp.dot(cat_ref[...], wout_ref[...], preferred_element_type=F32)


def _mixer(x, state, seq_len, params, tm):
    (g_mix, w_in, w_dw, b_dw, g_ln_a, b_ln_a, g_ln_v, b_ln_v, w_s, bias_rows, w_out) = params
    rows = x.shape[0]
    sample = state is not None
    vec = lambda n: _resident((1, n))
    common_in = [vec(D_MODEL), _resident((D_MODEL, 2 * C_A + 2 * C_B)), _resident((HIST_A, C_A)),
                 vec(C_A), vec(C_A), vec(C_A), vec(C_B), vec(C_B),
                 _resident((N_HEADS, CHUNK, CHUNK)), _resident((CHUNK, C_B)),
                 _resident((C_A + C_B, D_MODEL))]
    common_scratch = [pltpu.VMEM((tm, D_MODEL), BF16),
                      pltpu.VMEM((tm, 2 * C_A + 2 * C_B), F32),
                      pltpu.VMEM((tm, C_A + C_B), BF16)]
    tail_scratch = [pltpu.VMEM((CHUNK, C_B), BF16),
                    pltpu.VMEM((N_HEADS, CHUNK, CHUNK), BF16)]
    if sample:
        n_seq = rows // seq_len
        spt = tm // seq_len
        grid = (rows // tm,)
        row_map = lambda i: (i, 0)
        in_specs = [pl.BlockSpec((tm, D_MODEL), row_map),
                    pl.BlockSpec((spt, CONV_A - 1, C_A), lambda i: (i, 0, 0))] + common_in
        out_specs = [pl.BlockSpec((tm, D_MODEL), row_map),
                     pl.BlockSpec((tm, C_A), row_map),
                     pl.BlockSpec((tm, C_B), row_map)]
        out_shape = [jax.ShapeDtypeStruct((rows, D_MODEL), F32),
                     jax.ShapeDtypeStruct((rows, C_A), F32),
                     jax.ShapeDtypeStruct((rows, C_B), F32)]
        scratch = common_scratch + [pltpu.VMEM((tm, C_A), F32),
                                    pltpu.VMEM((tm, C_A), F32),
                                    pltpu.VMEM((16, C_A), F32)] + tail_scratch
        args = (x, state)
        sem = ("arbitrary",)
        del n_seq
    else:
        n_seq = rows // seq_len
        nt = seq_len // tm
        grid = (n_seq, nt)
        row_map = lambda b, t: (b * nt + t, 0)
        in_specs = [pl.BlockSpec((tm, D_MODEL), row_map)] + common_in
        out_specs = [pl.BlockSpec((tm, D_MODEL), row_map),
                     pl.BlockSpec((1, HIST_A, C_A), lambda b, t: (b, 0, 0)),
                     pl.BlockSpec((1, CHUNK, C_B), lambda b, t: (b, 0, 0))]
        out_shape = [jax.ShapeDtypeStruct((rows, D_MODEL), F32),
                     jax.ShapeDtypeStruct((n_seq, HIST_A, C_A), F32),
                     jax.ShapeDtypeStruct((n_seq, CHUNK, C_B), F32)]
        scratch = common_scratch + [pltpu.VMEM((tm + HIST_A, C_A), F32)] + tail_scratch
        args = (x,)
        sem = ("arbitrary", "arbitrary")
    return pl.pallas_call(
        functools.partial(_mixer_body, sample, tm, seq_len),
        grid=grid, in_specs=in_specs, out_specs=out_specs, out_shape=out_shape,
        scratch_shapes=scratch,
        compiler_params=pltpu.CompilerParams(dimension_semantics=sem,
                                             vmem_limit_bytes=VMEM_LIMIT),
        name="mixer_sample" if sample else "mixer_prompt",
    )(*args, g_mix, w_in, w_dw, b_dw, g_ln_a, b_ln_a, g_ln_v, b_ln_v, w_s, bias_rows, w_out)


def _ffn_body(sample, tm, tf, rc, blocks_per_seq, *refs):
    if sample:
        (x_ref, stv_ref, stg_ref, g_ref, wuv_ref, wug_ref, wdv_ref, wdg_ref, bdv_ref, bdg_ref,
         wdn_ref, y_ref, lastv_ref, lastg_ref, h_ref, upv_ref, upg_ref, gate_ref) = refs
    else:
        (x_ref, g_ref, wuv_ref, wug_ref, wdv_ref, wdg_ref, bdv_ref, bdg_ref,
         wdn_ref, y_ref, lastv_ref, lastg_ref, h_ref, upv_ref, upg_ref, gate_ref, carry_ref) = refs
    i = pl.program_id(0)
    j = pl.program_id(1)

    @pl.when(j == 0)
    def _():
        _rms_to(x_ref, g_ref, h_ref, tm)
        y_ref[...] = x_ref[...]
        if not sample:
            @pl.when(i == 0)
            def _():
                carry_ref[...] = jnp.zeros(carry_ref.shape, F32)

    hist = SUBLANES
    n_chunks = tm // rc
    ups = ((upv_ref, wuv_ref, lastv_ref), (upg_ref, wug_ref, lastg_ref))

    if not sample:
        first = (i % blocks_per_seq) == 0
        for half, (up_ref, _, _) in enumerate(ups):
            up_ref[0:hist, :] = jnp.where(first, 0.0, carry_ref[j, half])

    def up_proj(c):
        r0 = _aligned(c * rc, rc)
        hc = h_ref[pl.ds(r0, rc), :]
        for up_ref, wu_ref, _ in ups:
            up_ref[pl.ds(hist + r0, rc), :] = jnp.dot(hc, wu_ref[...], preferred_element_type=F32)

    def conv(up_ref, st_ref, wd_ref, bd_ref, row):
        w0 = wd_ref[0:1, :]
        w1 = wd_ref[1:2, :]
        w2 = wd_ref[2:3, :]
        if sample:
            n = ROWS // SUBLANES
            s0 = _aligned(row // SUBLANES, n)
            cur3 = up_ref[pl.ds(hist + row, ROWS), :].reshape(n, SUBLANES, tf)
            sl = lax.broadcasted_iota(jnp.int32, (n, SUBLANES, tf), 1)
            st0 = st_ref[pl.ds(s0, n), 0:1, :]
            st1 = st_ref[pl.ds(s0, n), 1:2, :]
            m1 = jnp.where(sl == 0, st1, pltpu.roll(cur3, 1, 1))
            m2 = jnp.where(sl == 0, st0, jnp.where(sl == 1, st1, pltpu.roll(cur3, 2, 1)))
            out = w0 * m2 + w1 * m1 + w2 * cur3 + bd_ref[...]
            return out.reshape(ROWS, tf)
        win = up_ref[pl.ds(row, ROWS + hist), :]
        cur = win[hist:hist + ROWS, :]
        m1 = win[hist - 1:hist - 1 + ROWS, :]
        m2 = win[hist - 2:hist - 2 + ROWS, :]
        return w0 * m2 + w1 * m1 + w2 * cur + bd_ref[...]

    def gate_down(c):
        r0 = _aligned(c * rc, rc)
        for r in range(0, rc, ROWS):
            row = _aligned(r0 + r, ROWS)
            cv = conv(upv_ref, stv_ref if sample else None, wdv_ref, bdv_ref, row)
            cg = conv(upg_ref, stg_ref if sample else None, wdg_ref, bdg_ref, row)
            gate_ref[r:r + ROWS, :] = (cv * jax.nn.sigmoid(cv) * cg).astype(BF16)
        y_ref[pl.ds(r0, rc), :] += jnp.dot(gate_ref[...], wdn_ref[...],
                                           preferred_element_type=F32)

    up_proj(0)
    for c in range(n_chunks - 1):
        up_proj(c + 1)
        gate_down(c)
    gate_down(n_chunks - 1)

    for half, (up_ref, _, last_ref) in enumerate(ups):
        if sample:
            last_ref[...] = up_ref[hist:hist + tm, :]
        else:
            tail = up_ref[tm:tm + hist, :]
            carry_ref[j, half] = tail
            last_ref[0] = tail


def _ffn(x, state, seq_len, params, tm, tf, rc):
    g_ffn, w_up, w_dw, b_dw, w_down = params
    rows = x.shape[0]
    sample = state is not None
    nj = D_FF // tf
    grid = (rows // tm, nj)
    row_map = lambda i, j: (i, 0)
    val_map = lambda i, j: (0, j)
    gate_map = lambda i, j: (0, nj + j)
    in_specs = [pl.BlockSpec((tm, D_MODEL), row_map)]
    args = [x]
    if sample:
        spt = tm // seq_len
        in_specs += [pl.BlockSpec((spt, FFN_CONV - 1, tf), lambda i, j: (i, 0, j)),
                     pl.BlockSpec((spt, FFN_CONV - 1, tf), lambda i, j: (i, 0, nj + j))]
        args += [state, state]
    in_specs += [pl.BlockSpec((1, D_MODEL), lambda i, j: (0, 0)),
                 pl.BlockSpec((D_MODEL, tf), val_map), pl.BlockSpec((D_MODEL, tf), gate_map),
                 pl.BlockSpec((FFN_CONV, tf), val_map), pl.BlockSpec((FFN_CONV, tf), gate_map),
                 pl.BlockSpec((1, tf), val_map), pl.BlockSpec((1, tf), gate_map),
                 pl.BlockSpec((tf, D_MODEL), lambda i, j: (j, 0))]
    args += [g_ffn, w_up, w_up, w_dw, w_dw, b_dw, b_dw, w_down]
    scratch = [pltpu.VMEM((tm, D_MODEL), BF16),
               pltpu.VMEM((tm + SUBLANES, tf), F32), pltpu.VMEM((tm + SUBLANES, tf), F32),
               pltpu.VMEM((rc, tf), BF16)]
    if sample:
        last_specs = [pl.BlockSpec((tm, tf), lambda i, j: (i, j))] * 2
        last_shape = [jax.ShapeDtypeStruct((rows, D_FF), F32)] * 2
        blocks_per_seq = 1
    else:
        blocks_per_seq = seq_len // tm
        last_specs = [pl.BlockSpec((1, SUBLANES, tf), lambda i, j: (i, 0, j))] * 2
        last_shape = [jax.ShapeDtypeStruct((rows // tm, SUBLANES, D_FF), F32)] * 2
        scratch += [pltpu.VMEM((nj, 2, SUBLANES, tf), F32)]
    return pl.pallas_call(
        functools.partial(_ffn_body, sample, tm, tf, rc, blocks_per_seq),
        grid=grid, in_specs=in_specs,
        out_specs=[pl.BlockSpec((tm, D_MODEL), row_map)] + last_specs,
        out_shape=[jax.ShapeDtypeStruct((rows, D_MODEL), F32)] + last_shape,
        scratch_shapes=scratch,
        compiler_params=pltpu.CompilerParams(dimension_semantics=("arbitrary", "arbitrary"),
                                             vmem_limit_bytes=VMEM_LIMIT),
        name="ffn_sample" if sample else "ffn_prompt",
    )(*args)


def _ple_body(tm, x_ref, p_ref, g_ref, wg_ref, wp_ref, gf_ref, y_ref, h_ref, gate_ref, proj_ref):
    _rms_to(x_ref, g_ref, h_ref, tm)
    gate_ref[...] = jnp.dot(h_ref[...], wg_ref[...], preferred_element_type=F32)
    proj_ref[...] = jnp.dot(p_ref[...].astype(BF16), wp_ref[...], preferred_element_type=F32)

    def piece(r):
        sl = pl.ds(r, ROWS)
        xv = x_ref[sl, :] + jax.nn.sigmoid(gate_ref[sl, :]) * proj_ref[sl, :]
        ms = jnp.mean(xv * xv, axis=-1, keepdims=True)
        y_ref[sl, :] = xv * lax.rsqrt(ms + EPS) * gf_ref[...]
    _row_loop(tm, ROWS, piece)


def _ple(x, p, params, tm, name):
    g_ple, w_gate, w_proj, g_final = params
    rows = x.shape[0]
    row_map = lambda i: (i, 0)
    return pl.pallas_call(
        functools.partial(_ple_body, tm),
        grid=(rows // tm,),
        in_specs=[pl.BlockSpec((tm, D_MODEL), row_map), pl.BlockSpec((tm, D_PLE), row_map),
                  _resident((1, D_MODEL)), _resident((D_MODEL, D_MODEL)),
                  _resident((D_PLE, D_MODEL)), _resident((1, D_MODEL))],
        out_specs=pl.BlockSpec((tm, D_MODEL), row_map),
        out_shape=jax.ShapeDtypeStruct((rows, D_MODEL), F32),
        scratch_shapes=[pltpu.VMEM((tm, D_MODEL), BF16), pltpu.VMEM((tm, D_MODEL), F32),
                        pltpu.VMEM((tm, D_MODEL), F32)],
        compiler_params=pltpu.CompilerParams(dimension_semantics=("arbitrary",),
                                             vmem_limit_bytes=VMEM_LIMIT),
        name=name,
    )(x, p, g_ple, w_gate, w_proj, g_final)


def kernel(x_prompt, x_sample, p_prompt, p_sample, state_conv_a, state_ffn_conv, g_mix, w_in, w_dw_a, b_dw_a, g_ln_a, b_ln_a, g_ln_v, b_ln_v, w_s, b_s, w_out, g_ffn, w_up, w_dw_f, b_dw_f, w_down, g_ple, w_ple_gate, w_ple_proj, g_final):
    depth = w_in.shape[0]
    assert depth == 1, "single-layer step"
    batch, seq, _ = x_prompt.shape
    dec_batch, dec_seq, _ = x_sample.shape
    row = lambda v: v.reshape(1, -1)

    def bias_rows(length):
        b = jnp.tile(b_s[0][:, :length], (1, CHUNK // length))
        return jnp.repeat(b.T, HEAD_DIM, axis=1)

    def mix_weights(length):
        return jnp.tile(w_s[0][:, :length, :length], (1, CHUNK // length, CHUNK // length))

    w_dw_pad = jnp.pad(w_dw_a[0], ((0, HIST_A - CONV_A), (0, 0)))
    mixer_common = (row(g_mix[0]), w_in[0].astype(BF16), w_dw_pad, row(b_dw_a[0]),
                    row(g_ln_a[0]), row(b_ln_a[0]), row(g_ln_v[0]), row(b_ln_v[0]))
    w_out_b = w_out[0].astype(BF16)
    ffn_params = (row(g_ffn[0]), w_up[0].astype(BF16), w_dw_f[0], row(b_dw_f[0]),
                  w_down[0].astype(BF16))
    ple_params = (row(g_ple[0]), w_ple_gate[0].astype(BF16), w_ple_proj[0].astype(BF16),
                  row(g_final))

    xp = x_prompt.reshape(batch * seq, D_MODEL)
    lp = min(seq, CHUNK)
    xp, conv_p, cv_p = _mixer(xp, None, seq,
                              mixer_common + (mix_weights(lp), bias_rows(lp), w_out_b), tm=256)
    xp, lastv_p, lastg_p = _ffn(xp, None, seq, ffn_params, tm=512, tf=512, rc=128)
    yp = _ple(xp, p_prompt[0].reshape(batch * seq, D_PLE), ple_params, 512, "ple_prompt")

    xs = x_sample.reshape(dec_batch * dec_seq, D_MODEL)
    ls = min(dec_seq, CHUNK)
    xs, a_s, cv_s = _mixer(xs, state_conv_a[0], dec_seq,
                           mixer_common + (mix_weights(ls), bias_rows(ls), w_out_b), tm=256)
    xs, upv_s, upg_s = _ffn(xs, state_ffn_conv[0], dec_seq, ffn_params, tm=512, tf=512, rc=128)
    ys = _ple(xs, p_sample[0].reshape(dec_batch * dec_seq, D_PLE), ple_params, 512, "ple_sample")

    keep = FFN_CONV - 1
    conv_a_prompt = conv_p[:, HIST_A - (CONV_A - 1):]
    conv_a_sample = jnp.concatenate(
        [state_conv_a[0][:, dec_seq:], a_s.reshape(dec_batch, dec_seq, C_A)], axis=1)
    per_seq = lastv_p.shape[0] // batch
    ffn_prompt = jnp.concatenate([lastv_p, lastg_p], axis=-1)[per_seq - 1::per_seq, SUBLANES - keep:]
    ffn_sample = jnp.concatenate([upv_s, upg_s], axis=-1).reshape(
        dec_batch, dec_seq, 2 * D_FF)[:, dec_seq - keep:]
    return (yp.reshape(batch, seq, D_MODEL), ys.reshape(dec_batch, dec_seq, D_MODEL),
            conv_a_prompt[None], conv_a_sample[None], ffn_prompt[None], ffn_sample[None],
            cv_p[None], cv_s.reshape(dec_batch, dec_seq, C_B)[None])
```

```python
import functools

import jax
import jax.numpy as jnp
from jax import lax
from jax.experimental import pallas as pl
from jax.experimental.pallas import tpu as pltpu

D_MODEL = 2048
HEAD_DIM = 128
C_A = D_MODEL // 2
C_B = D_MODEL - C_A
N_HEADS = C_B // HEAD_DIM
CONV_A = 31
HIST_A = 32
CHUNK = 128
FFN_CONV = 3
D_FF = 5632
D_PLE = 256
EPS = 1e-6

SUBLANES = 8
ROWS = 32
VMEM_LIMIT = 56 * 1024 * 1024

F32 = jnp.float32
BF16 = jnp.bfloat16


def _resident(shape):
    return pl.BlockSpec(shape, lambda *_: (0,) * len(shape), pipeline_mode=pl.Buffered(1))


def _aligned(x, m):
    return x if isinstance(x, int) else pl.multiple_of(x, m)


def _row_loop(n_rows, step, fn):
    def body(c, carry):
        fn(pl.multiple_of(c * step, step))
        return carry
    lax.fori_loop(0, n_rows // step, body, 0)


def _rms_to(x_ref, g_ref, dst_ref, n_rows):
    def piece(r):
        xv = x_ref[pl.ds(r, ROWS), :]
        ms = jnp.mean(xv * xv, axis=-1, keepdims=True)
        dst_ref[pl.ds(r, ROWS), :] = (xv * lax.rsqrt(ms + EPS) * g_ref[...]).astype(BF16)
    _row_loop(n_rows, ROWS, piece)


def _head_ln(x, g, b):
    outs = []
    for h in range(x.shape[-1] // HEAD_DIM):
        xh = x[:, h * HEAD_DIM:(h + 1) * HEAD_DIM]
        mu = jnp.mean(xh, axis=-1, keepdims=True)
        xc = xh - mu
        var = jnp.mean(xc * xc, axis=-1, keepdims=True)
        outs.append(xc * lax.rsqrt(var + EPS))
    return jnp.concatenate(outs, axis=-1) * g + b


def _mixer_body(sample, tm, seq_len, *refs):
    if sample:
        (x_ref, st_ref, gmix_ref, win_ref, wdw_ref, bdw_ref, glna_ref, blna_ref, glnv_ref,
         blnv_ref, ws_ref, bsb_ref, wout_ref,
         y_ref, aout_ref, vout_ref,
         h_ref, z_ref, cat_ref, a_ref, conv_ref, tail_ref, vn_ref, wm_ref) = refs
    else:
        (x_ref, gmix_ref, win_ref, wdw_ref, bdw_ref, glna_ref, blna_ref, glnv_ref,
         blnv_ref, ws_ref, bsb_ref, wout_ref,
         y_ref, aout_ref, vout_ref,
         h_ref, z_ref, cat_ref, a_ref, vn_ref, wm_ref) = refs

    _rms_to(x_ref, gmix_ref, h_ref, tm)
    z_ref[...] = jnp.dot(h_ref[...], win_ref[...], preferred_element_type=F32)

    ti = lax.broadcasted_iota(jnp.int32, (CHUNK, CHUNK), 0)
    si = lax.broadcasted_iota(jnp.int32, (CHUNK, CHUNK), 1)
    if seq_len >= CHUNK:
        mask = si <= ti
    else:
        mask = ((si // seq_len) == (ti // seq_len)) & ((si % seq_len) <= (ti % seq_len))
    for h in range(N_HEADS):
        wm_ref[h] = jnp.where(mask, ws_ref[h], 0.0).astype(BF16)

    if not sample:
        t = pl.program_id(1)

        @pl.when(t == 0)
        def _():
            a_ref[0:HIST_A, :] = jnp.zeros((HIST_A, C_A), F32)

        @pl.when(t > 0)
        def _():
            a_ref[0:HIST_A, :] = a_ref[tm:tm + HIST_A, :]

    a_base = 0 if sample else HIST_A

    def chunk(c, carry):
        r0 = pl.multiple_of(c * CHUNK, CHUNK)

        def glu_piece(r):
            zv = z_ref[pl.ds(r0 + r, ROWS), 0:C_A]
            zg = z_ref[pl.ds(r0 + r, ROWS), C_A:2 * C_A]
            a = zv * jax.nn.sigmoid(zg)
            a_ref[pl.ds(a_base + r0 + r, ROWS), :] = a
            if sample:
                aout_ref[pl.ds(r0 + r, ROWS), :] = a
        _row_loop(CHUNK, ROWS, glu_piece)

        def finish_a(r, conv):
            y = _head_ln(conv, glna_ref[...], blna_ref[...])
            cat_ref[pl.ds(r0 + r, ROWS), 0:C_A] = (y * jax.nn.sigmoid(y)).astype(BF16)

        if sample:
            def seq_conv(q, carry2):
                rs = pl.multiple_of(r0 + q * seq_len, seq_len)
                s = c * (CHUNK // seq_len) + q
                tail_ref[0:8, :] = st_ref[s, CONV_A - 1 - 8:CONV_A - 1, :]
                tail_ref[8:16, :] = a_ref[pl.ds(rs, seq_len), :]
                acc = jnp.broadcast_to(bdw_ref[...], (seq_len, C_A))
                for k in range(CONV_A):
                    if k + seq_len <= CONV_A - 1:
                        win = st_ref[s, k:k + seq_len, :]
                    else:
                        o = k - (CONV_A - 1 - 8)
                        win = tail_ref[o:o + seq_len, :]
                    acc = acc + wdw_ref[k:k + 1, :] * win
                conv_ref[pl.ds(rs, seq_len), :] = acc
                return carry2
            lax.fori_loop(0, CHUNK // seq_len, seq_conv, 0)

            def ln_piece(r):
                finish_a(r, conv_ref[pl.ds(r0 + r, ROWS), :])
            _row_loop(CHUNK, ROWS, ln_piece)
        else:
            def conv_piece(r):
                win = a_ref[pl.ds(r0 + r, ROWS + HIST_A), :]
                acc = jnp.broadcast_to(bdw_ref[...], (ROWS, C_A))
                off = HIST_A - (CONV_A - 1)
                for k in range(CONV_A):
                    acc = acc + wdw_ref[k:k + 1, :] * win[off + k:off + k + ROWS, :]
                finish_a(r, acc)
            _row_loop(CHUNK, ROWS, conv_piece)

        def v_piece(r):
            v = jax.nn.gelu(z_ref[pl.ds(r0 + r, ROWS), 2 * C_A + C_B:2 * C_A + 2 * C_B])
            vn = _head_ln(v, glnv_ref[...], blnv_ref[...])
            vn_ref[pl.ds(r, ROWS), :] = vn.astype(BF16)
            if sample:
                vout_ref[pl.ds(r0 + r, ROWS), :] = vn
            else:
                vout_ref[0, pl.ds(r, ROWS), :] = vn
        _row_loop(CHUNK, ROWS, v_piece)

        for h in range(N_HEADS):
            lo = h * HEAD_DIM
            mixed = jnp.dot(wm_ref[h], vn_ref[:, lo:lo + HEAD_DIM], preferred_element_type=F32)
            mixed = mixed + bsb_ref[:, lo:lo + HEAD_DIM]
            u = jax.nn.gelu(z_ref[pl.ds(r0, CHUNK), 2 * C_A + lo:2 * C_A + lo + HEAD_DIM])
            cat_ref[pl.ds(r0, CHUNK), C_A + lo:C_A + lo + HEAD_DIM] = (u * mixed).astype(BF16)
        return carry

    lax.fori_loop(0, tm // CHUNK, chunk, 0)

    if not sample:
        aout_ref[0] = a_ref[tm:tm + HIST_A, :]

    y_ref[...] = x_ref[...] + jnp.dot(cat_ref[...], wout_ref[...], preferred_element_type=F32)


def _mixer(x, state, seq_len, params, tm):
    (g_mix, w_in, w_dw, b_dw, g_ln_a, b_ln_a, g_ln_v, b_ln_v, w_s, bias_rows, w_out) = params
    rows = x.shape[0]
    sample = state is not None
    vec = lambda n: _resident((1, n))
    common_in = [vec(D_MODEL), _resident((D_MODEL, 2 * C_A + 2 * C_B)), _resident((HIST_A, C_A)),
                 vec(C_A), vec(C_A), vec(C_A), vec(C_B), vec(C_B),
                 _resident((N_HEADS, CHUNK, CHUNK)), _resident((CHUNK, C_B)),
                 _resident((C_A + C_B, D_MODEL))]
    common_scratch = [pltpu.VMEM((tm, D_MODEL), BF16),
                      pltpu.VMEM((tm, 2 * C_A + 2 * C_B), F32),
                      pltpu.VMEM((tm, C_A + C_B), BF16)]
    tail_scratch = [pltpu.VMEM((CHUNK, C_B), BF16),
                    pltpu.VMEM((N_HEADS, CHUNK, CHUNK), BF16)]
    if sample:
        n_seq = rows // seq_len
        spt = tm // seq_len
        grid = (rows // tm,)
        row_map = lambda i: (i, 0)
        in_specs = [pl.BlockSpec((tm, D_MODEL), row_map),
                    pl.BlockSpec((spt, CONV_A - 1, C_A), lambda i: (i, 0, 0))] + common_in
        out_specs = [pl.BlockSpec((tm, D_MODEL), row_map),
                     pl.BlockSpec((tm, C_A), row_map),
                     pl.BlockSpec((tm, C_B), row_map)]
        out_shape = [jax.ShapeDtypeStruct((rows, D_MODEL), F32),
                     jax.ShapeDtypeStruct((rows, C_A), F32),
                     jax.ShapeDtypeStruct((rows, C_B), F32)]
        scratch = common_scratch + [pltpu.VMEM((tm, C_A), F32),
                                    pltpu.VMEM((tm, C_A), F32),
                                    pltpu.VMEM((16, C_A), F32)] + tail_scratch
        args = (x, state)
        sem = ("arbitrary",)
        del n_seq
    else:
        n_seq = rows // seq_len
        nt = seq_len // tm
        grid = (n_seq, nt)
        row_map = lambda b, t: (b * nt + t, 0)
        in_specs = [pl.BlockSpec((tm, D_MODEL), row_map)] + common_in
        out_specs = [pl.BlockSpec((tm, D_MODEL), row_map),
                     pl.BlockSpec((1, HIST_A, C_A), lambda b, t: (b, 0, 0)),
                     pl.BlockSpec((1, CHUNK, C_B), lambda b, t: (b, 0, 0))]
        out_shape = [jax.ShapeDtypeStruct((rows, D_MODEL), F32),
                     jax.ShapeDtypeStruct((n_seq, HIST_A, C_A), F32),
                     jax.ShapeDtypeStruct((n_seq, CHUNK, C_B), F32)]
        scratch = common_scratch + [pltpu.VMEM((tm + HIST_A, C_A), F32)] + tail_scratch
        args = (x,)
        sem = ("arbitrary", "arbitrary")
    return pl.pallas_call(
        functools.partial(_mixer_body, sample, tm, seq_len),
        grid=grid, in_specs=in_specs, out_specs=out_specs, out_shape=out_shape,
        scratch_shapes=scratch,
        compiler_params=pltpu.CompilerParams(dimension_semantics=sem,
                                             vmem_limit_bytes=VMEM_LIMIT),
        name="mixer_sample" if sample else "mixer_prompt",
    )(*args, g_mix, w_in, w_dw, b_dw, g_ln_a, b_ln_a, g_ln_v, b_ln_v, w_s, bias_rows, w_out)


def _ffn_body(sample, tm, tf, rc, blocks_per_seq, *refs):
    if sample:
        (x_ref, stv_ref, stg_ref, g_ref, wuv_ref, wug_ref, wdv_ref, wdg_ref, bdv_ref, bdg_ref,
         wdn_ref, y_ref, lastv_ref, lastg_ref, h_ref, upv_ref, upg_ref, gate_ref) = refs
    else:
        (x_ref, g_ref, wuv_ref, wug_ref, wdv_ref, wdg_ref, bdv_ref, bdg_ref,
         wdn_ref, y_ref, lastv_ref, lastg_ref, h_ref, upv_ref, upg_ref, gate_ref, carry_ref) = refs
    i = pl.program_id(0)
    j = pl.program_id(1)

    @pl.when(j == 0)
    def _():
        _rms_to(x_ref, g_ref, h_ref, tm)
        y_ref[...] = x_ref[...]
        if not sample:
            @pl.when(i == 0)
            def _():
                carry_ref[...] = jnp.zeros(carry_ref.shape, F32)

    hist = SUBLANES
    n_chunks = tm // rc
    ups = ((upv_ref, wuv_ref, lastv_ref), (upg_ref, wug_ref, lastg_ref))

    if not sample:
        first = (i % blocks_per_seq) == 0
        for half, (up_ref, _, _) in enumerate(ups):
            up_ref[0:hist, :] = jnp.where(first, 0.0, carry_ref[j, half])

    def up_proj(c):
        r0 = _aligned(c * rc, rc)
        hc = h_ref[pl.ds(r0, rc), :]
        for up_ref, wu_ref, _ in ups:
            up_ref[pl.ds(hist + r0, rc), :] = jnp.dot(hc, wu_ref[...], preferred_element_type=F32)

    def conv(up_ref, st_ref, wd_ref, bd_ref, row):
        w0 = wd_ref[0:1, :]
        w1 = wd_ref[1:2, :]
        w2 = wd_ref[2:3, :]
        if sample:
            n = ROWS // SUBLANES
            s0 = _aligned(row // SUBLANES, n)
            cur3 = up_ref[pl.ds(hist + row, ROWS), :].reshape(n, SUBLANES, tf)
            sl = lax.broadcasted_iota(jnp.int32, (n, SUBLANES, tf), 1)
            st0 = st_ref[pl.ds(s0, n), 0:1, :]
            st1 = st_ref[pl.ds(s0, n), 1:2, :]
            m1 = jnp.where(sl == 0, st1, pltpu.roll(cur3, 1, 1))
            m2 = jnp.where(sl == 0, st0, jnp.where(sl == 1, st1, pltpu.roll(cur3, 2, 1)))
            out = w0 * m2 + w1 * m1 + w2 * cur3 + bd_ref[...]
            return out.reshape(ROWS, tf)
        win = up_ref[pl.ds(row, ROWS + hist), :]
        cur = win[hist:hist + ROWS, :]
        m1 = win[hist - 1:hist - 1 + ROWS, :]
        m2 = win[hist - 2:hist - 2 + ROWS, :]
        return w0 * m2 + w1 * m1 + w2 * cur + bd_ref[...]

    def gate_down(c):
        r0 = _aligned(c * rc, rc)
        for r in range(0, rc, ROWS):
            row = _aligned(r0 + r, ROWS)
            cv = conv(upv_ref, stv_ref if sample else None, wdv_ref, bdv_ref, row)
            cg = conv(upg_ref, stg_ref if sample else None, wdg_ref, bdg_ref, row)
            gate_ref[r:r + ROWS, :] = (cv * jax.nn.sigmoid(cv) * cg).astype(BF16)
        y_ref[pl.ds(r0, rc), :] += jnp.dot(gate_ref[...], wdn_ref[...],
                                           preferred_element_type=F32)

    up_proj(0)
    for c in range(n_chunks - 1):
        up_proj(c + 1)
        gate_down(c)
    gate_down(n_chunks - 1)

    for half, (up_ref, _, last_ref) in enumerate(ups):
        if sample:
            last_ref[...] = up_ref[hist:hist + tm, :]
        else:
            tail = up_ref[tm:tm + hist, :]
            carry_ref[j, half] = tail
            last_ref[0] = tail


def _ffn(x, state, seq_len, params, tm, tf, rc):
    g_ffn, w_up, w_dw, b_dw, w_down = params
    rows = x.shape[0]
    sample = state is not None
    nj = D_FF // tf
    grid = (rows // tm, nj)
    row_map = lambda i, j: (i, 0)
    val_map = lambda i, j: (0, j)
    gate_map = lambda i, j: (0, nj + j)
    in_specs = [pl.BlockSpec((tm, D_MODEL), row_map)]
    args = [x]
    if sample:
        spt = tm // seq_len
        in_specs += [pl.BlockSpec((spt, FFN_CONV - 1, tf), lambda i, j: (i, 0, j)),
                     pl.BlockSpec((spt, FFN_CONV - 1, tf), lambda i, j: (i, 0, nj + j))]
        args += [state, state]
    in_specs += [pl.BlockSpec((1, D_MODEL), lambda i, j: (0, 0)),
                 pl.BlockSpec((D_MODEL, tf), val_map), pl.BlockSpec((D_MODEL, tf), gate_map),
                 pl.BlockSpec((FFN_CONV, tf), val_map), pl.BlockSpec((FFN_CONV, tf), gate_map),
                 pl.BlockSpec((1, tf), val_map), pl.BlockSpec((1, tf), gate_map),
                 pl.BlockSpec((tf, D_MODEL), lambda i, j: (j, 0))]
    args += [g_ffn, w_up, w_up, w_dw, w_dw, b_dw, b_dw, w_down]
    scratch = [pltpu.VMEM((tm, D_MODEL), BF16),
               pltpu.VMEM((tm + SUBLANES, tf), F32), pltpu.VMEM((tm + SUBLANES, tf), F32),
               pltpu.VMEM((rc, tf), BF16)]
    if sample:
        last_specs = [pl.BlockSpec((tm, tf), lambda i, j: (i, j))] * 2
        last_shape = [jax.ShapeDtypeStruct((rows, D_FF), F32)] * 2
        blocks_per_seq = 1
    else:
        blocks_per_seq = seq_len // tm
        last_specs = [pl.BlockSpec((1, SUBLANES, tf), lambda i, j: (i, 0, j))] * 2
        last_shape = [jax.ShapeDtypeStruct((rows // tm, SUBLANES, D_FF), F32)] * 2
        scratch += [pltpu.VMEM((nj, 2, SUBLANES, tf), F32)]
    return pl.pallas_call(
        functools.partial(_ffn_body, sample, tm, tf, rc, blocks_per_seq),
        grid=grid, in_specs=in_specs,
        out_specs=[pl.BlockSpec((tm, D_MODEL), row_map)] + last_specs,
        out_shape=[jax.ShapeDtypeStruct((rows, D_MODEL), F32)] + last_shape,
        scratch_shapes=scratch,
        compiler_params=pltpu.CompilerParams(dimension_semantics=("arbitrary", "arbitrary"),
                                             vmem_limit_bytes=VMEM_LIMIT),
        name="ffn_sample" if sample else "ffn_prompt",
    )(*args)


def _ple_body(tm, x_ref, p_ref, g_ref, wg_ref, wp_ref, gf_ref, y_ref, h_ref, gate_ref, proj_ref):
    _rms_to(x_ref, g_ref, h_ref, tm)
    gate_ref[...] = jnp.dot(h_ref[...], wg_ref[...], preferred_element_type=F32)
    proj_ref[...] = jnp.dot(p_ref[...].astype(BF16), wp_ref[...], preferred_element_type=F32)

    def piece(r):
        sl = pl.ds(r, ROWS)
        xv = x_ref[sl, :] + jax.nn.sigmoid(gate_ref[sl, :]) * proj_ref[sl, :]
        ms = jnp.mean(xv * xv, axis=-1, keepdims=True)
        y_ref[sl, :] = xv * lax.rsqrt(ms + EPS) * gf_ref[...]
    _row_loop(tm, ROWS, piece)


def _ple(x, p, params, tm, name):
    g_ple, w_gate, w_proj, g_final = params
    rows = x.shape[0]
    row_map = lambda i: (i, 0)
    return pl.pallas_call(
        functools.partial(_ple_body, tm),
        grid=(rows // tm,),
        in_specs=[pl.BlockSpec((tm, D_MODEL), row_map), pl.BlockSpec((tm, D_PLE), row_map),
                  _resident((1, D_MODEL)), _resident((D_MODEL, D_MODEL)),
                  _resident((D_PLE, D_MODEL)), _resident((1, D_MODEL))],
        out_specs=pl.BlockSpec((tm, D_MODEL), row_map),
        out_shape=jax.ShapeDtypeStruct((rows, D_MODEL), F32),
        scratch_shapes=[pltpu.VMEM((tm, D_MODEL), BF16), pltpu.VMEM((tm, D_MODEL), F32),
                        pltpu.VMEM((tm, D_MODEL), F32)],
        compiler_params=pltpu.CompilerParams(dimension_semantics=("arbitrary",),
                                             vmem_limit_bytes=VMEM_LIMIT),
        name=name,
    )(x, p, g_ple, w_gate, w_proj, g_final)


def kernel(x_prompt, x_sample, p_prompt, p_sample, state_conv_a, state_ffn_conv, g_mix, w_in, w_dw_a, b_dw_a, g_ln_a, b_ln_a, g_ln_v, b_ln_v, w_s, b_s, w_out, g_ffn, w_up, w_dw_f, b_dw_f, w_down, g_ple, w_ple_gate, w_ple_proj, g_final):
    depth = w_in.shape[0]
    assert depth == 1, "single-layer step"
    batch, seq, _ = x_prompt.shape
    dec_batch, dec_seq, _ = x_sample.shape
    row = lambda v: v.reshape(1, -1)

    def bias_rows(length):
        b = jnp.tile(b_s[0][:, :length], (1, CHUNK // length))
        return jnp.repeat(b.T, HEAD_DIM, axis=1)

    def mix_weights(length):
        return jnp.tile(w_s[0][:, :length, :length], (1, CHUNK // length, CHUNK // length))

    w_dw_pad = jnp.pad(w_dw_a[0], ((0, HIST_A - CONV_A), (0, 0)))
    mixer_common = (row(g_mix[0]), w_in[0].astype(BF16), w_dw_pad, row(b_dw_a[0]),
                    row(g_ln_a[0]), row(b_ln_a[0]), row(g_ln_v[0]), row(b_ln_v[0]))
    w_out_b = w_out[0].astype(BF16)
    ffn_params = (row(g_ffn[0]), w_up[0].astype(BF16), w_dw_f[0], row(b_dw_f[0]),
                  w_down[0].astype(BF16))
    ple_params = (row(g_ple[0]), w_ple_gate[0].astype(BF16), w_ple_proj[0].astype(BF16),
                  row(g_final))

    xp = x_prompt.reshape(batch * seq, D_MODEL)
    lp = min(seq, CHUNK)
    xp, conv_p, cv_p = _mixer(xp, None, seq,
                              mixer_common + (mix_weights(lp), bias_rows(lp), w_out_b), tm=256)
    xp, lastv_p, lastg_p = _ffn(xp, None, seq, ffn_params, tm=1024, tf=512, rc=256)
    yp = _ple(xp, p_prompt[0].reshape(batch * seq, D_PLE), ple_params, 512, "ple_prompt")

    xs = x_sample.reshape(dec_batch * dec_seq, D_MODEL)
    ls = min(dec_seq, CHUNK)
    xs, a_s, cv_s = _mixer(xs, state_conv_a[0], dec_seq,
                           mixer_common + (mix_weights(ls), bias_rows(ls), w_out_b), tm=256)
    xs, upv_s, upg_s = _ffn(xs, state_ffn_conv[0], dec_seq, ffn_params, tm=1024, tf=512, rc=256)
    ys = _ple(xs, p_sample[0].reshape(dec_batch * dec_seq, D_PLE), ple_params, 512, "ple_sample")

    keep = FFN_CONV - 1
    conv_a_prompt = conv_p[:, HIST_A - (CONV_A - 1):]
    conv_a_sample = jnp.concatenate(
        [state_conv_a[0][:, dec_seq:], a_s.reshape(dec_batch, dec_seq, C_A)], axis=1)
    per_seq = lastv_p.shape[0] // batch
    ffn_prompt = jnp.concatenate([lastv_p, lastg_p], axis=-1)[per_seq - 1::per_seq, SUBLANES - keep:]
    ffn_sample = jnp.concatenate([upv_s, upg_s], axis=-1).reshape(
        dec_batch, dec_seq, 2 * D_FF)[:, dec_seq - keep:]
    return (yp.reshape(batch, seq, D_MODEL), ys.reshape(dec_batch, dec_seq, D_MODEL),
            conv_a_prompt[None], conv_a_sample[None], ffn_prompt[None], ffn_sample[None],
            cv_p[None], cv_s.reshape(dec_batch, dec_seq, C_B)[None])
```

```python
import functools

import jax
import jax.numpy as jnp
from jax import lax
from jax.experimental import pallas as pl
from jax.experimental.pallas import tpu as pltpu

D_MODEL = 2048
HEAD_DIM = 128
C_A = D_MODEL // 2
C_B = D_MODEL - C_A
N_HEADS = C_B // HEAD_DIM
CONV_A = 31
HIST_A = 32
CONV_TILES = HIST_A // 8 + 1
CHUNK = 128
FFN_CONV = 3
D_FF = 5632
D_PLE = 256
EPS = 1e-6

SUBLANES = 8
ROWS = 32
LANES_EW = 256
VMEM_LIMIT = 56 * 1024 * 1024

F32 = jnp.float32
BF16 = jnp.bfloat16


def _resident(shape):
    return pl.BlockSpec(shape, lambda *_: (0,) * len(shape), pipeline_mode=pl.Buffered(1))


def _aligned(x, m):
    return x if isinstance(x, int) else pl.multiple_of(x, m)


def _row_loop(n_rows, step, fn):
    def body(c, carry):
        fn(pl.multiple_of(c * step, step))
        return carry
    lax.fori_loop(0, n_rows // step, body, 0)


def _rms_to(x_ref, g_ref, dst_ref, n_rows):
    def piece(r):
        xv = x_ref[pl.ds(r, ROWS), :]
        ms = jnp.mean(xv * xv, axis=-1, keepdims=True)
        dst_ref[pl.ds(r, ROWS), :] = (xv * lax.rsqrt(ms + EPS) * g_ref[...]).astype(BF16)
    _row_loop(n_rows, ROWS, piece)


def _head_ln(x, g, b):
    outs = []
    for h in range(x.shape[-1] // HEAD_DIM):
        xh = x[:, h * HEAD_DIM:(h + 1) * HEAD_DIM]
        mu = jnp.mean(xh, axis=-1, keepdims=True)
        xc = xh - mu
        var = jnp.mean(xc * xc, axis=-1, keepdims=True)
        outs.append(xc * lax.rsqrt(var + EPS))
    return jnp.concatenate(outs, axis=-1) * g + b


def _conv31(win3, n_out, coef_ref, lanes):
    acc = None
    for r in range(SUBLANES):
        inner = None
        for q in range(CONV_TILES):
            if q == 0 and r < HIST_A - (CONV_A - 1):
                continue
            k = (r * CONV_TILES + q) * SUBLANES
            term = coef_ref[k:k + SUBLANES, lanes] * win3[q:q + n_out]
            inner = term if inner is None else inner + term
        if r:
            inner = pltpu.roll(inner, SUBLANES - r, 1)
        acc = inner if acc is None else acc + inner
    return acc


def _mixer_body(sample, tm, seq_len, *refs):
    if sample:
        (x_ref, st_ref, gmix_ref, win_ref, coef_ref, bdw_ref, glna_ref, blna_ref, glnv_ref,
         blnv_ref, ws_ref, bsb_ref, wout_ref,
         y_ref, aout_ref, vout_ref,
         h_ref, z_ref, cat_ref, a_ref, vn_ref, wm_ref) = refs
    else:
        (x_ref, gmix_ref, win_ref, coef_ref, bdw_ref, glna_ref, blna_ref, glnv_ref,
         blnv_ref, ws_ref, bsb_ref, wout_ref,
         y_ref, aout_ref, vout_ref,
         h_ref, z_ref, cat_ref, a_ref, vn_ref, wm_ref) = refs
    n_chunks = tm // CHUNK
    lane_groups = [slice(g * LANES_EW, (g + 1) * LANES_EW) for g in range(C_A // LANES_EW)]

    _rms_to(x_ref, gmix_ref, h_ref, tm)

    ti = lax.broadcasted_iota(jnp.int32, (CHUNK, CHUNK), 0)
    si = lax.broadcasted_iota(jnp.int32, (CHUNK, CHUNK), 1)
    if seq_len >= CHUNK:
        mask = si <= ti
    else:
        mask = ((si // seq_len) == (ti // seq_len)) & ((si % seq_len) <= (ti % seq_len))
    for h in range(N_HEADS):
        wm_ref[h] = jnp.where(mask, ws_ref[h], 0.0).astype(BF16)

    if not sample:
        t = pl.program_id(1)

        @pl.when(t == 0)
        def _():
            a_ref[0:HIST_A, :] = jnp.zeros((HIST_A, C_A), F32)

        @pl.when(t > 0)
        def _():
            a_ref[0:HIST_A, :] = a_ref[tm:tm + HIST_A, :]

    a_base = 0 if sample else HIST_A

    def in_proj(c):
        rows = slice(c * CHUNK, (c + 1) * CHUNK)
        z_ref[rows, :] = jnp.dot(h_ref[rows, :], win_ref[...], preferred_element_type=F32)

    def out_proj(c):
        rows = slice(c * CHUNK, (c + 1) * CHUNK)
        y_ref[rows, :] = x_ref[rows, :] + jnp.dot(cat_ref[rows, :], wout_ref[...],
                                                   preferred_element_type=F32)

    def elementwise(c):
        r0 = c * CHUNK
        for r in range(r0, r0 + CHUNK, ROWS):
            for lanes in lane_groups:
                zv = z_ref[r:r + ROWS, lanes]
                zg = z_ref[r:r + ROWS, C_A + lanes.start:C_A + lanes.stop]
                a = zv * jax.nn.sigmoid(zg)
                a_ref[a_base + r:a_base + r + ROWS, lanes] = a
                if sample:
                    aout_ref[r:r + ROWS, lanes] = a
        for r in range(r0, r0 + CHUNK, ROWS):
            for lanes in lane_groups:
                width = lanes.stop - lanes.start
                if sample:
                    outs = []
                    for rs in range(r, r + ROWS, seq_len):
                        hist = st_ref[rs // seq_len, :, lanes].reshape(
                            CONV_TILES - 1, SUBLANES, width)
                        new = a_ref[rs:rs + seq_len, lanes].reshape(1, SUBLANES, width)
                        win3 = jnp.concatenate([hist, new], axis=0)
                        outs.append(_conv31(win3, 1, coef_ref, lanes))
                    conv = jnp.concatenate(outs, axis=0).reshape(ROWS, width)
                else:
                    n_out = ROWS // SUBLANES
                    win3 = a_ref[r:r + ROWS + HIST_A, lanes].reshape(
                        n_out + CONV_TILES - 1, SUBLANES, width)
                    conv = _conv31(win3, n_out, coef_ref, lanes).reshape(ROWS, width)
                conv = conv + bdw_ref[:, lanes]
                yv = _head_ln(conv, glna_ref[:, lanes], blna_ref[:, lanes])
                cat_ref[r:r + ROWS, lanes] = (yv * jax.nn.sigmoid(yv)).astype(BF16)

        v_off = 2 * C_A + C_B
        for r in range(r0, r0 + CHUNK, ROWS):
            for lanes in lane_groups:
                v = jax.nn.gelu(z_ref[r:r + ROWS, v_off + lanes.start:v_off + lanes.stop])
                vn = _head_ln(v, glnv_ref[:, lanes], blnv_ref[:, lanes])
                vn_ref[r - r0:r - r0 + ROWS, lanes] = vn.astype(BF16)
                if sample:
                    vout_ref[r:r + ROWS, lanes] = vn
                else:
                    vout_ref[0, r - r0:r - r0 + ROWS, lanes] = vn
        for h in range(N_HEADS):
            lo = h * HEAD_DIM
            mixed = jnp.dot(wm_ref[h], vn_ref[:, lo:lo + HEAD_DIM], preferred_element_type=F32)
            mixed = mixed + bsb_ref[:, lo:lo + HEAD_DIM]
            u = jax.nn.gelu(z_ref[r0:r0 + CHUNK, 2 * C_A + lo:2 * C_A + lo + HEAD_DIM])
            cat_ref[r0:r0 + CHUNK, C_A + lo:C_A + lo + HEAD_DIM] = (u * mixed).astype(BF16)

    in_proj(0)
    for c in range(n_chunks):
        if c + 1 < n_chunks:
            in_proj(c + 1)
        elementwise(c)
        out_proj(c)

    if not sample:
        aout_ref[0] = a_ref[tm:tm + HIST_A, :]


def _mixer(x, state, seq_len, params, tm):
    (g_mix, w_in, w_dw, b_dw, g_ln_a, b_ln_a, g_ln_v, b_ln_v, w_s, bias_rows, w_out) = params
    rows = x.shape[0]
    sample = state is not None
    vec = lambda n: _resident((1, n))
    common_in = [vec(D_MODEL), _resident((D_MODEL, 2 * C_A + 2 * C_B)),
                 _resident((SUBLANES * CONV_TILES * SUBLANES, C_A)),
                 vec(C_A), vec(C_A), vec(C_A), vec(C_B), vec(C_B),
                 _resident((N_HEADS, CHUNK, CHUNK)), _resident((CHUNK, C_B)),
                 _resident((C_A + C_B, D_MODEL))]
    common_scratch = [pltpu.VMEM((tm, D_MODEL), BF16),
                      pltpu.VMEM((tm, 2 * C_A + 2 * C_B), F32),
                      pltpu.VMEM((tm, C_A + C_B), BF16)]
    tail_scratch = [pltpu.VMEM((CHUNK, C_B), BF16),
                    pltpu.VMEM((N_HEADS, CHUNK, CHUNK), BF16)]
    if sample:
        n_seq = rows // seq_len
        spt = tm // seq_len
        grid = (rows // tm,)
        row_map = lambda i: (i, 0)
        in_specs = [pl.BlockSpec((tm, D_MODEL), row_map),
                    pl.BlockSpec((spt, HIST_A, C_A), lambda i: (i, 0, 0))] + common_in
        out_specs = [pl.BlockSpec((tm, D_MODEL), row_map),
                     pl.BlockSpec((tm, C_A), row_map),
                     pl.BlockSpec((tm, C_B), row_map)]
        out_shape = [jax.ShapeDtypeStruct((rows, D_MODEL), F32),
                     jax.ShapeDtypeStruct((rows, C_A), F32),
                     jax.ShapeDtypeStruct((rows, C_B), F32)]
        scratch = common_scratch + [pltpu.VMEM((tm, C_A), F32)] + tail_scratch
        args = (x, state)
        sem = ("arbitrary",)
        del n_seq
    else:
        n_seq = rows // seq_len
        nt = seq_len // tm
        grid = (n_seq, nt)
        row_map = lambda b, t: (b * nt + t, 0)
        in_specs = [pl.BlockSpec((tm, D_MODEL), row_map)] + common_in
        out_specs = [pl.BlockSpec((tm, D_MODEL), row_map),
                     pl.BlockSpec((1, HIST_A, C_A), lambda b, t: (b, 0, 0)),
                     pl.BlockSpec((1, CHUNK, C_B), lambda b, t: (b, 0, 0))]
        out_shape = [jax.ShapeDtypeStruct((rows, D_MODEL), F32),
                     jax.ShapeDtypeStruct((n_seq, HIST_A, C_A), F32),
                     jax.ShapeDtypeStruct((n_seq, CHUNK, C_B), F32)]
        scratch = common_scratch + [pltpu.VMEM((tm + HIST_A, C_A), F32)] + tail_scratch
        args = (x,)
        sem = ("arbitrary", "arbitrary")
    return pl.pallas_call(
        functools.partial(_mixer_body, sample, tm, seq_len),
        grid=grid, in_specs=in_specs, out_specs=out_specs, out_shape=out_shape,
        scratch_shapes=scratch,
        compiler_params=pltpu.CompilerParams(dimension_semantics=sem,
                                             vmem_limit_bytes=VMEM_LIMIT),
        name="mixer_sample" if sample else "mixer_prompt",
    )(*args, g_mix, w_in, w_dw, b_dw, g_ln_a, b_ln_a, g_ln_v, b_ln_v, w_s, bias_rows, w_out)


def _ffn_body(sample, tm, tf, rc, blocks_per_seq, *refs):
    if sample:
        (x_ref, stv_ref, stg_ref, g_ref, wuv_ref, wug_ref, wdv_ref, wdg_ref, bdv_ref, bdg_ref,
         wdn_ref, y_ref, lastv_ref, lastg_ref, h_ref, upv_ref, upg_ref, gate_ref) = refs
    else:
        (x_ref, g_ref, wuv_ref, wug_ref, wdv_ref, wdg_ref, bdv_ref, bdg_ref,
         wdn_ref, y_ref, lastv_ref, lastg_ref, h_ref, upv_ref, upg_ref, gate_ref, carry_ref) = refs
    i = pl.program_id(0)
    j = pl.program_id(1)

    @pl.when(j == 0)
    def _():
        _rms_to(x_ref, g_ref, h_ref, tm)
        y_ref[...] = x_ref[...]
        if not sample:
            @pl.when(i == 0)
            def _():
                carry_ref[...] = jnp.zeros(carry_ref.shape, F32)

    hist = SUBLANES
    n_chunks = tm // rc
    ups = ((upv_ref, wuv_ref, lastv_ref), (upg_ref, wug_ref, lastg_ref))

    if not sample:
        first = (i % blocks_per_seq) == 0
        for half, (up_ref, _, _) in enumerate(ups):
            up_ref[0:hist, :] = jnp.where(first, 0.0, carry_ref[j, half])

    def up_proj(c):
        r0 = _aligned(c * rc, rc)
        hc = h_ref[pl.ds(r0, rc), :]
        for up_ref, wu_ref, _ in ups:
            up_ref[pl.ds(hist + r0, rc), :] = jnp.dot(hc, wu_ref[...], preferred_element_type=F32)

    def conv(up_ref, st_ref, wd_ref, bd_ref, row):
        w0 = wd_ref[0:1, :]
        w1 = wd_ref[1:2, :]
        w2 = wd_ref[2:3, :]
        if sample:
            n = ROWS // SUBLANES
            s0 = _aligned(row // SUBLANES, n)
            cur3 = up_ref[pl.ds(hist + row, ROWS), :].reshape(n, SUBLANES, tf)
            sl = lax.broadcasted_iota(jnp.int32, (n, SUBLANES, tf), 1)
            st0 = st_ref[pl.ds(s0, n), 0:1, :]
            st1 = st_ref[pl.ds(s0, n), 1:2, :]
            m1 = jnp.where(sl == 0, st1, pltpu.roll(cur3, 1, 1))
            m2 = jnp.where(sl == 0, st0, jnp.where(sl == 1, st1, pltpu.roll(cur3, 2, 1)))
            out = w0 * m2 + w1 * m1 + w2 * cur3 + bd_ref[...]
            return out.reshape(ROWS, tf)
        win = up_ref[pl.ds(row, ROWS + hist), :]
        cur = win[hist:hist + ROWS, :]
        m1 = win[hist - 1:hist - 1 + ROWS, :]
        m2 = win[hist - 2:hist - 2 + ROWS, :]
        return w0 * m2 + w1 * m1 + w2 * cur + bd_ref[...]

    def gate_down(c):
        r0 = _aligned(c * rc, rc)
        for r in range(0, rc, ROWS):
            row = _aligned(r0 + r, ROWS)
            cv = conv(upv_ref, stv_ref if sample else None, wdv_ref, bdv_ref, row)
            cg = conv(upg_ref, stg_ref if sample else None, wdg_ref, bdg_ref, row)
            gate_ref[r:r + ROWS, :] = (cv * jax.nn.sigmoid(cv) * cg).astype(BF16)
        y_ref[pl.ds(r0, rc), :] += jnp.dot(gate_ref[...], wdn_ref[...],
                                           preferred_element_type=F32)

    up_proj(0)
    for c in range(n_chunks - 1):
        up_proj(c + 1)
        gate_down(c)
    gate_down(n_chunks - 1)

    for half, (up_ref, _, last_ref) in enumerate(ups):
        if sample:
            last_ref[...] = up_ref[hist:hist + tm, :]
        else:
            tail = up_ref[tm:tm + hist, :]
            carry_ref[j, half] = tail
            last_ref[0] = tail


def _ffn(x, state, seq_len, params, tm, tf, rc):
    g_ffn, w_up, w_dw, b_dw, w_down = params
    rows = x.shape[0]
    sample = state is not None
    nj = D_FF // tf
    grid = (rows // tm, nj)
    row_map = lambda i, j: (i, 0)
    val_map = lambda i, j: (0, j)
    gate_map = lambda i, j: (0, nj + j)
    in_specs = [pl.BlockSpec((tm, D_MODEL), row_map)]
    args = [x]
    if sample:
        spt = tm // seq_len
        in_specs += [pl.BlockSpec((spt, FFN_CONV - 1, tf), lambda i, j: (i, 0, j)),
                     pl.BlockSpec((spt, FFN_CONV - 1, tf), lambda i, j: (i, 0, nj + j))]
        args += [state, state]
    in_specs += [pl.BlockSpec((1, D_MODEL), lambda i, j: (0, 0)),
                 pl.BlockSpec((D_MODEL, tf), val_map), pl.BlockSpec((D_MODEL, tf), gate_map),
                 pl.BlockSpec((FFN_CONV, tf), val_map), pl.BlockSpec((FFN_CONV, tf), gate_map),
                 pl.BlockSpec((1, tf), val_map), pl.BlockSpec((1, tf), gate_map),
                 pl.BlockSpec((tf, D_MODEL), lambda i, j: (j, 0))]
    args += [g_ffn, w_up, w_up, w_dw, w_dw, b_dw, b_dw, w_down]
    scratch = [pltpu.VMEM((tm, D_MODEL), BF16),
               pltpu.VMEM((tm + SUBLANES, tf), F32), pltpu.VMEM((tm + SUBLANES, tf), F32),
               pltpu.VMEM((rc, tf), BF16)]
    if sample:
        last_specs = [pl.BlockSpec((tm, tf), lambda i, j: (i, j))] * 2
        last_shape = [jax.ShapeDtypeStruct((rows, D_FF), F32)] * 2
        blocks_per_seq = 1
    else:
        blocks_per_seq = seq_len // tm
        last_specs = [pl.BlockSpec((1, SUBLANES, tf), lambda i, j: (i, 0, j))] * 2
        last_shape = [jax.ShapeDtypeStruct((rows // tm, SUBLANES, D_FF), F32)] * 2
        scratch += [pltpu.VMEM((nj, 2, SUBLANES, tf), F32)]
    return pl.pallas_call(
        functools.partial(_ffn_body, sample, tm, tf, rc, blocks_per_seq),
        grid=grid, in_specs=in_specs,
        out_specs=[pl.BlockSpec((tm, D_MODEL), row_map)] + last_specs,
        out_shape=[jax.ShapeDtypeStruct((rows, D_MODEL), F32)] + last_shape,
        scratch_shapes=scratch,
        compiler_params=pltpu.CompilerParams(dimension_semantics=("arbitrary", "arbitrary"),
                                             vmem_limit_bytes=VMEM_LIMIT),
        name="ffn_sample" if sample else "ffn_prompt",
    )(*args)


def _ple_body(tm, x_ref, p_ref, g_ref, wg_ref, wp_ref, gf_ref, y_ref, h_ref, gate_ref, proj_ref):
    _rms_to(x_ref, g_ref, h_ref, tm)
    gate_ref[...] = jnp.dot(h_ref[...], wg_ref[...], preferred_element_type=F32)
    proj_ref[...] = jnp.dot(p_ref[...].astype(BF16), wp_ref[...], preferred_element_type=F32)

    def piece(r):
        sl = pl.ds(r, ROWS)
        xv = x_ref[sl, :] + jax.nn.sigmoid(gate_ref[sl, :]) * proj_ref[sl, :]
        ms = jnp.mean(xv * xv, axis=-1, keepdims=True)
        y_ref[sl, :] = xv * lax.rsqrt(ms + EPS) * gf_ref[...]
    _row_loop(tm, ROWS, piece)


def _ple(x, p, params, tm, name):
    g_ple, w_gate, w_proj, g_final = params
    rows = x.shape[0]
    row_map = lambda i: (i, 0)
    return pl.pallas_call(
        functools.partial(_ple_body, tm),
        grid=(rows // tm,),
        in_specs=[pl.BlockSpec((tm, D_MODEL), row_map), pl.BlockSpec((tm, D_PLE), row_map),
                  _resident((1, D_MODEL)), _resident((D_MODEL, D_MODEL)),
                  _resident((D_PLE, D_MODEL)), _resident((1, D_MODEL))],
        out_specs=pl.BlockSpec((tm, D_MODEL), row_map),
        out_shape=jax.ShapeDtypeStruct((rows, D_MODEL), F32),
        scratch_shapes=[pltpu.VMEM((tm, D_MODEL), BF16), pltpu.VMEM((tm, D_MODEL), F32),
                        pltpu.VMEM((tm, D_MODEL), F32)],
        compiler_params=pltpu.CompilerParams(dimension_semantics=("arbitrary",),
                                             vmem_limit_bytes=VMEM_LIMIT),
        name=name,
    )(x, p, g_ple, w_gate, w_proj, g_final)


def _conv_coef(w):
    c = w.shape[1]
    off = HIST_A - (CONV_A - 1)
    n = SUBLANES * (CONV_TILES + 1)
    wp = jnp.zeros((n, c), w.dtype).at[off:off + CONV_A].set(w)
    wp_prev = jnp.concatenate([jnp.zeros((SUBLANES, c), w.dtype), wp[:n - SUBLANES]], axis=0)
    r = jnp.arange(SUBLANES)[:, None]
    q = jnp.arange(CONV_TILES)[None, :]
    d = SUBLANES * q + r
    s = jnp.arange(SUBLANES)[None, None, :, None]
    coef = jnp.where(s >= r[:, :, None, None], wp[d][:, :, None, :], wp_prev[d][:, :, None, :])
    return coef.reshape(SUBLANES * CONV_TILES * SUBLANES, c)


def kernel(x_prompt, x_sample, p_prompt, p_sample, state_conv_a, state_ffn_conv, g_mix, w_in, w_dw_a, b_dw_a, g_ln_a, b_ln_a, g_ln_v, b_ln_v, w_s, b_s, w_out, g_ffn, w_up, w_dw_f, b_dw_f, w_down, g_ple, w_ple_gate, w_ple_proj, g_final):
    depth = w_in.shape[0]
    assert depth == 1, "single-layer step"
    batch, seq, _ = x_prompt.shape
    dec_batch, dec_seq, _ = x_sample.shape
    row = lambda v: v.reshape(1, -1)

    def bias_rows(length):
        b = jnp.tile(b_s[0][:, :length], (1, CHUNK // length))
        return jnp.repeat(b.T, HEAD_DIM, axis=1)

    def mix_weights(length):
        return jnp.tile(w_s[0][:, :length, :length], (1, CHUNK // length, CHUNK // length))

    mixer_common = (row(g_mix[0]), w_in[0].astype(BF16), _conv_coef(w_dw_a[0]), row(b_dw_a[0]),
                    row(g_ln_a[0]), row(b_ln_a[0]), row(g_ln_v[0]), row(b_ln_v[0]))
    w_out_b = w_out[0].astype(BF16)
    ffn_params = (row(g_ffn[0]), w_up[0].astype(BF16), w_dw_f[0], row(b_dw_f[0]),
                  w_down[0].astype(BF16))
    ple_params = (row(g_ple[0]), w_ple_gate[0].astype(BF16), w_ple_proj[0].astype(BF16),
                  row(g_final))

    xp = x_prompt.reshape(batch * seq, D_MODEL)
    lp = min(seq, CHUNK)
    xp, conv_p, cv_p = _mixer(xp, None, seq,
                              mixer_common + (mix_weights(lp), bias_rows(lp), w_out_b), tm=256)
    xp, lastv_p, lastg_p = _ffn(xp, None, seq, ffn_params, tm=1024, tf=512, rc=256)
    yp = _ple(xp, p_prompt[0].reshape(batch * seq, D_PLE), ple_params, 512, "ple_prompt")

    xs = x_sample.reshape(dec_batch * dec_seq, D_MODEL)
    ls = min(dec_seq, CHUNK)
    state_pad = jnp.pad(state_conv_a[0], ((0, 0), (HIST_A - (CONV_A - 1), 0), (0, 0)))
    xs, a_s, cv_s = _mixer(xs, state_pad, dec_seq,
                           mixer_common + (mix_weights(ls), bias_rows(ls), w_out_b), tm=256)
    xs, upv_s, upg_s = _ffn(xs, state_ffn_conv[0], dec_seq, ffn_params, tm=1024, tf=512, rc=256)
    ys = _ple(xs, p_sample[0].reshape(dec_batch * dec_seq, D_PLE), ple_params, 512, "ple_sample")

    keep = FFN_CONV - 1
    conv_a_prompt = conv_p[:, HIST_A - (CONV_A - 1):]
    conv_a_sample = jnp.concatenate(
        [state_conv_a[0][:, dec_seq:], a_s.reshape(dec_batch, dec_seq, C_A)], axis=1)
    per_seq = lastv_p.shape[0] // batch
    ffn_prompt = jnp.concatenate([lastv_p, lastg_p], axis=-1)[per_seq - 1::per_seq, SUBLANES - keep:]
    ffn_sample = jnp.concatenate([upv_s, upg_s], axis=-1).reshape(
        dec_batch, dec_seq, 2 * D_FF)[:, dec_seq - keep:]
    return (yp.reshape(batch, seq, D_MODEL), ys.reshape(dec_batch, dec_seq, D_MODEL),
            conv_a_prompt[None], conv_a_sample[None], ffn_prompt[None], ffn_sample[None],
            cv_p[None], cv_s.reshape(dec_batch, dec_seq, C_B)[None])
```

```python
import functools

import jax
import jax.numpy as jnp
from jax import lax
from jax.experimental import pallas as pl
from jax.experimental.pallas import tpu as pltpu

D_MODEL = 2048
HEAD_DIM = 128
C_A = D_MODEL // 2
C_B = D_MODEL - C_A
N_HEADS = C_B // HEAD_DIM
CONV_A = 31
HIST_A = 32
CONV_TILES = HIST_A // 8 + 1
CHUNK = 128
FFN_CONV = 3
D_FF = 5632
D_PLE = 256
EPS = 1e-6

SUBLANES = 8
ROWS = 32
LANES_EW = 256
PLE_SKEW = 2
PLE_SLOTS = 3
VMEM_LIMIT = 56 * 1024 * 1024

F32 = jnp.float32
BF16 = jnp.bfloat16


def _resident(shape):
    return pl.BlockSpec(shape, lambda *_: (0,) * len(shape), pipeline_mode=pl.Buffered(1))


def _aligned(x, m):
    return x if isinstance(x, int) else pl.multiple_of(x, m)


def _row_loop(n_rows, step, fn):
    def body(c, carry):
        fn(pl.multiple_of(c * step, step))
        return carry
    lax.fori_loop(0, n_rows // step, body, 0)


def _rms_to(x_ref, g_ref, dst_ref, n_rows):
    def piece(r):
        xv = x_ref[pl.ds(r, ROWS), :]
        ms = jnp.mean(xv * xv, axis=-1, keepdims=True)
        dst_ref[pl.ds(r, ROWS), :] = (xv * lax.rsqrt(ms + EPS) * g_ref[...]).astype(BF16)
    _row_loop(n_rows, ROWS, piece)


def _head_ln(x, g, b):
    outs = []
    for h in range(x.shape[-1] // HEAD_DIM):
        xh = x[:, h * HEAD_DIM:(h + 1) * HEAD_DIM]
        mu = jnp.mean(xh, axis=-1, keepdims=True)
        xc = xh - mu
        var = jnp.mean(xc * xc, axis=-1, keepdims=True)
        outs.append(xc * lax.rsqrt(var + EPS))
    return jnp.concatenate(outs, axis=-1) * g + b


def _conv31(win3, n_out, coef_ref, lanes):
    acc = None
    for r in range(SUBLANES):
        inner = None
        for q in range(CONV_TILES):
            if q == 0 and r < HIST_A - (CONV_A - 1):
                continue
            k = (r * CONV_TILES + q) * SUBLANES
            term = coef_ref[k:k + SUBLANES, lanes] * win3[q:q + n_out]
            inner = term if inner is None else inner + term
        if r:
            inner = pltpu.roll(inner, SUBLANES - r, 1)
        acc = inner if acc is None else acc + inner
    return acc


def _mixer_body(sample, tm, seq_len, *refs):
    if sample:
        (x_ref, st_ref, gmix_ref, win_ref, coef_ref, bdw_ref, glna_ref, blna_ref, glnv_ref,
         blnv_ref, ws_ref, bsb_ref, wout_ref,
         y_ref, aout_ref, vout_ref,
         h_ref, z_ref, cat_ref, a_ref, vn_ref, wm_ref) = refs
    else:
        (x_ref, gmix_ref, win_ref, coef_ref, bdw_ref, glna_ref, blna_ref, glnv_ref,
         blnv_ref, ws_ref, bsb_ref, wout_ref,
         y_ref, aout_ref, vout_ref,
         h_ref, z_ref, cat_ref, a_ref, vn_ref, wm_ref) = refs
    n_chunks = tm // CHUNK
    lane_groups = [slice(g * LANES_EW, (g + 1) * LANES_EW) for g in range(C_A // LANES_EW)]

    _rms_to(x_ref, gmix_ref, h_ref, tm)

    ti = lax.broadcasted_iota(jnp.int32, (CHUNK, CHUNK), 0)
    si = lax.broadcasted_iota(jnp.int32, (CHUNK, CHUNK), 1)
    if seq_len >= CHUNK:
        mask = si <= ti
    else:
        mask = ((si // seq_len) == (ti // seq_len)) & ((si % seq_len) <= (ti % seq_len))
    for h in range(N_HEADS):
        wm_ref[h] = jnp.where(mask, ws_ref[h], 0.0).astype(BF16)

    if not sample:
        t = pl.program_id(1)

        @pl.when(t == 0)
        def _():
            a_ref[0:HIST_A, :] = jnp.zeros((HIST_A, C_A), F32)

        @pl.when(t > 0)
        def _():
            a_ref[0:HIST_A, :] = a_ref[tm:tm + HIST_A, :]

    a_base = 0 if sample else HIST_A

    def in_proj(c):
        rows = slice(c * CHUNK, (c + 1) * CHUNK)
        z_ref[rows, :] = jnp.dot(h_ref[rows, :], win_ref[...], preferred_element_type=F32)

    def out_proj(c):
        rows = slice(c * CHUNK, (c + 1) * CHUNK)
        y_ref[rows, :] = x_ref[rows, :] + jnp.dot(cat_ref[rows, :], wout_ref[...],
                                                   preferred_element_type=F32)

    def elementwise(c):
        r0 = c * CHUNK
        for r in range(r0, r0 + CHUNK, ROWS):
            for lanes in lane_groups:
                zv = z_ref[r:r + ROWS, lanes]
                zg = z_ref[r:r + ROWS, C_A + lanes.start:C_A + lanes.stop]
                a = zv * jax.nn.sigmoid(zg)
                a_ref[a_base + r:a_base + r + ROWS, lanes] = a
                if sample:
                    aout_ref[r:r + ROWS, lanes] = a
        for r in range(r0, r0 + CHUNK, ROWS):
            for lanes in lane_groups:
                width = lanes.stop - lanes.start
                if sample:
                    outs = []
                    for rs in range(r, r + ROWS, seq_len):
                        hist = st_ref[rs // seq_len, :, lanes].reshape(
                            CONV_TILES - 1, SUBLANES, width)
                        new = a_ref[rs:rs + seq_len, lanes].reshape(1, SUBLANES, width)
                        win3 = jnp.concatenate([hist, new], axis=0)
                        outs.append(_conv31(win3, 1, coef_ref, lanes))
                    conv = jnp.concatenate(outs, axis=0).reshape(ROWS, width)
                else:
                    n_out = ROWS // SUBLANES
                    win3 = a_ref[r:r + ROWS + HIST_A, lanes].reshape(
                        n_out + CONV_TILES - 1, SUBLANES, width)
                    conv = _conv31(win3, n_out, coef_ref, lanes).reshape(ROWS, width)
                conv = conv + bdw_ref[:, lanes]
                yv = _head_ln(conv, glna_ref[:, lanes], blna_ref[:, lanes])
                cat_ref[r:r + ROWS, lanes] = (yv * jax.nn.sigmoid(yv)).astype(BF16)

        v_off = 2 * C_A + C_B
        for r in range(r0, r0 + CHUNK, ROWS):
            for lanes in lane_groups:
                v = jax.nn.gelu(z_ref[r:r + ROWS, v_off + lanes.start:v_off + lanes.stop])
                vn = _head_ln(v, glnv_ref[:, lanes], blnv_ref[:, lanes])
                vn_ref[r - r0:r - r0 + ROWS, lanes] = vn.astype(BF16)
                if sample:
                    vout_ref[r:r + ROWS, lanes] = vn
                else:
                    vout_ref[0, r - r0:r - r0 + ROWS, lanes] = vn
        for h in range(N_HEADS):
            lo = h * HEAD_DIM
            mixed = jnp.dot(wm_ref[h], vn_ref[:, lo:lo + HEAD_DIM], preferred_element_type=F32)
            mixed = mixed + bsb_ref[:, lo:lo + HEAD_DIM]
            u = jax.nn.gelu(z_ref[r0:r0 + CHUNK, 2 * C_A + lo:2 * C_A + lo + HEAD_DIM])
            cat_ref[r0:r0 + CHUNK, C_A + lo:C_A + lo + HEAD_DIM] = (u * mixed).astype(BF16)

    in_proj(0)
    for c in range(n_chunks):
        if c + 1 < n_chunks:
            in_proj(c + 1)
        elementwise(c)
        out_proj(c)

    if not sample:
        aout_ref[0] = a_ref[tm:tm + HIST_A, :]


def _mixer(x, state, seq_len, params, tm):
    (g_mix, w_in, w_dw, b_dw, g_ln_a, b_ln_a, g_ln_v, b_ln_v, w_s, bias_rows, w_out) = params
    rows = x.shape[0]
    sample = state is not None
    vec = lambda n: _resident((1, n))
    common_in = [vec(D_MODEL), _resident((D_MODEL, 2 * C_A + 2 * C_B)),
                 _resident((SUBLANES * CONV_TILES * SUBLANES, C_A)),
                 vec(C_A), vec(C_A), vec(C_A), vec(C_B), vec(C_B),
                 _resident((N_HEADS, CHUNK, CHUNK)), _resident((CHUNK, C_B)),
                 _resident((C_A + C_B, D_MODEL))]
    common_scratch = [pltpu.VMEM((tm, D_MODEL), BF16),
                      pltpu.VMEM((tm, 2 * C_A + 2 * C_B), F32),
                      pltpu.VMEM((tm, C_A + C_B), BF16)]
    tail_scratch = [pltpu.VMEM((CHUNK, C_B), BF16),
                    pltpu.VMEM((N_HEADS, CHUNK, CHUNK), BF16)]
    if sample:
        n_seq = rows // seq_len
        spt = tm // seq_len
        grid = (rows // tm,)
        row_map = lambda i: (i, 0)
        in_specs = [pl.BlockSpec((tm, D_MODEL), row_map),
                    pl.BlockSpec((spt, HIST_A, C_A), lambda i: (i, 0, 0))] + common_in
        out_specs = [pl.BlockSpec((tm, D_MODEL), row_map),
                     pl.BlockSpec((tm, C_A), row_map),
                     pl.BlockSpec((tm, C_B), row_map)]
        out_shape = [jax.ShapeDtypeStruct((rows, D_MODEL), F32),
                     jax.ShapeDtypeStruct((rows, C_A), F32),
                     jax.ShapeDtypeStruct((rows, C_B), F32)]
        scratch = common_scratch + [pltpu.VMEM((tm, C_A), F32)] + tail_scratch
        args = (x, state)
        sem = ("arbitrary",)
        del n_seq
    else:
        n_seq = rows // seq_len
        nt = seq_len // tm
        grid = (n_seq, nt)
        row_map = lambda b, t: (b * nt + t, 0)
        in_specs = [pl.BlockSpec((tm, D_MODEL), row_map)] + common_in
        out_specs = [pl.BlockSpec((tm, D_MODEL), row_map),
                     pl.BlockSpec((1, HIST_A, C_A), lambda b, t: (b, 0, 0)),
                     pl.BlockSpec((1, CHUNK, C_B), lambda b, t: (b, 0, 0))]
        out_shape = [jax.ShapeDtypeStruct((rows, D_MODEL), F32),
                     jax.ShapeDtypeStruct((n_seq, HIST_A, C_A), F32),
                     jax.ShapeDtypeStruct((n_seq, CHUNK, C_B), F32)]
        scratch = common_scratch + [pltpu.VMEM((tm + HIST_A, C_A), F32)] + tail_scratch
        args = (x,)
        sem = ("arbitrary", "arbitrary")
    return pl.pallas_call(
        functools.partial(_mixer_body, sample, tm, seq_len),
        grid=grid, in_specs=in_specs, out_specs=out_specs, out_shape=out_shape,
        scratch_shapes=scratch,
        compiler_params=pltpu.CompilerParams(dimension_semantics=sem,
                                             vmem_limit_bytes=VMEM_LIMIT),
        name="mixer_sample" if sample else "mixer_prompt",
    )(*args, g_mix, w_in, w_dw, b_dw, g_ln_a, b_ln_a, g_ln_v, b_ln_v, w_s, bias_rows, w_out)


def _ffn_body(sample, tm, tf, rc, blocks_per_seq, *refs):
    if sample:
        (x_ref, stv_ref, stg_ref, g_ref, wuv_ref, wug_ref, wdv_ref, wdg_ref, bdv_ref, bdg_ref,
         wdn_ref, y_ref, lastv_ref, lastg_ref, h_ref, upv_ref, upg_ref, gate_ref) = refs
    else:
        (x_ref, g_ref, wuv_ref, wug_ref, wdv_ref, wdg_ref, bdv_ref, bdg_ref,
         wdn_ref, y_ref, lastv_ref, lastg_ref, h_ref, upv_ref, upg_ref, gate_ref, carry_ref) = refs
    i = pl.program_id(0)
    j = pl.program_id(1)

    @pl.when(j == 0)
    def _():
        _rms_to(x_ref, g_ref, h_ref, tm)
        y_ref[...] = x_ref[...]
        if not sample:
            @pl.when(i == 0)
            def _():
                carry_ref[...] = jnp.zeros(carry_ref.shape, F32)

    hist = SUBLANES
    n_chunks = tm // rc
    ups = ((upv_ref, wuv_ref, lastv_ref), (upg_ref, wug_ref, lastg_ref))

    if not sample:
        first = (i % blocks_per_seq) == 0
        for half, (up_ref, _, _) in enumerate(ups):
            up_ref[0:hist, :] = jnp.where(first, 0.0, carry_ref[j, half])

    def up_proj(c):
        r0 = _aligned(c * rc, rc)
        hc = h_ref[pl.ds(r0, rc), :]
        for up_ref, wu_ref, _ in ups:
            up_ref[pl.ds(hist + r0, rc), :] = jnp.dot(hc, wu_ref[...], preferred_element_type=F32)

    def conv(up_ref, st_ref, wd_ref, bd_ref, row):
        w0 = wd_ref[0:1, :]
        w1 = wd_ref[1:2, :]
        w2 = wd_ref[2:3, :]
        if sample:
            n = ROWS // SUBLANES
            s0 = _aligned(row // SUBLANES, n)
            cur3 = up_ref[pl.ds(hist + row, ROWS), :].reshape(n, SUBLANES, tf)
            sl = lax.broadcasted_iota(jnp.int32, (n, SUBLANES, tf), 1)
            st0 = st_ref[pl.ds(s0, n), 0:1, :]
            st1 = st_ref[pl.ds(s0, n), 1:2, :]
            m1 = jnp.where(sl == 0, st1, pltpu.roll(cur3, 1, 1))
            m2 = jnp.where(sl == 0, st0, jnp.where(sl == 1, st1, pltpu.roll(cur3, 2, 1)))
            out = w0 * m2 + w1 * m1 + w2 * cur3 + bd_ref[...]
            return out.reshape(ROWS, tf)
        win = up_ref[pl.ds(row, ROWS + hist), :]
        cur = win[hist:hist + ROWS, :]
        m1 = win[hist - 1:hist - 1 + ROWS, :]
        m2 = win[hist - 2:hist - 2 + ROWS, :]
        return w0 * m2 + w1 * m1 + w2 * cur + bd_ref[...]

    def gate_down(c):
        r0 = _aligned(c * rc, rc)
        for r in range(0, rc, ROWS):
            row = _aligned(r0 + r, ROWS)
            cv = conv(upv_ref, stv_ref if sample else None, wdv_ref, bdv_ref, row)
            cg = conv(upg_ref, stg_ref if sample else None, wdg_ref, bdg_ref, row)
            gate_ref[r:r + ROWS, :] = (cv * jax.nn.sigmoid(cv) * cg).astype(BF16)
        y_ref[pl.ds(r0, rc), :] += jnp.dot(gate_ref[...], wdn_ref[...],
                                           preferred_element_type=F32)

    up_proj(0)
    for c in range(n_chunks - 1):
        up_proj(c + 1)
        gate_down(c)
    gate_down(n_chunks - 1)

    for half, (up_ref, _, last_ref) in enumerate(ups):
        if sample:
            up3 = up_ref[hist:hist + tm, :].reshape(tm // SUBLANES, SUBLANES, tf)
            last_ref[...] = up3[:, SUBLANES - (FFN_CONV - 1):, :]
        else:
            tail = up_ref[tm:tm + hist, :]
            carry_ref[j, half] = tail
            last_ref[0] = tail


def _ffn(x, state, seq_len, params, tm, tf, rc):
    g_ffn, w_up, w_dw, b_dw, w_down = params
    rows = x.shape[0]
    sample = state is not None
    nj = D_FF // tf
    grid = (rows // tm, nj)
    row_map = lambda i, j: (i, 0)
    val_map = lambda i, j: (0, j)
    gate_map = lambda i, j: (0, nj + j)
    in_specs = [pl.BlockSpec((tm, D_MODEL), row_map)]
    args = [x]
    if sample:
        spt = tm // seq_len
        in_specs += [pl.BlockSpec((spt, FFN_CONV - 1, tf), lambda i, j: (i, 0, j)),
                     pl.BlockSpec((spt, FFN_CONV - 1, tf), lambda i, j: (i, 0, nj + j))]
        args += [state, state]
    in_specs += [pl.BlockSpec((1, D_MODEL), lambda i, j: (0, 0)),
                 pl.BlockSpec((D_MODEL, tf), val_map), pl.BlockSpec((D_MODEL, tf), gate_map),
                 pl.BlockSpec((FFN_CONV, tf), val_map), pl.BlockSpec((FFN_CONV, tf), gate_map),
                 pl.BlockSpec((1, tf), val_map), pl.BlockSpec((1, tf), gate_map),
                 pl.BlockSpec((tf, D_MODEL), lambda i, j: (j, 0))]
    args += [g_ffn, w_up, w_up, w_dw, w_dw, b_dw, b_dw, w_down]
    scratch = [pltpu.VMEM((tm, D_MODEL), BF16),
               pltpu.VMEM((tm + SUBLANES, tf), F32), pltpu.VMEM((tm + SUBLANES, tf), F32),
               pltpu.VMEM((rc, tf), BF16)]
    if sample:
        last_specs = [pl.BlockSpec((spt, FFN_CONV - 1, tf), lambda i, j: (i, 0, j))] * 2
        last_shape = [jax.ShapeDtypeStruct((rows // seq_len, FFN_CONV - 1, D_FF), F32)] * 2
        blocks_per_seq = 1
    else:
        blocks_per_seq = seq_len // tm
        last_specs = [pl.BlockSpec((1, SUBLANES, tf), lambda i, j: (i, 0, j))] * 2
        last_shape = [jax.ShapeDtypeStruct((rows // tm, SUBLANES, D_FF), F32)] * 2
        scratch += [pltpu.VMEM((nj, 2, SUBLANES, tf), F32)]
    return pl.pallas_call(
        functools.partial(_ffn_body, sample, tm, tf, rc, blocks_per_seq),
        grid=grid, in_specs=in_specs,
        out_specs=[pl.BlockSpec((tm, D_MODEL), row_map)] + last_specs,
        out_shape=[jax.ShapeDtypeStruct((rows, D_MODEL), F32)] + last_shape,
        scratch_shapes=scratch,
        compiler_params=pltpu.CompilerParams(dimension_semantics=("arbitrary", "arbitrary"),
                                             vmem_limit_bytes=VMEM_LIMIT),
        name="ffn_sample" if sample else "ffn_prompt",
    )(*args)


def _ple_body(tm, x_ref, p_ref, g_ref, wg_ref, wp_ref, gf_ref, y_ref, h_ref, gate_ref, proj_ref):
    n_chunks = tm // CHUNK
    never = pl.program_id(0) < 0
    pieces = CHUNK // ROWS
    anchors = {}

    def norm(c):
        slot = (c % PLE_SLOTS) * CHUNK
        for k in range(pieces):
            r = c * CHUNK + k * ROWS
            xv = x_ref[r:r + ROWS, :]
            ms = jnp.mean(xv * xv, axis=-1, keepdims=True)
            hv = xv * lax.rsqrt(ms + EPS) * g_ref[...]
            anchor = anchors.pop((c - PLE_SKEW, k), None)
            if anchor is not None:
                head = jnp.where(never, anchor, hv[:, :HEAD_DIM])
                hv = jnp.concatenate([head, hv[:, HEAD_DIM:]], axis=-1)
            h_ref[slot + k * ROWS:slot + (k + 1) * ROWS, :] = hv.astype(BF16)

    def projections(c):
        slot = (c % PLE_SLOTS) * CHUNK
        rows = slice(c * CHUNK, (c + 1) * CHUNK)
        gate_ref[slot:slot + CHUNK, :] = jnp.dot(h_ref[slot:slot + CHUNK, :], wg_ref[...],
                                                 preferred_element_type=F32)
        proj_ref[slot:slot + CHUNK, :] = jnp.dot(p_ref[rows, :].astype(BF16), wp_ref[...],
                                                 preferred_element_type=F32)

    def finish(c):
        slot = (c % PLE_SLOTS) * CHUNK
        for k in range(pieces):
            r = c * CHUNK + k * ROWS
            s = slot + k * ROWS
            xv = x_ref[r:r + ROWS, :] + jax.nn.sigmoid(gate_ref[s:s + ROWS, :]) * proj_ref[s:s + ROWS, :]
            ms = jnp.mean(xv * xv, axis=-1, keepdims=True)
            yv = xv * lax.rsqrt(ms + EPS) * gf_ref[...]
            y_ref[r:r + ROWS, :] = yv
            anchors[(c, k)] = yv[:, :HEAD_DIM]

    norm(0)
    for c in range(n_chunks):
        projections(c)
        if c >= 1:
            finish(c - 1)
        if c + 1 < n_chunks:
            norm(c + 1)
    finish(n_chunks - 1)


def _ple(x, p, params, tm, name):
    g_ple, w_gate, w_proj, g_final = params
    rows = x.shape[0]
    row_map = lambda i: (i, 0)
    return pl.pallas_call(
        functools.partial(_ple_body, tm),
        grid=(rows // tm,),
        in_specs=[pl.BlockSpec((tm, D_MODEL), row_map), pl.BlockSpec((tm, D_PLE), row_map),
                  _resident((1, D_MODEL)), _resident((D_MODEL, D_MODEL)),
                  _resident((D_PLE, D_MODEL)), _resident((1, D_MODEL))],
        out_specs=pl.BlockSpec((tm, D_MODEL), row_map),
        out_shape=jax.ShapeDtypeStruct((rows, D_MODEL), F32),
        scratch_shapes=[pltpu.VMEM((PLE_SLOTS * CHUNK, D_MODEL), BF16),
                        pltpu.VMEM((PLE_SLOTS * CHUNK, D_MODEL), F32),
                        pltpu.VMEM((PLE_SLOTS * CHUNK, D_MODEL), F32)],
        compiler_params=pltpu.CompilerParams(dimension_semantics=("arbitrary",),
                                             vmem_limit_bytes=VMEM_LIMIT),
        name=name,
    )(x, p, g_ple, w_gate, w_proj, g_final)


def _conv_coef(w):
    c = w.shape[1]
    off = HIST_A - (CONV_A - 1)
    n = SUBLANES * (CONV_TILES + 1)
    wp = jnp.zeros((n, c), w.dtype).at[off:off + CONV_A].set(w)
    wp_prev = jnp.concatenate([jnp.zeros((SUBLANES, c), w.dtype), wp[:n - SUBLANES]], axis=0)
    r = jnp.arange(SUBLANES)[:, None]
    q = jnp.arange(CONV_TILES)[None, :]
    d = SUBLANES * q + r
    s = jnp.arange(SUBLANES)[None, None, :, None]
    coef = jnp.where(s >= r[:, :, None, None], wp[d][:, :, None, :], wp_prev[d][:, :, None, :])
    return coef.reshape(SUBLANES * CONV_TILES * SUBLANES, c)


def kernel(x_prompt, x_sample, p_prompt, p_sample, state_conv_a, state_ffn_conv, g_mix, w_in, w_dw_a, b_dw_a, g_ln_a, b_ln_a, g_ln_v, b_ln_v, w_s, b_s, w_out, g_ffn, w_up, w_dw_f, b_dw_f, w_down, g_ple, w_ple_gate, w_ple_proj, g_final):
    depth = w_in.shape[0]
    assert depth == 1, "single-layer step"
    batch, seq, _ = x_prompt.shape
    dec_batch, dec_seq, _ = x_sample.shape
    row = lambda v: v.reshape(1, -1)

    def bias_rows(length):
        b = jnp.tile(b_s[0][:, :length], (1, CHUNK // length))
        return jnp.repeat(b.T, HEAD_DIM, axis=1)

    def mix_weights(length):
        if length == CHUNK:
            return w_s[0]
        sel = (jnp.arange(CHUNK)[:, None] % length == jnp.arange(length)[None, :]).astype(F32)
        return jnp.einsum('ti,hij,sj->hts', sel, w_s[0][:, :length, :length], sel,
                          precision=lax.Precision.HIGHEST)

    mixer_common = (row(g_mix[0]), w_in[0].astype(BF16), _conv_coef(w_dw_a[0]), row(b_dw_a[0]),
                    row(g_ln_a[0]), row(b_ln_a[0]), row(g_ln_v[0]), row(b_ln_v[0]))
    w_out_b = w_out[0].astype(BF16)
    ffn_params = (row(g_ffn[0]), w_up[0].astype(BF16), w_dw_f[0], row(b_dw_f[0]),
                  w_down[0].astype(BF16))
    ple_params = (row(g_ple[0]), w_ple_gate[0].astype(BF16), w_ple_proj[0].astype(BF16),
                  row(g_final))

    xp = x_prompt.reshape(batch * seq, D_MODEL)
    lp = min(seq, CHUNK)
    xp, conv_p, cv_p = _mixer(xp, None, seq,
                              mixer_common + (mix_weights(lp), bias_rows(lp), w_out_b), tm=256)
    xp, lastv_p, lastg_p = _ffn(xp, None, seq, ffn_params, tm=1024, tf=512, rc=256)
    yp = _ple(xp, p_prompt[0].reshape(batch * seq, D_PLE), ple_params, 1024, "ple_prompt")

    xs = x_sample.reshape(dec_batch * dec_seq, D_MODEL)
    ls = min(dec_seq, CHUNK)
    state_pad = jnp.pad(state_conv_a[0], ((0, 0), (HIST_A - (CONV_A - 1), 0), (0, 0)))
    xs, a_s, cv_s = _mixer(xs, state_pad, dec_seq,
                           mixer_common + (mix_weights(ls), bias_rows(ls), w_out_b), tm=256)
    xs, upv_s, upg_s = _ffn(xs, state_ffn_conv[0], dec_seq, ffn_params, tm=1024, tf=512, rc=256)
    ys = _ple(xs, p_sample[0].reshape(dec_batch * dec_seq, D_PLE), ple_params, 1024, "ple_sample")

    keep = FFN_CONV - 1
    conv_a_prompt = conv_p[:, HIST_A - (CONV_A - 1):]
    conv_a_sample = jnp.concatenate(
        [state_conv_a[0][:, dec_seq:], a_s.reshape(dec_batch, dec_seq, C_A)], axis=1)
    per_seq = lastv_p.shape[0] // batch
    ffn_prompt = jnp.concatenate([lastv_p, lastg_p], axis=-1)[per_seq - 1::per_seq, SUBLANES - keep:]
    ffn_sample = jnp.concatenate([upv_s, upg_s], axis=-1)
    return (yp.reshape(batch, seq, D_MODEL), ys.reshape(dec_batch, dec_seq, D_MODEL),
            conv_a_prompt[None], conv_a_sample[None], ffn_prompt[None], ffn_sample[None],
            cv_p[None], cv_s.reshape(dec_batch, dec_seq, C_B)[None])
```

```python
import functools

import jax
import jax.numpy as jnp
from jax import lax
from jax.experimental import pallas as pl
from jax.experimental.pallas import tpu as pltpu

D_MODEL = 2048
HEAD_DIM = 128
C_A = D_MODEL // 2
C_B = D_MODEL - C_A
N_HEADS = C_B // HEAD_DIM
CONV_A = 31
HIST_A = 32
CONV_TILES = HIST_A // 8 + 1
CHUNK = 128
FFN_CONV = 3
D_FF = 5632
D_PLE = 256
EPS = 1e-6

SUBLANES = 8
ROWS = 32
LANES_EW = 128
MIXER_SLOTS = 2
PLE_SKEW = 2
PLE_SLOTS = 3
VMEM_LIMIT = 56 * 1024 * 1024

F32 = jnp.float32
BF16 = jnp.bfloat16


def _resident(shape):
    return pl.BlockSpec(shape, lambda *_: (0,) * len(shape), pipeline_mode=pl.Buffered(1))


def _aligned(x, m):
    return x if isinstance(x, int) else pl.multiple_of(x, m)


def _row_loop(n_rows, step, fn):
    def body(c, carry):
        fn(pl.multiple_of(c * step, step))
        return carry
    lax.fori_loop(0, n_rows // step, body, 0)


def _rms_to(x_ref, g_ref, dst_ref, n_rows):
    def piece(r):
        xv = x_ref[pl.ds(r, ROWS), :]
        ms = jnp.mean(xv * xv, axis=-1, keepdims=True)
        dst_ref[pl.ds(r, ROWS), :] = (xv * lax.rsqrt(ms + EPS) * g_ref[...]).astype(BF16)
    _row_loop(n_rows, ROWS, piece)


def _head_ln(x, g, b):
    outs = []
    for h in range(x.shape[-1] // HEAD_DIM):
        xh = x[:, h * HEAD_DIM:(h + 1) * HEAD_DIM]
        mu = jnp.mean(xh, axis=-1, keepdims=True)
        xc = xh - mu
        var = jnp.mean(xc * xc, axis=-1, keepdims=True)
        outs.append(xc * lax.rsqrt(var + EPS))
    return jnp.concatenate(outs, axis=-1) * g + b


def _conv31(win3, n_out, coef_ref, lanes):
    acc = None
    for r in range(SUBLANES):
        inner = None
        for q in range(CONV_TILES):
            if q == 0 and r < HIST_A - (CONV_A - 1):
                continue
            k = (r * CONV_TILES + q) * SUBLANES
            term = coef_ref[k:k + SUBLANES, lanes] * win3[q:q + n_out]
            inner = term if inner is None else inner + term
        if r:
            inner = pltpu.roll(inner, SUBLANES - r, 1)
        acc = inner if acc is None else acc + inner
    return acc


def _mixer_body(sample, tm, seq_len, *refs):
    if sample:
        (x_ref, st_ref, gmix_ref, win_ref, coef_ref, bdw_ref, glna_ref, blna_ref, glnv_ref,
         blnv_ref, ws_ref, bsb_ref, wout_ref,
         y_ref, aout_ref, vout_ref,
         h_ref, z_ref, cat_ref, a_ref, vn_ref, wm_ref) = refs
    else:
        (x_ref, gmix_ref, win_ref, coef_ref, bdw_ref, glna_ref, blna_ref, glnv_ref,
         blnv_ref, ws_ref, bsb_ref, wout_ref,
         y_ref, aout_ref, vout_ref,
         h_ref, z_ref, cat_ref, a_ref, vn_ref, wm_ref) = refs
    n_chunks = tm // CHUNK
    lane_groups = [slice(g * LANES_EW, (g + 1) * LANES_EW) for g in range(C_A // LANES_EW)]

    _rms_to(x_ref, gmix_ref, h_ref, tm)

    ti = lax.broadcasted_iota(jnp.int32, (CHUNK, CHUNK), 0)
    si = lax.broadcasted_iota(jnp.int32, (CHUNK, CHUNK), 1)
    if seq_len >= CHUNK:
        mask = si <= ti
    else:
        mask = ((si // seq_len) == (ti // seq_len)) & ((si % seq_len) <= (ti % seq_len))
    for h in range(N_HEADS):
        wm_ref[h] = jnp.where(mask, ws_ref[h], 0.0).astype(BF16)

    if not sample:
        t = pl.program_id(1)

        @pl.when(t == 0)
        def _():
            a_ref[0:HIST_A, :] = jnp.zeros((HIST_A, C_A), F32)

        @pl.when(t > 0)
        def _():
            a_ref[0:HIST_A, :] = a_ref[tm:tm + HIST_A, :]

    a_base = 0 if sample else HIST_A

    def slot_rows(c):
        lo = (c % MIXER_SLOTS) * CHUNK
        return slice(lo, lo + CHUNK)

    def in_proj(c):
        rows = slice(c * CHUNK, (c + 1) * CHUNK)
        z_ref[slot_rows(c), :] = jnp.dot(h_ref[rows, :], win_ref[...], preferred_element_type=F32)

    def out_proj(c):
        rows = slice(c * CHUNK, (c + 1) * CHUNK)
        y_ref[rows, :] = x_ref[rows, :] + jnp.dot(cat_ref[slot_rows(c), :], wout_ref[...],
                                                   preferred_element_type=F32)

    def elementwise(c):
        r0 = c * CHUNK
        s0 = slot_rows(c).start - r0
        for r in range(r0, r0 + CHUNK, ROWS):
            for lanes in lane_groups:
                zv = z_ref[s0 + r:s0 + r + ROWS, lanes]
                zg = z_ref[s0 + r:s0 + r + ROWS, C_A + lanes.start:C_A + lanes.stop]
                a = zv * jax.nn.sigmoid(zg)
                a_ref[a_base + r:a_base + r + ROWS, lanes] = a
                if sample:
                    aout_ref[r:r + ROWS, lanes] = a
        for r in range(r0, r0 + CHUNK, ROWS):
            for lanes in lane_groups:
                width = lanes.stop - lanes.start
                if sample:
                    outs = []
                    for rs in range(r, r + ROWS, seq_len):
                        hist = st_ref[rs // seq_len, :, lanes].reshape(
                            CONV_TILES - 1, SUBLANES, width)
                        new = a_ref[rs:rs + seq_len, lanes].reshape(1, SUBLANES, width)
                        win3 = jnp.concatenate([hist, new], axis=0)
                        outs.append(_conv31(win3, 1, coef_ref, lanes))
                    conv = jnp.concatenate(outs, axis=0).reshape(ROWS, width)
                else:
                    n_out = ROWS // SUBLANES
                    win3 = a_ref[r:r + ROWS + HIST_A, lanes].reshape(
                        n_out + CONV_TILES - 1, SUBLANES, width)
                    conv = _conv31(win3, n_out, coef_ref, lanes).reshape(ROWS, width)
                conv = conv + bdw_ref[:, lanes]
                yv = _head_ln(conv, glna_ref[:, lanes], blna_ref[:, lanes])
                cat_ref[s0 + r:s0 + r + ROWS, lanes] = (yv * jax.nn.sigmoid(yv)).astype(BF16)

        v_off = 2 * C_A + C_B
        for r in range(r0, r0 + CHUNK, ROWS):
            for lanes in lane_groups:
                v = jax.nn.gelu(z_ref[s0 + r:s0 + r + ROWS, v_off + lanes.start:v_off + lanes.stop])
                vn = _head_ln(v, glnv_ref[:, lanes], blnv_ref[:, lanes])
                vn_ref[r - r0:r - r0 + ROWS, lanes] = vn.astype(BF16)
                if sample:
                    vout_ref[r:r + ROWS, lanes] = vn
                else:
                    vout_ref[0, r - r0:r - r0 + ROWS, lanes] = vn
        for h in range(N_HEADS):
            lo = h * HEAD_DIM
            mixed = jnp.dot(wm_ref[h], vn_ref[:, lo:lo + HEAD_DIM], preferred_element_type=F32)
            mixed = mixed + bsb_ref[:, lo:lo + HEAD_DIM]
            u = jax.nn.gelu(z_ref[slot_rows(c), 2 * C_A + lo:2 * C_A + lo + HEAD_DIM])
            cat_ref[slot_rows(c), C_A + lo:C_A + lo + HEAD_DIM] = (u * mixed).astype(BF16)

    in_proj(0)
    for c in range(n_chunks):
        if c + 1 < n_chunks:
            in_proj(c + 1)
        elementwise(c)
        out_proj(c)

    if not sample:
        aout_ref[0] = a_ref[tm:tm + HIST_A, :]


def _mixer(x, state, seq_len, params, tm):
    (g_mix, w_in, w_dw, b_dw, g_ln_a, b_ln_a, g_ln_v, b_ln_v, w_s, bias_rows, w_out) = params
    rows = x.shape[0]
    sample = state is not None
    vec = lambda n: _resident((1, n))
    common_in = [vec(D_MODEL), _resident((D_MODEL, 2 * C_A + 2 * C_B)),
                 _resident((SUBLANES * CONV_TILES * SUBLANES, C_A)),
                 vec(C_A), vec(C_A), vec(C_A), vec(C_B), vec(C_B),
                 _resident((N_HEADS, CHUNK, CHUNK)), _resident((CHUNK, C_B)),
                 _resident((C_A + C_B, D_MODEL))]
    ring = MIXER_SLOTS * CHUNK
    common_scratch = [pltpu.VMEM((tm, D_MODEL), BF16),
                      pltpu.VMEM((ring, 2 * C_A + 2 * C_B), F32),
                      pltpu.VMEM((ring, C_A + C_B), BF16)]
    tail_scratch = [pltpu.VMEM((CHUNK, C_B), BF16),
                    pltpu.VMEM((N_HEADS, CHUNK, CHUNK), BF16)]
    if sample:
        n_seq = rows // seq_len
        spt = tm // seq_len
        grid = (rows // tm,)
        row_map = lambda i: (i, 0)
        in_specs = [pl.BlockSpec((tm, D_MODEL), row_map),
                    pl.BlockSpec((spt, HIST_A, C_A), lambda i: (i, 0, 0))] + common_in
        out_specs = [pl.BlockSpec((tm, D_MODEL), row_map),
                     pl.BlockSpec((tm, C_A), row_map),
                     pl.BlockSpec((tm, C_B), row_map)]
        out_shape = [jax.ShapeDtypeStruct((rows, D_MODEL), F32),
                     jax.ShapeDtypeStruct((rows, C_A), F32),
                     jax.ShapeDtypeStruct((rows, C_B), F32)]
        scratch = common_scratch + [pltpu.VMEM((tm, C_A), F32)] + tail_scratch
        args = (x, state)
        sem = ("arbitrary",)
        del n_seq
    else:
        n_seq = rows // seq_len
        nt = seq_len // tm
        grid = (n_seq, nt)
        row_map = lambda b, t: (b * nt + t, 0)
        in_specs = [pl.BlockSpec((tm, D_MODEL), row_map)] + common_in
        out_specs = [pl.BlockSpec((tm, D_MODEL), row_map),
                     pl.BlockSpec((1, HIST_A, C_A), lambda b, t: (b, 0, 0)),
                     pl.BlockSpec((1, CHUNK, C_B), lambda b, t: (b, 0, 0))]
        out_shape = [jax.ShapeDtypeStruct((rows, D_MODEL), F32),
                     jax.ShapeDtypeStruct((n_seq, HIST_A, C_A), F32),
                     jax.ShapeDtypeStruct((n_seq, CHUNK, C_B), F32)]
        scratch = common_scratch + [pltpu.VMEM((tm + HIST_A, C_A), F32)] + tail_scratch
        args = (x,)
        sem = ("arbitrary", "arbitrary")
    return pl.pallas_call(
        functools.partial(_mixer_body, sample, tm, seq_len),
        grid=grid, in_specs=in_specs, out_specs=out_specs, out_shape=out_shape,
        scratch_shapes=scratch,
        compiler_params=pltpu.CompilerParams(dimension_semantics=sem,
                                             vmem_limit_bytes=VMEM_LIMIT),
        name="mixer_sample" if sample else "mixer_prompt",
    )(*args, g_mix, w_in, w_dw, b_dw, g_ln_a, b_ln_a, g_ln_v, b_ln_v, w_s, bias_rows, w_out)


def _ffn_body(sample, tm, tf, rc, blocks_per_seq, *refs):
    if sample:
        (x_ref, stv_ref, stg_ref, g_ref, wuv_ref, wug_ref, wdv_ref, wdg_ref, bdv_ref, bdg_ref,
         wdn_ref, y_ref, lastv_ref, lastg_ref, h_ref, upv_ref, upg_ref, gate_ref) = refs
    else:
        (x_ref, g_ref, wuv_ref, wug_ref, wdv_ref, wdg_ref, bdv_ref, bdg_ref,
         wdn_ref, y_ref, lastv_ref, lastg_ref, h_ref, upv_ref, upg_ref, gate_ref, carry_ref) = refs
    i = pl.program_id(0)
    j = pl.program_id(1)

    @pl.when(j == 0)
    def _():
        _rms_to(x_ref, g_ref, h_ref, tm)
        y_ref[...] = x_ref[...]
        if not sample:
            @pl.when(i == 0)
            def _():
                carry_ref[...] = jnp.zeros(carry_ref.shape, F32)

    hist = SUBLANES
    starts = list(range(0, tm - rc, rc)) + [tm - rc, tm - rc // 2]
    chunks = [(r0, r1 - r0) for r0, r1 in zip(starts, starts[1:] + [tm])]
    ups = ((upv_ref, wuv_ref, lastv_ref), (upg_ref, wug_ref, lastg_ref))

    if not sample:
        first = (i % blocks_per_seq) == 0
        for half, (up_ref, _, _) in enumerate(ups):
            up_ref[0:hist, :] = jnp.where(first, 0.0, carry_ref[j, half])

    def up_proj(c):
        r0, n = chunks[c]
        hc = h_ref[r0:r0 + n, :]
        for up_ref, wu_ref, _ in ups:
            up_ref[hist + r0:hist + r0 + n, :] = jnp.dot(hc, wu_ref[...],
                                                         preferred_element_type=F32)

    def conv(up_ref, st_ref, wd_ref, bd_ref, row):
        w0 = wd_ref[0:1, :]
        w1 = wd_ref[1:2, :]
        w2 = wd_ref[2:3, :]
        if sample:
            n = ROWS // SUBLANES
            s0 = _aligned(row // SUBLANES, n)
            cur3 = up_ref[pl.ds(hist + row, ROWS), :].reshape(n, SUBLANES, tf)
            sl = lax.broadcasted_iota(jnp.int32, (n, SUBLANES, tf), 1)
            st0 = st_ref[pl.ds(s0, n), 0:1, :]
            st1 = st_ref[pl.ds(s0, n), 1:2, :]
            m1 = jnp.where(sl == 0, st1, pltpu.roll(cur3, 1, 1))
            m2 = jnp.where(sl == 0, st0, jnp.where(sl == 1, st1, pltpu.roll(cur3, 2, 1)))
            out = w0 * m2 + w1 * m1 + w2 * cur3 + bd_ref[...]
            return out.reshape(ROWS, tf)
        n = ROWS // SUBLANES
        win3 = up_ref[pl.ds(row, ROWS + hist), :].reshape(n + 1, SUBLANES, tf)
        top = lax.broadcasted_iota(jnp.int32, (n, SUBLANES, tf), 1) == 0
        r0 = pltpu.roll(w0 * win3, 1, 1)
        s0 = jnp.concatenate([r0[0:1], jnp.where(top, r0[0:n], r0[1:n + 1])], axis=0)
        r1 = pltpu.roll(w1 * win3 + s0, 1, 1)
        s1 = jnp.where(top, r1[0:n], r1[1:n + 1])
        return (w2 * win3[1:n + 1] + s1 + bd_ref[...]).reshape(ROWS, tf)

    def gate_down(c):
        r0, n = chunks[c]
        for r in range(0, n, ROWS):
            cv = conv(upv_ref, stv_ref if sample else None, wdv_ref, bdv_ref, r0 + r)
            cg = conv(upg_ref, stg_ref if sample else None, wdg_ref, bdg_ref, r0 + r)
            gate_ref[r:r + ROWS, :] = (cv * jax.nn.sigmoid(cv) * cg).astype(BF16)
        y_ref[r0:r0 + n, :] += jnp.dot(gate_ref[0:n, :], wdn_ref[...],
                                       preferred_element_type=F32)

    up_proj(0)
    for c in range(len(chunks) - 1):
        up_proj(c + 1)
        gate_down(c)
    gate_down(len(chunks) - 1)

    for half, (up_ref, _, last_ref) in enumerate(ups):
        if sample:
            up3 = up_ref[hist:hist + tm, :].reshape(tm // SUBLANES, SUBLANES, tf)
            last_ref[...] = up3[:, SUBLANES - (FFN_CONV - 1):, :]
        else:
            tail = up_ref[tm:tm + hist, :]
            carry_ref[j, half] = tail
            last_ref[0] = tail


def _ffn(x, state, seq_len, params, tm, tf, rc):
    g_ffn, w_up, w_dw, b_dw, w_down = params
    rows = x.shape[0]
    sample = state is not None
    nj = D_FF // tf
    grid = (rows // tm, nj)
    row_map = lambda i, j: (i, 0)
    val_map = lambda i, j: (0, j)
    gate_map = lambda i, j: (0, nj + j)
    in_specs = [pl.BlockSpec((tm, D_MODEL), row_map)]
    args = [x]
    if sample:
        spt = tm // seq_len
        in_specs += [pl.BlockSpec((spt, FFN_CONV - 1, tf), lambda i, j: (i, 0, j)),
                     pl.BlockSpec((spt, FFN_CONV - 1, tf), lambda i, j: (i, 0, nj + j))]
        args += [state, state]
    in_specs += [pl.BlockSpec((1, D_MODEL), lambda i, j: (0, 0)),
                 pl.BlockSpec((D_MODEL, tf), val_map), pl.BlockSpec((D_MODEL, tf), gate_map),
                 pl.BlockSpec((FFN_CONV, tf), val_map), pl.BlockSpec((FFN_CONV, tf), gate_map),
                 pl.BlockSpec((1, tf), val_map), pl.BlockSpec((1, tf), gate_map),
                 pl.BlockSpec((tf, D_MODEL), lambda i, j: (j, 0))]
    args += [g_ffn, w_up, w_up, w_dw, w_dw, b_dw, b_dw, w_down]
    scratch = [pltpu.VMEM((tm, D_MODEL), BF16),
               pltpu.VMEM((tm + SUBLANES, tf), F32), pltpu.VMEM((tm + SUBLANES, tf), F32),
               pltpu.VMEM((rc, tf), BF16)]
    if sample:
        last_specs = [pl.BlockSpec((spt, FFN_CONV - 1, tf), lambda i, j: (i, 0, j))] * 2
        last_shape = [jax.ShapeDtypeStruct((rows // seq_len, FFN_CONV - 1, D_FF), F32)] * 2
        blocks_per_seq = 1
    else:
        blocks_per_seq = seq_len // tm
        last_specs = [pl.BlockSpec((1, SUBLANES, tf), lambda i, j: (i, 0, j))] * 2
        last_shape = [jax.ShapeDtypeStruct((rows // tm, SUBLANES, D_FF), F32)] * 2
        scratch += [pltpu.VMEM((nj, 2, SUBLANES, tf), F32)]
    return pl.pallas_call(
        functools.partial(_ffn_body, sample, tm, tf, rc, blocks_per_seq),
        grid=grid, in_specs=in_specs,
        out_specs=[pl.BlockSpec((tm, D_MODEL), row_map)] + last_specs,
        out_shape=[jax.ShapeDtypeStruct((rows, D_MODEL), F32)] + last_shape,
        scratch_shapes=scratch,
        compiler_params=pltpu.CompilerParams(dimension_semantics=("arbitrary", "arbitrary"),
                                             vmem_limit_bytes=VMEM_LIMIT),
        name="ffn_sample" if sample else "ffn_prompt",
    )(*args)


def _ple_body(tm, x_ref, p_ref, g_ref, wg_ref, wp_ref, gf_ref, y_ref, h_ref, gate_ref, proj_ref):
    n_chunks = tm // CHUNK
    never = pl.program_id(0) < 0
    pieces = CHUNK // ROWS
    anchors = {}

    def norm(c):
        slot = (c % PLE_SLOTS) * CHUNK
        for k in range(pieces):
            r = c * CHUNK + k * ROWS
            xv = x_ref[r:r + ROWS, :]
            ms = jnp.mean(xv * xv, axis=-1, keepdims=True)
            hv = xv * lax.rsqrt(ms + EPS) * g_ref[...]
            anchor = anchors.pop((c - PLE_SKEW, k), None)
            if anchor is not None:
                head = jnp.where(never, anchor, hv[:, :HEAD_DIM])
                hv = jnp.concatenate([head, hv[:, HEAD_DIM:]], axis=-1)
            h_ref[slot + k * ROWS:slot + (k + 1) * ROWS, :] = hv.astype(BF16)

    def projections(c):
        slot = (c % PLE_SLOTS) * CHUNK
        rows = slice(c * CHUNK, (c + 1) * CHUNK)
        gate_ref[slot:slot + CHUNK, :] = jnp.dot(h_ref[slot:slot + CHUNK, :], wg_ref[...],
                                                 preferred_element_type=F32)
        proj_ref[slot:slot + CHUNK, :] = jnp.dot(p_ref[rows, :].astype(BF16), wp_ref[...],
                                                 preferred_element_type=F32)

    def finish(c):
        slot = (c % PLE_SLOTS) * CHUNK
        for k in range(pieces):
            r = c * CHUNK + k * ROWS
            s = slot + k * ROWS
            xv = x_ref[r:r + ROWS, :] + jax.nn.sigmoid(gate_ref[s:s + ROWS, :]) * proj_ref[s:s + ROWS, :]
            ms = jnp.mean(xv * xv, axis=-1, keepdims=True)
            yv = xv * lax.rsqrt(ms + EPS) * gf_ref[...]
            y_ref[r:r + ROWS, :] = yv
            anchors[(c, k)] = yv[:, :HEAD_DIM]

    norm(0)
    for c in range(n_chunks):
        projections(c)
        if c >= 1:
            finish(c - 1)
        if c + 1 < n_chunks:
            norm(c + 1)
    finish(n_chunks - 1)


def _ple(x, p, params, tm, name):
    g_ple, w_gate, w_proj, g_final = params
    rows = x.shape[0]
    row_map = lambda i: (i, 0)
    return pl.pallas_call(
        functools.partial(_ple_body, tm),
        grid=(rows // tm,),
        in_specs=[pl.BlockSpec((tm, D_MODEL), row_map), pl.BlockSpec((tm, D_PLE), row_map),
                  _resident((1, D_MODEL)), _resident((D_MODEL, D_MODEL)),
                  _resident((D_PLE, D_MODEL)), _resident((1, D_MODEL))],
        out_specs=pl.BlockSpec((tm, D_MODEL), row_map),
        out_shape=jax.ShapeDtypeStruct((rows, D_MODEL), F32),
        scratch_shapes=[pltpu.VMEM((PLE_SLOTS * CHUNK, D_MODEL), BF16),
                        pltpu.VMEM((PLE_SLOTS * CHUNK, D_MODEL), F32),
                        pltpu.VMEM((PLE_SLOTS * CHUNK, D_MODEL), F32)],
        compiler_params=pltpu.CompilerParams(dimension_semantics=("arbitrary",),
                                             vmem_limit_bytes=VMEM_LIMIT),
        name=name,
    )(x, p, g_ple, w_gate, w_proj, g_final)


def _conv_coef(w):
    c = w.shape[1]
    off = HIST_A - (CONV_A - 1)
    n = SUBLANES * (CONV_TILES + 1)
    wp = jnp.zeros((n, c), w.dtype).at[off:off + CONV_A].set(w)
    wp_prev = jnp.concatenate([jnp.zeros((SUBLANES, c), w.dtype), wp[:n - SUBLANES]], axis=0)
    r = jnp.arange(SUBLANES)[:, None]
    q = jnp.arange(CONV_TILES)[None, :]
    d = SUBLANES * q + r
    s = jnp.arange(SUBLANES)[None, None, :, None]
    coef = jnp.where(s >= r[:, :, None, None], wp[d][:, :, None, :], wp_prev[d][:, :, None, :])
    return coef.reshape(SUBLANES * CONV_TILES * SUBLANES, c)


def kernel(x_prompt, x_sample, p_prompt, p_sample, state_conv_a, state_ffn_conv, g_mix, w_in, w_dw_a, b_dw_a, g_ln_a, b_ln_a, g_ln_v, b_ln_v, w_s, b_s, w_out, g_ffn, w_up, w_dw_f, b_dw_f, w_down, g_ple, w_ple_gate, w_ple_proj, g_final):
    depth = w_in.shape[0]
    assert depth == 1, "single-layer step"
    batch, seq, _ = x_prompt.shape
    dec_batch, dec_seq, _ = x_sample.shape
    row = lambda v: v.reshape(1, -1)

    def bias_rows(length):
        b = jnp.tile(b_s[0][:, :length], (1, CHUNK // length))
        return jnp.repeat(b.T, HEAD_DIM, axis=1)

    def mix_weights(length):
        if length == CHUNK:
            return w_s[0]
        sel = (jnp.arange(CHUNK)[:, None] % length == jnp.arange(length)[None, :]).astype(F32)
        return jnp.einsum('ti,hij,sj->hts', sel, w_s[0][:, :length, :length], sel,
                          precision=lax.Precision.HIGHEST)

    mixer_common = (row(g_mix[0]), w_in[0].astype(BF16), _conv_coef(w_dw_a[0]), row(b_dw_a[0]),
                    row(g_ln_a[0]), row(b_ln_a[0]), row(g_ln_v[0]), row(b_ln_v[0]))
    w_out_b = w_out[0].astype(BF16)
    ffn_params = (row(g_ffn[0]), w_up[0].astype(BF16), w_dw_f[0], row(b_dw_f[0]),
                  w_down[0].astype(BF16))
    ple_params = (row(g_ple[0]), w_ple_gate[0].astype(BF16), w_ple_proj[0].astype(BF16),
                  row(g_final))

    xp = x_prompt.reshape(batch * seq, D_MODEL)
    lp = min(seq, CHUNK)
    xp, conv_p, cv_p = _mixer(xp, None, seq,
                              mixer_common + (mix_weights(lp), bias_rows(lp), w_out_b), tm=256)
    xp, lastv_p, lastg_p = _ffn(xp, None, seq, ffn_params, tm=1024, tf=512, rc=256)
    yp = _ple(xp, p_prompt[0].reshape(batch * seq, D_PLE), ple_params, 1024, "ple_prompt")

    xs = x_sample.reshape(dec_batch * dec_seq, D_MODEL)
    ls = min(dec_seq, CHUNK)
    state_pad = jnp.pad(state_conv_a[0], ((0, 0), (HIST_A - (CONV_A - 1), 0), (0, 0)))
    xs, a_s, cv_s = _mixer(xs, state_pad, dec_seq,
                           mixer_common + (mix_weights(ls), bias_rows(ls), w_out_b), tm=256)
    xs, upv_s, upg_s = _ffn(xs, state_ffn_conv[0], dec_seq, ffn_params, tm=1024, tf=512, rc=256)
    ys = _ple(xs, p_sample[0].reshape(dec_batch * dec_seq, D_PLE), ple_params, 1024, "ple_sample")

    keep = FFN_CONV - 1
    conv_a_prompt = conv_p[:, HIST_A - (CONV_A - 1):]
    conv_a_sample = jnp.concatenate(
        [state_conv_a[0][:, dec_seq:], a_s.reshape(dec_batch, dec_seq, C_A)], axis=1)
    per_seq = lastv_p.shape[0] // batch
    ffn_prompt = jnp.concatenate([lastv_p, lastg_p], axis=-1)[per_seq - 1::per_seq, SUBLANES - keep:]
    ffn_sample = jnp.concatenate([upv_s, upg_s], axis=-1)
    return (yp.reshape(batch, seq, D_MODEL), ys.reshape(dec_batch, dec_seq, D_MODEL),
            conv_a_prompt[None], conv_a_sample[None], ffn_prompt[None], ffn_sample[None],
            cv_p[None], cv_s.reshape(dec_batch, dec_seq, C_B)[None])
```

```python
import functools

import jax
import jax.numpy as jnp
from jax import lax
from jax.experimental import pallas as pl
from jax.experimental.pallas import tpu as pltpu

D_MODEL = 2048
HEAD_DIM = 128
C_A = D_MODEL // 2
C_B = D_MODEL - C_A
N_HEADS = C_B // HEAD_DIM
CONV_A = 31
HIST_A = 32
CONV_TILES = HIST_A // 8 + 1
CHUNK = 128
FFN_CONV = 3
D_FF = 5632
D_PLE = 256
EPS = 1e-6

SUBLANES = 8
ROWS = 32
LANES_EW = 128
DOT_N = 512
MIXER_SLOTS = 2
PLE_SKEW = 2
PLE_SLOTS = 3
VMEM_LIMIT = 56 * 1024 * 1024

F32 = jnp.float32
BF16 = jnp.bfloat16


def _resident(shape):
    return pl.BlockSpec(shape, lambda *_: (0,) * len(shape), pipeline_mode=pl.Buffered(1))


def _aligned(x, m):
    return x if isinstance(x, int) else pl.multiple_of(x, m)


def _row_loop(n_rows, step, fn):
    def body(c, carry):
        fn(pl.multiple_of(c * step, step))
        return carry
    lax.fori_loop(0, n_rows // step, body, 0)


def _rms_to(x_ref, g_ref, dst_ref, n_rows):
    def piece(r):
        xv = x_ref[pl.ds(r, ROWS), :]
        ms = jnp.mean(xv * xv, axis=-1, keepdims=True)
        dst_ref[pl.ds(r, ROWS), :] = (xv * lax.rsqrt(ms + EPS) * g_ref[...]).astype(BF16)
    _row_loop(n_rows, ROWS, piece)


def _head_ln(x, g, b):
    outs = []
    for h in range(x.shape[-1] // HEAD_DIM):
        xh = x[:, h * HEAD_DIM:(h + 1) * HEAD_DIM]
        mu = jnp.mean(xh, axis=-1, keepdims=True)
        xc = xh - mu
        var = jnp.mean(xc * xc, axis=-1, keepdims=True)
        outs.append(xc * lax.rsqrt(var + EPS))
    return jnp.concatenate(outs, axis=-1) * g + b


def _conv31(win3, n_out, coef_ref, lanes):
    acc = None
    for r in range(SUBLANES):
        inner = None
        for q in range(CONV_TILES):
            if q == 0 and r < HIST_A - (CONV_A - 1):
                continue
            k = (r * CONV_TILES + q) * SUBLANES
            term = coef_ref[k:k + SUBLANES, lanes] * win3[q:q + n_out]
            inner = term if inner is None else inner + term
        if r:
            inner = pltpu.roll(inner, SUBLANES - r, 1)
        acc = inner if acc is None else acc + inner
    return acc


def _mixer_body(sample, tm, seq_len, *refs):
    if sample:
        (x_ref, st_ref, gmix_ref, win_ref, coef_ref, bdw_ref, glna_ref, blna_ref, glnv_ref,
         blnv_ref, ws_ref, bsb_ref, wout_ref,
         y_ref, aout_ref, vout_ref,
         h_ref, z_ref, cat_ref, a_ref, vn_ref, wm_ref) = refs
    else:
        (x_ref, gmix_ref, win_ref, coef_ref, bdw_ref, glna_ref, blna_ref, glnv_ref,
         blnv_ref, ws_ref, bsb_ref, wout_ref,
         y_ref, aout_ref, vout_ref,
         h_ref, z_ref, cat_ref, a_ref, vn_ref, wm_ref) = refs
    n_chunks = tm // CHUNK
    lane_groups = [slice(g * LANES_EW, (g + 1) * LANES_EW) for g in range(C_A // LANES_EW)]

    _rms_to(x_ref, gmix_ref, h_ref, tm)

    ti = lax.broadcasted_iota(jnp.int32, (CHUNK, CHUNK), 0)
    si = lax.broadcasted_iota(jnp.int32, (CHUNK, CHUNK), 1)
    if seq_len >= CHUNK:
        mask = si <= ti
    else:
        mask = ((si // seq_len) == (ti // seq_len)) & ((si % seq_len) <= (ti % seq_len))
    for h in range(N_HEADS):
        wm_ref[h] = jnp.where(mask, ws_ref[h], 0.0).astype(BF16)

    if not sample:
        t = pl.program_id(1)

        @pl.when(t == 0)
        def _():
            a_ref[0:HIST_A, :] = jnp.zeros((HIST_A, C_A), F32)

        @pl.when(t > 0)
        def _():
            a_ref[0:HIST_A, :] = a_ref[tm:tm + HIST_A, :]

    a_base = 0 if sample else HIST_A

    def slot_rows(c):
        lo = (c % MIXER_SLOTS) * CHUNK
        return slice(lo, lo + CHUNK)

    def in_proj(c):
        rows = slice(c * CHUNK, (c + 1) * CHUNK)
        hv = h_ref[rows, :]
        for g in range(win_ref.shape[0]):
            z_ref[slot_rows(c), g * DOT_N:(g + 1) * DOT_N] = jnp.dot(
                hv, win_ref[g], preferred_element_type=F32)

    def out_proj(c):
        rows = slice(c * CHUNK, (c + 1) * CHUNK)
        cv = cat_ref[slot_rows(c), :]
        for g in range(wout_ref.shape[0]):
            cols = slice(g * DOT_N, (g + 1) * DOT_N)
            y_ref[rows, cols] = x_ref[rows, cols] + jnp.dot(cv, wout_ref[g],
                                                            preferred_element_type=F32)

    def elementwise(c):
        r0 = c * CHUNK
        s0 = slot_rows(c).start - r0
        for r in range(r0, r0 + CHUNK, ROWS):
            for lanes in lane_groups:
                zv = z_ref[s0 + r:s0 + r + ROWS, lanes]
                zg = z_ref[s0 + r:s0 + r + ROWS, C_A + lanes.start:C_A + lanes.stop]
                a = zv * jax.nn.sigmoid(zg)
                a_ref[a_base + r:a_base + r + ROWS, lanes] = a
                if sample:
                    aout_ref[r:r + ROWS, lanes] = a
        for r in range(r0, r0 + CHUNK, ROWS):
            for lanes in lane_groups:
                width = lanes.stop - lanes.start
                if sample:
                    outs = []
                    for rs in range(r, r + ROWS, seq_len):
                        hist = st_ref[rs // seq_len, :, lanes].reshape(
                            CONV_TILES - 1, SUBLANES, width)
                        new = a_ref[rs:rs + seq_len, lanes].reshape(1, SUBLANES, width)
                        win3 = jnp.concatenate([hist, new], axis=0)
                        outs.append(_conv31(win3, 1, coef_ref, lanes))
                    conv = jnp.concatenate(outs, axis=0).reshape(ROWS, width)
                else:
                    n_out = ROWS // SUBLANES
                    win3 = a_ref[r:r + ROWS + HIST_A, lanes].reshape(
                        n_out + CONV_TILES - 1, SUBLANES, width)
                    conv = _conv31(win3, n_out, coef_ref, lanes).reshape(ROWS, width)
                conv = conv + bdw_ref[:, lanes]
                yv = _head_ln(conv, glna_ref[:, lanes], blna_ref[:, lanes])
                cat_ref[s0 + r:s0 + r + ROWS, lanes] = (yv * jax.nn.sigmoid(yv)).astype(BF16)

        v_off = 2 * C_A + C_B
        for r in range(r0, r0 + CHUNK, ROWS):
            for lanes in lane_groups:
                v = jax.nn.gelu(z_ref[s0 + r:s0 + r + ROWS, v_off + lanes.start:v_off + lanes.stop])
                vn = _head_ln(v, glnv_ref[:, lanes], blnv_ref[:, lanes])
                vn_ref[r - r0:r - r0 + ROWS, lanes] = vn.astype(BF16)
                if sample:
                    vout_ref[r:r + ROWS, lanes] = vn
                else:
                    vout_ref[0, r - r0:r - r0 + ROWS, lanes] = vn
        for h in range(N_HEADS):
            lo = h * HEAD_DIM
            mixed = jnp.dot(wm_ref[h], vn_ref[:, lo:lo + HEAD_DIM], preferred_element_type=F32)
            mixed = mixed + bsb_ref[:, lo:lo + HEAD_DIM]
            u = jax.nn.gelu(z_ref[slot_rows(c), 2 * C_A + lo:2 * C_A + lo + HEAD_DIM])
            cat_ref[slot_rows(c), C_A + lo:C_A + lo + HEAD_DIM] = (u * mixed).astype(BF16)

    in_proj(0)
    for c in range(n_chunks):
        if c + 1 < n_chunks:
            in_proj(c + 1)
        elementwise(c)
        out_proj(c)

    if not sample:
        aout_ref[0] = a_ref[tm:tm + HIST_A, :]


def _mixer(x, state, seq_len, params, tm):
    (g_mix, w_in, w_dw, b_dw, g_ln_a, b_ln_a, g_ln_v, b_ln_v, w_s, bias_rows, w_out) = params
    rows = x.shape[0]
    sample = state is not None
    vec = lambda n: _resident((1, n))
    common_in = [vec(D_MODEL), _resident(((2 * C_A + 2 * C_B) // DOT_N, D_MODEL, DOT_N)),
                 _resident((SUBLANES * CONV_TILES * SUBLANES, C_A)),
                 vec(C_A), vec(C_A), vec(C_A), vec(C_B), vec(C_B),
                 _resident((N_HEADS, CHUNK, CHUNK)), _resident((CHUNK, C_B)),
                 _resident((D_MODEL // DOT_N, C_A + C_B, DOT_N))]
    ring = MIXER_SLOTS * CHUNK
    common_scratch = [pltpu.VMEM((tm, D_MODEL), BF16),
                      pltpu.VMEM((ring, 2 * C_A + 2 * C_B), F32),
                      pltpu.VMEM((ring, C_A + C_B), BF16)]
    tail_scratch = [pltpu.VMEM((CHUNK, C_B), BF16),
                    pltpu.VMEM((N_HEADS, CHUNK, CHUNK), BF16)]
    if sample:
        n_seq = rows // seq_len
        spt = tm // seq_len
        grid = (rows // tm,)
        row_map = lambda i: (i, 0)
        in_specs = [pl.BlockSpec((tm, D_MODEL), row_map),
                    pl.BlockSpec((spt, HIST_A, C_A), lambda i: (i, 0, 0))] + common_in
        out_specs = [pl.BlockSpec((tm, D_MODEL), row_map),
                     pl.BlockSpec((tm, C_A), row_map),
                     pl.BlockSpec((tm, C_B), row_map)]
        out_shape = [jax.ShapeDtypeStruct((rows, D_MODEL), F32),
                     jax.ShapeDtypeStruct((rows, C_A), F32),
                     jax.ShapeDtypeStruct((rows, C_B), F32)]
        scratch = common_scratch + [pltpu.VMEM((tm, C_A), F32)] + tail_scratch
        args = (x, state)
        sem = ("arbitrary",)
        del n_seq
    else:
        n_seq = rows // seq_len
        nt = seq_len // tm
        grid = (n_seq, nt)
        row_map = lambda b, t: (b * nt + t, 0)
        in_specs = [pl.BlockSpec((tm, D_MODEL), row_map)] + common_in
        out_specs = [pl.BlockSpec((tm, D_MODEL), row_map),
                     pl.BlockSpec((1, HIST_A, C_A), lambda b, t: (b, 0, 0)),
                     pl.BlockSpec((1, CHUNK, C_B), lambda b, t: (b, 0, 0))]
        out_shape = [jax.ShapeDtypeStruct((rows, D_MODEL), F32),
                     jax.ShapeDtypeStruct((n_seq, HIST_A, C_A), F32),
                     jax.ShapeDtypeStruct((n_seq, CHUNK, C_B), F32)]
        scratch = common_scratch + [pltpu.VMEM((tm + HIST_A, C_A), F32)] + tail_scratch
        args = (x,)
        sem = ("arbitrary", "arbitrary")
    return pl.pallas_call(
        functools.partial(_mixer_body, sample, tm, seq_len),
        grid=grid, in_specs=in_specs, out_specs=out_specs, out_shape=out_shape,
        scratch_shapes=scratch,
        compiler_params=pltpu.CompilerParams(dimension_semantics=sem,
                                             vmem_limit_bytes=VMEM_LIMIT),
        name="mixer_sample" if sample else "mixer_prompt",
    )(*args, g_mix, w_in, w_dw, b_dw, g_ln_a, b_ln_a, g_ln_v, b_ln_v, w_s, bias_rows, w_out)


def _ffn_body(sample, tm, tf, rc, blocks_per_seq, *refs):
    if sample:
        (x_ref, stv_ref, stg_ref, g_ref, wuv_ref, wug_ref, wdv_ref, wdg_ref, bdv_ref, bdg_ref,
         wdn_ref, y_ref, lastv_ref, lastg_ref, h_ref, upv_ref, upg_ref, gate_ref) = refs
    else:
        (x_ref, g_ref, wuv_ref, wug_ref, wdv_ref, wdg_ref, bdv_ref, bdg_ref,
         wdn_ref, y_ref, lastv_ref, lastg_ref, h_ref, upv_ref, upg_ref, gate_ref, carry_ref) = refs
    i = pl.program_id(0)
    j = pl.program_id(1)

    @pl.when(j == 0)
    def _():
        _rms_to(x_ref, g_ref, h_ref, tm)
        y_ref[...] = x_ref[...]
        if not sample:
            @pl.when(i == 0)
            def _():
                carry_ref[...] = jnp.zeros(carry_ref.shape, F32)

    hist = SUBLANES
    starts = list(range(0, tm - rc, rc)) + [tm - rc, tm - rc // 2]
    chunks = [(r0, r1 - r0) for r0, r1 in zip(starts, starts[1:] + [tm])]
    ups = ((upv_ref, wuv_ref, lastv_ref), (upg_ref, wug_ref, lastg_ref))

    if not sample:
        first = (i % blocks_per_seq) == 0
        for half, (up_ref, _, _) in enumerate(ups):
            up_ref[0:hist, :] = jnp.where(first, 0.0, carry_ref[j, half])

    def up_proj(c):
        r0, n = chunks[c]
        hc = h_ref[r0:r0 + n, :]
        for up_ref, wu_ref, _ in ups:
            up_ref[hist + r0:hist + r0 + n, :] = jnp.dot(hc, wu_ref[...],
                                                         preferred_element_type=F32)

    def conv(up_ref, st_ref, wd_ref, bd_ref, row):
        w0 = wd_ref[0:1, :]
        w1 = wd_ref[1:2, :]
        w2 = wd_ref[2:3, :]
        if sample:
            n = ROWS // SUBLANES
            s0 = _aligned(row // SUBLANES, n)
            cur3 = up_ref[pl.ds(hist + row, ROWS), :].reshape(n, SUBLANES, tf)
            sl = lax.broadcasted_iota(jnp.int32, (n, SUBLANES, tf), 1)
            st0 = st_ref[pl.ds(s0, n), 0:1, :]
            st1 = st_ref[pl.ds(s0, n), 1:2, :]
            m1 = jnp.where(sl == 0, st1, pltpu.roll(cur3, 1, 1))
            m2 = jnp.where(sl == 0, st0, jnp.where(sl == 1, st1, pltpu.roll(cur3, 2, 1)))
            out = w0 * m2 + w1 * m1 + w2 * cur3 + bd_ref[...]
            return out.reshape(ROWS, tf)
        n = ROWS // SUBLANES
        win3 = up_ref[pl.ds(row, ROWS + hist), :].reshape(n + 1, SUBLANES, tf)
        top = lax.broadcasted_iota(jnp.int32, (n, SUBLANES, tf), 1) == 0
        r0 = pltpu.roll(w0 * win3, 1, 1)
        s0 = jnp.concatenate([r0[0:1], jnp.where(top, r0[0:n], r0[1:n + 1])], axis=0)
        r1 = pltpu.roll(w1 * win3 + s0, 1, 1)
        s1 = jnp.where(top, r1[0:n], r1[1:n + 1])
        return (w2 * win3[1:n + 1] + s1 + bd_ref[...]).reshape(ROWS, tf)

    def gate_down(c):
        r0, n = chunks[c]
        for r in range(0, n, ROWS):
            cv = conv(upv_ref, stv_ref if sample else None, wdv_ref, bdv_ref, r0 + r)
            cg = conv(upg_ref, stg_ref if sample else None, wdg_ref, bdg_ref, r0 + r)
            gate_ref[r:r + ROWS, :] = (cv * jax.nn.sigmoid(cv) * cg).astype(BF16)
        gv = gate_ref[0:n, :]
        for g in range(wdn_ref.shape[0]):
            cols = slice(g * DOT_N, (g + 1) * DOT_N)
            y_ref[r0:r0 + n, cols] += jnp.dot(gv, wdn_ref[g], preferred_element_type=F32)

    up_proj(0)
    for c in range(len(chunks) - 1):
        up_proj(c + 1)
        gate_down(c)
    gate_down(len(chunks) - 1)

    for half, (up_ref, _, last_ref) in enumerate(ups):
        if sample:
            up3 = up_ref[hist:hist + tm, :].reshape(tm // SUBLANES, SUBLANES, tf)
            last_ref[...] = up3[:, SUBLANES - (FFN_CONV - 1):, :]
        else:
            tail = up_ref[tm:tm + hist, :]
            carry_ref[j, half] = tail
            last_ref[0] = tail


def _ffn(x, state, seq_len, params, tm, tf, rc):
    g_ffn, w_up, w_dw, b_dw, w_down = params
    rows = x.shape[0]
    sample = state is not None
    nj = D_FF // tf
    grid = (rows // tm, nj)
    row_map = lambda i, j: (i, 0)
    val_map = lambda i, j: (0, j)
    gate_map = lambda i, j: (0, nj + j)
    in_specs = [pl.BlockSpec((tm, D_MODEL), row_map)]
    args = [x]
    if sample:
        spt = tm // seq_len
        in_specs += [pl.BlockSpec((spt, FFN_CONV - 1, tf), lambda i, j: (i, 0, j)),
                     pl.BlockSpec((spt, FFN_CONV - 1, tf), lambda i, j: (i, 0, nj + j))]
        args += [state, state]
    in_specs += [pl.BlockSpec((1, D_MODEL), lambda i, j: (0, 0)),
                 pl.BlockSpec((D_MODEL, tf), val_map), pl.BlockSpec((D_MODEL, tf), gate_map),
                 pl.BlockSpec((FFN_CONV, tf), val_map), pl.BlockSpec((FFN_CONV, tf), gate_map),
                 pl.BlockSpec((1, tf), val_map), pl.BlockSpec((1, tf), gate_map),
                 pl.BlockSpec((D_MODEL // DOT_N, tf, DOT_N), lambda i, j: (0, j, 0))]
    args += [g_ffn, w_up, w_up, w_dw, w_dw, b_dw, b_dw, w_down]
    scratch = [pltpu.VMEM((tm, D_MODEL), BF16),
               pltpu.VMEM((tm + SUBLANES, tf), F32), pltpu.VMEM((tm + SUBLANES, tf), F32),
               pltpu.VMEM((rc, tf), BF16)]
    if sample:
        last_specs = [pl.BlockSpec((spt, FFN_CONV - 1, tf), lambda i, j: (i, 0, j))] * 2
        last_shape = [jax.ShapeDtypeStruct((rows // seq_len, FFN_CONV - 1, D_FF), F32)] * 2
        blocks_per_seq = 1
    else:
        blocks_per_seq = seq_len // tm
        last_specs = [pl.BlockSpec((1, SUBLANES, tf), lambda i, j: (i, 0, j))] * 2
        last_shape = [jax.ShapeDtypeStruct((rows // tm, SUBLANES, D_FF), F32)] * 2
        scratch += [pltpu.VMEM((nj, 2, SUBLANES, tf), F32)]
    return pl.pallas_call(
        functools.partial(_ffn_body, sample, tm, tf, rc, blocks_per_seq),
        grid=grid, in_specs=in_specs,
        out_specs=[pl.BlockSpec((tm, D_MODEL), row_map)] + last_specs,
        out_shape=[jax.ShapeDtypeStruct((rows, D_MODEL), F32)] + last_shape,
        scratch_shapes=scratch,
        compiler_params=pltpu.CompilerParams(dimension_semantics=("arbitrary", "arbitrary"),
                                             vmem_limit_bytes=VMEM_LIMIT),
        name="ffn_sample" if sample else "ffn_prompt",
    )(*args)


def _ple_body(tm, x_ref, p_ref, g_ref, wg_ref, wp_ref, gf_ref, y_ref, h_ref, gate_ref, proj_ref):
    n_chunks = tm // CHUNK
    never = pl.program_id(0) < 0
    pieces = CHUNK // ROWS
    anchors = {}

    def norm(c):
        slot = (c % PLE_SLOTS) * CHUNK
        for k in range(pieces):
            r = c * CHUNK + k * ROWS
            xv = x_ref[r:r + ROWS, :]
            ms = jnp.mean(xv * xv, axis=-1, keepdims=True)
            hv = xv * lax.rsqrt(ms + EPS) * g_ref[...]
            anchor = anchors.pop((c - PLE_SKEW, k), None)
            if anchor is not None:
                head = jnp.where(never, anchor, hv[:, :HEAD_DIM])
                hv = jnp.concatenate([head, hv[:, HEAD_DIM:]], axis=-1)
            h_ref[slot + k * ROWS:slot + (k + 1) * ROWS, :] = hv.astype(BF16)

    def projections(c):
        slot = (c % PLE_SLOTS) * CHUNK
        rows = slice(c * CHUNK, (c + 1) * CHUNK)
        pv = p_ref[rows, :].astype(BF16)
        for g in range(D_MODEL // DOT_N):
            cols = slice(g * DOT_N, (g + 1) * DOT_N)
            gate_ref[slot:slot + CHUNK, cols] = jnp.dot(
                h_ref[slot:slot + CHUNK, :], wg_ref[g], preferred_element_type=F32)
            proj_ref[slot:slot + CHUNK, cols] = jnp.dot(pv, wp_ref[g], preferred_element_type=F32)

    def finish(c):
        slot = (c % PLE_SLOTS) * CHUNK
        for k in range(pieces):
            r = c * CHUNK + k * ROWS
            s = slot + k * ROWS
            xv = x_ref[r:r + ROWS, :] + jax.nn.sigmoid(gate_ref[s:s + ROWS, :]) * proj_ref[s:s + ROWS, :]
            ms = jnp.mean(xv * xv, axis=-1, keepdims=True)
            yv = xv * lax.rsqrt(ms + EPS) * gf_ref[...]
            y_ref[r:r + ROWS, :] = yv
            anchors[(c, k)] = yv[:, :HEAD_DIM]

    norm(0)
    for c in range(n_chunks):
        projections(c)
        if c >= 1:
            finish(c - 1)
        if c + 1 < n_chunks:
            norm(c + 1)
    finish(n_chunks - 1)


def _ple(x, p, params, tm, name):
    g_ple, w_gate, w_proj, g_final = params
    rows = x.shape[0]
    row_map = lambda i: (i, 0)
    return pl.pallas_call(
        functools.partial(_ple_body, tm),
        grid=(rows // tm,),
        in_specs=[pl.BlockSpec((tm, D_MODEL), row_map), pl.BlockSpec((tm, D_PLE), row_map),
                  _resident((1, D_MODEL)), _resident((D_MODEL // DOT_N, D_MODEL, DOT_N)),
                  _resident((D_MODEL // DOT_N, D_PLE, DOT_N)), _resident((1, D_MODEL))],
        out_specs=pl.BlockSpec((tm, D_MODEL), row_map),
        out_shape=jax.ShapeDtypeStruct((rows, D_MODEL), F32),
        scratch_shapes=[pltpu.VMEM((PLE_SLOTS * CHUNK, D_MODEL), BF16),
                        pltpu.VMEM((PLE_SLOTS * CHUNK, D_MODEL), F32),
                        pltpu.VMEM((PLE_SLOTS * CHUNK, D_MODEL), F32)],
        compiler_params=pltpu.CompilerParams(dimension_semantics=("arbitrary",),
                                             vmem_limit_bytes=VMEM_LIMIT),
        name=name,
    )(x, p, g_ple, w_gate, w_proj, g_final)


def _panels(w):
    k, n = w.shape
    return w.astype(BF16).reshape(k, n // DOT_N, DOT_N).transpose(1, 0, 2)


def _conv_coef(w):
    c = w.shape[1]
    off = HIST_A - (CONV_A - 1)
    n = SUBLANES * (CONV_TILES + 1)
    wp = jnp.zeros((n, c), w.dtype).at[off:off + CONV_A].set(w)
    wp_prev = jnp.concatenate([jnp.zeros((SUBLANES, c), w.dtype), wp[:n - SUBLANES]], axis=0)
    r = jnp.arange(SUBLANES)[:, None]
    q = jnp.arange(CONV_TILES)[None, :]
    d = SUBLANES * q + r
    s = jnp.arange(SUBLANES)[None, None, :, None]
    coef = jnp.where(s >= r[:, :, None, None], wp[d][:, :, None, :], wp_prev[d][:, :, None, :])
    return coef.reshape(SUBLANES * CONV_TILES * SUBLANES, c)


def kernel(x_prompt, x_sample, p_prompt, p_sample, state_conv_a, state_ffn_conv, g_mix, w_in, w_dw_a, b_dw_a, g_ln_a, b_ln_a, g_ln_v, b_ln_v, w_s, b_s, w_out, g_ffn, w_up, w_dw_f, b_dw_f, w_down, g_ple, w_ple_gate, w_ple_proj, g_final):
    depth = w_in.shape[0]
    assert depth == 1, "single-layer step"
    batch, seq, _ = x_prompt.shape
    dec_batch, dec_seq, _ = x_sample.shape
    row = lambda v: v.reshape(1, -1)

    def bias_rows(length):
        b = jnp.tile(b_s[0][:, :length], (1, CHUNK // length))
        return jnp.repeat(b.T, HEAD_DIM, axis=1)

    def mix_weights(length):
        if length == CHUNK:
            return w_s[0]
        sel = (jnp.arange(CHUNK)[:, None] % length == jnp.arange(length)[None, :]).astype(F32)
        return jnp.einsum('ti,hij,sj->hts', sel, w_s[0][:, :length, :length], sel,
                          precision=lax.Precision.HIGHEST)

    mixer_common = (row(g_mix[0]), _panels(w_in[0]), _conv_coef(w_dw_a[0]), row(b_dw_a[0]),
                    row(g_ln_a[0]), row(b_ln_a[0]), row(g_ln_v[0]), row(b_ln_v[0]))
    w_out_b = _panels(w_out[0])
    ffn_params = (row(g_ffn[0]), w_up[0].astype(BF16), w_dw_f[0], row(b_dw_f[0]),
                  _panels(w_down[0]))
    ple_params = (row(g_ple[0]), _panels(w_ple_gate[0]), _panels(w_ple_proj[0]), row(g_final))

    xp = x_prompt.reshape(batch * seq, D_MODEL)
    lp = min(seq, CHUNK)
    xp, conv_p, cv_p = _mixer(xp, None, seq,
                              mixer_common + (mix_weights(lp), bias_rows(lp), w_out_b), tm=256)
    xp, lastv_p, lastg_p = _ffn(xp, None, seq, ffn_params, tm=1024, tf=512, rc=256)
    yp = _ple(xp, p_prompt[0].reshape(batch * seq, D_PLE), ple_params, 1024, "ple_prompt")

    xs = x_sample.reshape(dec_batch * dec_seq, D_MODEL)
    ls = min(dec_seq, CHUNK)
    state_pad = jnp.pad(state_conv_a[0], ((0, 0), (HIST_A - (CONV_A - 1), 0), (0, 0)))
    xs, a_s, cv_s = _mixer(xs, state_pad, dec_seq,
                           mixer_common + (mix_weights(ls), bias_rows(ls), w_out_b), tm=256)
    xs, upv_s, upg_s = _ffn(xs, state_ffn_conv[0], dec_seq, ffn_params, tm=1024, tf=512, rc=256)
    ys = _ple(xs, p_sample[0].reshape(dec_batch * dec_seq, D_PLE), ple_params, 1024, "ple_sample")

    keep = FFN_CONV - 1
    conv_a_prompt = conv_p[:, HIST_A - (CONV_A - 1):]
    conv_a_sample = jnp.concatenate(
        [state_conv_a[0][:, dec_seq:], a_s.reshape(dec_batch, dec_seq, C_A)], axis=1)
    per_seq = lastv_p.shape[0] // batch
    ffn_prompt = jnp.concatenate([lastv_p, lastg_p], axis=-1)[per_seq - 1::per_seq, SUBLANES - keep:]
    ffn_sample = jnp.concatenate([upv_s, upg_s], axis=-1)
    return (yp.reshape(batch, seq, D_MODEL), ys.reshape(dec_batch, dec_seq, D_MODEL),
            conv_a_prompt[None], conv_a_sample[None], ffn_prompt[None], ffn_sample[None],
            cv_p[None], cv_s.reshape(dec_batch, dec_seq, C_B)[None])
```

```python
import functools

import jax
import jax.numpy as jnp
from jax import lax
from jax.experimental import pallas as pl
from jax.experimental.pallas import tpu as pltpu

D_MODEL = 2048
HEAD_DIM = 128
C_A = D_MODEL // 2
C_B = D_MODEL - C_A
N_HEADS = C_B // HEAD_DIM
CONV_A = 31
HIST_A = 32
CONV_TILES = HIST_A // 8 + 1
CHUNK = 128
FFN_CONV = 3
D_FF = 5632
D_PLE = 256
EPS = 1e-6

SUBLANES = 8
ROWS = 32
LANES_EW = 128
DOT_N = 512
MIXER_SLOTS = 2
PLE_SKEW = 2
PLE_SLOTS = 3
VMEM_LIMIT = 56 * 1024 * 1024

F32 = jnp.float32
BF16 = jnp.bfloat16


def _resident(shape):
    return pl.BlockSpec(shape, lambda *_: (0,) * len(shape), pipeline_mode=pl.Buffered(1))


def _panel_specs(k, n, index=None, resident=True):
    specs = []
    for g in range(n // DOT_N):
        imap = index(g) if index is not None else (lambda *_, g=g: (0, g))
        specs.append(pl.BlockSpec((k, DOT_N), imap,
                                  pipeline_mode=pl.Buffered(1) if resident else None))
    return specs


def _aligned(x, m):
    return x if isinstance(x, int) else pl.multiple_of(x, m)


def _row_loop(n_rows, step, fn):
    def body(c, carry):
        fn(pl.multiple_of(c * step, step))
        return carry
    lax.fori_loop(0, n_rows // step, body, 0)


def _rms_to(x_ref, g_ref, dst_ref, n_rows):
    def piece(r):
        xv = x_ref[pl.ds(r, ROWS), :]
        ms = jnp.mean(xv * xv, axis=-1, keepdims=True)
        dst_ref[pl.ds(r, ROWS), :] = (xv * lax.rsqrt(ms + EPS) * g_ref[...]).astype(BF16)
    _row_loop(n_rows, ROWS, piece)


def _head_ln(x, g, b):
    outs = []
    for h in range(x.shape[-1] // HEAD_DIM):
        xh = x[:, h * HEAD_DIM:(h + 1) * HEAD_DIM]
        mu = jnp.mean(xh, axis=-1, keepdims=True)
        xc = xh - mu
        var = jnp.mean(xc * xc, axis=-1, keepdims=True)
        outs.append(xc * lax.rsqrt(var + EPS))
    return jnp.concatenate(outs, axis=-1) * g + b


def _conv31(win3, n_out, coef_ref, lanes):
    acc = None
    for r in range(SUBLANES):
        inner = None
        for q in range(CONV_TILES):
            if q == 0 and r < HIST_A - (CONV_A - 1):
                continue
            k = (r * CONV_TILES + q) * SUBLANES
            term = coef_ref[k:k + SUBLANES, lanes] * win3[q:q + n_out]
            inner = term if inner is None else inner + term
        if r:
            inner = pltpu.roll(inner, SUBLANES - r, 1)
        acc = inner if acc is None else acc + inner
    return acc


def _mixer_body(sample, tm, seq_len, *refs):
    refs = list(refs)
    x_ref = refs.pop(0)
    st_ref = refs.pop(0) if sample else None
    gmix_ref = refs.pop(0)
    win_ref = [refs.pop(0) for _ in range((2 * C_A + 2 * C_B) // DOT_N)]
    (coef_ref, bdw_ref, glna_ref, blna_ref, glnv_ref, blnv_ref, ws_ref,
     bsb_ref) = [refs.pop(0) for _ in range(8)]
    wout_ref = [refs.pop(0) for _ in range(D_MODEL // DOT_N)]
    y_ref, aout_ref, vout_ref, h_ref, z_ref, cat_ref, a_ref, vn_ref, wm_ref = refs
    n_chunks = tm // CHUNK
    lane_groups = [slice(g * LANES_EW, (g + 1) * LANES_EW) for g in range(C_A // LANES_EW)]

    _rms_to(x_ref, gmix_ref, h_ref, tm)

    ti = lax.broadcasted_iota(jnp.int32, (CHUNK, CHUNK), 0)
    si = lax.broadcasted_iota(jnp.int32, (CHUNK, CHUNK), 1)
    if seq_len >= CHUNK:
        mask = si <= ti
    else:
        mask = ((si // seq_len) == (ti // seq_len)) & ((si % seq_len) <= (ti % seq_len))
    for h in range(N_HEADS):
        wm_ref[h] = jnp.where(mask, ws_ref[h], 0.0).astype(BF16)

    if not sample:
        t = pl.program_id(1)

        @pl.when(t == 0)
        def _():
            a_ref[0:HIST_A, :] = jnp.zeros((HIST_A, C_A), F32)

        @pl.when(t > 0)
        def _():
            a_ref[0:HIST_A, :] = a_ref[tm:tm + HIST_A, :]

    a_base = 0 if sample else HIST_A

    def slot_rows(c):
        lo = (c % MIXER_SLOTS) * CHUNK
        return slice(lo, lo + CHUNK)

    def in_proj(c):
        rows = slice(c * CHUNK, (c + 1) * CHUNK)
        hv = h_ref[rows, :]
        for g in range(len(win_ref)):
            z_ref[slot_rows(c), g * DOT_N:(g + 1) * DOT_N] = jnp.dot(
                hv, win_ref[g][...], preferred_element_type=F32)

    def out_proj(c):
        rows = slice(c * CHUNK, (c + 1) * CHUNK)
        cv = cat_ref[slot_rows(c), :]
        for g in range(len(wout_ref)):
            cols = slice(g * DOT_N, (g + 1) * DOT_N)
            y_ref[rows, cols] = x_ref[rows, cols] + jnp.dot(cv, wout_ref[g][...],
                                                            preferred_element_type=F32)

    def elementwise(c):
        r0 = c * CHUNK
        s0 = slot_rows(c).start - r0
        for r in range(r0, r0 + CHUNK, ROWS):
            for lanes in lane_groups:
                zv = z_ref[s0 + r:s0 + r + ROWS, lanes]
                zg = z_ref[s0 + r:s0 + r + ROWS, C_A + lanes.start:C_A + lanes.stop]
                a = zv * jax.nn.sigmoid(zg)
                a_ref[a_base + r:a_base + r + ROWS, lanes] = a
                if sample:
                    aout_ref[r:r + ROWS, lanes] = a
        for r in range(r0, r0 + CHUNK, ROWS):
            for lanes in lane_groups:
                width = lanes.stop - lanes.start
                if sample:
                    outs = []
                    for rs in range(r, r + ROWS, seq_len):
                        hist = st_ref[rs // seq_len, :, lanes].reshape(
                            CONV_TILES - 1, SUBLANES, width)
                        new = a_ref[rs:rs + seq_len, lanes].reshape(1, SUBLANES, width)
                        win3 = jnp.concatenate([hist, new], axis=0)
                        outs.append(_conv31(win3, 1, coef_ref, lanes))
                    conv = jnp.concatenate(outs, axis=0).reshape(ROWS, width)
                else:
                    n_out = ROWS // SUBLANES
                    win3 = a_ref[r:r + ROWS + HIST_A, lanes].reshape(
                        n_out + CONV_TILES - 1, SUBLANES, width)
                    conv = _conv31(win3, n_out, coef_ref, lanes).reshape(ROWS, width)
                conv = conv + bdw_ref[:, lanes]
                yv = _head_ln(conv, glna_ref[:, lanes], blna_ref[:, lanes])
                cat_ref[s0 + r:s0 + r + ROWS, lanes] = (yv * jax.nn.sigmoid(yv)).astype(BF16)

        v_off = 2 * C_A + C_B
        for r in range(r0, r0 + CHUNK, ROWS):
            for lanes in lane_groups:
                v = jax.nn.gelu(z_ref[s0 + r:s0 + r + ROWS, v_off + lanes.start:v_off + lanes.stop])
                vn = _head_ln(v, glnv_ref[:, lanes], blnv_ref[:, lanes])
                vn_ref[r - r0:r - r0 + ROWS, lanes] = vn.astype(BF16)
                if sample:
                    vout_ref[r:r + ROWS, lanes] = vn
                else:
                    vout_ref[0, r - r0:r - r0 + ROWS, lanes] = vn
        for h in range(N_HEADS):
            lo = h * HEAD_DIM
            mixed = jnp.dot(wm_ref[h], vn_ref[:, lo:lo + HEAD_DIM], preferred_element_type=F32)
            mixed = mixed + bsb_ref[:, lo:lo + HEAD_DIM]
            u = jax.nn.gelu(z_ref[slot_rows(c), 2 * C_A + lo:2 * C_A + lo + HEAD_DIM])
            cat_ref[slot_rows(c), C_A + lo:C_A + lo + HEAD_DIM] = (u * mixed).astype(BF16)

    in_proj(0)
    for c in range(n_chunks):
        if c + 1 < n_chunks:
            in_proj(c + 1)
        elementwise(c)
        out_proj(c)

    if not sample:
        aout_ref[0] = a_ref[tm:tm + HIST_A, :]


def _mixer(x, state, seq_len, params, tm):
    (g_mix, w_in, w_dw, b_dw, g_ln_a, b_ln_a, g_ln_v, b_ln_v, w_s, bias_rows, w_out) = params
    rows = x.shape[0]
    sample = state is not None
    vec = lambda n: _resident((1, n))
    common_in = ([vec(D_MODEL)] + _panel_specs(D_MODEL, 2 * C_A + 2 * C_B)
                 + [_resident((SUBLANES * CONV_TILES * SUBLANES, C_A)),
                    vec(C_A), vec(C_A), vec(C_A), vec(C_B), vec(C_B),
                    _resident((N_HEADS, CHUNK, CHUNK)), _resident((CHUNK, C_B))]
                 + _panel_specs(C_A + C_B, D_MODEL))
    ring = MIXER_SLOTS * CHUNK
    common_scratch = [pltpu.VMEM((tm, D_MODEL), BF16),
                      pltpu.VMEM((ring, 2 * C_A + 2 * C_B), F32),
                      pltpu.VMEM((ring, C_A + C_B), BF16)]
    tail_scratch = [pltpu.VMEM((CHUNK, C_B), BF16),
                    pltpu.VMEM((N_HEADS, CHUNK, CHUNK), BF16)]
    if sample:
        n_seq = rows // seq_len
        spt = tm // seq_len
        grid = (rows // tm,)
        row_map = lambda i: (i, 0)
        in_specs = [pl.BlockSpec((tm, D_MODEL), row_map),
                    pl.BlockSpec((spt, HIST_A, C_A), lambda i: (i, 0, 0))] + common_in
        out_specs = [pl.BlockSpec((tm, D_MODEL), row_map),
                     pl.BlockSpec((tm, C_A), row_map),
                     pl.BlockSpec((tm, C_B), row_map)]
        out_shape = [jax.ShapeDtypeStruct((rows, D_MODEL), F32),
                     jax.ShapeDtypeStruct((rows, C_A), F32),
                     jax.ShapeDtypeStruct((rows, C_B), F32)]
        scratch = common_scratch + [pltpu.VMEM((tm, C_A), F32)] + tail_scratch
        args = (x, state)
        sem = ("arbitrary",)
        del n_seq
    else:
        n_seq = rows // seq_len
        nt = seq_len // tm
        grid = (n_seq, nt)
        row_map = lambda b, t: (b * nt + t, 0)
        in_specs = [pl.BlockSpec((tm, D_MODEL), row_map)] + common_in
        out_specs = [pl.BlockSpec((tm, D_MODEL), row_map),
                     pl.BlockSpec((1, HIST_A, C_A), lambda b, t: (b, 0, 0)),
                     pl.BlockSpec((1, CHUNK, C_B), lambda b, t: (b, 0, 0))]
        out_shape = [jax.ShapeDtypeStruct((rows, D_MODEL), F32),
                     jax.ShapeDtypeStruct((n_seq, HIST_A, C_A), F32),
                     jax.ShapeDtypeStruct((n_seq, CHUNK, C_B), F32)]
        scratch = common_scratch + [pltpu.VMEM((tm + HIST_A, C_A), F32)] + tail_scratch
        args = (x,)
        sem = ("arbitrary", "arbitrary")
    return pl.pallas_call(
        functools.partial(_mixer_body, sample, tm, seq_len),
        grid=grid, in_specs=in_specs, out_specs=out_specs, out_shape=out_shape,
        scratch_shapes=scratch,
        compiler_params=pltpu.CompilerParams(dimension_semantics=sem,
                                             vmem_limit_bytes=VMEM_LIMIT),
        name="mixer_sample" if sample else "mixer_prompt",
    )(*args, g_mix, *[w_in] * (w_in.shape[1] // DOT_N), w_dw, b_dw, g_ln_a, b_ln_a, g_ln_v,
      b_ln_v, w_s, bias_rows, *[w_out] * (w_out.shape[1] // DOT_N))


def _ffn_body(sample, tm, tf, rc, blocks_per_seq, *refs):
    refs = list(refs)
    x_ref = refs.pop(0)
    stv_ref, stg_ref = (refs.pop(0), refs.pop(0)) if sample else (None, None)
    (g_ref, wuv_ref, wug_ref, wdv_ref, wdg_ref, bdv_ref,
     bdg_ref) = [refs.pop(0) for _ in range(7)]
    wdn_ref = [refs.pop(0) for _ in range(D_MODEL // DOT_N)]
    y_ref, lastv_ref, lastg_ref, h_ref, upv_ref, upg_ref, gate_ref = refs[:7]
    carry_ref = None if sample else refs[7]
    i = pl.program_id(0)
    j = pl.program_id(1)

    @pl.when(j == 0)
    def _():
        _rms_to(x_ref, g_ref, h_ref, tm)
        y_ref[...] = x_ref[...]
        if not sample:
            @pl.when(i == 0)
            def _():
                carry_ref[...] = jnp.zeros(carry_ref.shape, F32)

    hist = SUBLANES
    starts = list(range(0, tm - rc, rc)) + [tm - rc, tm - rc // 2]
    chunks = [(r0, r1 - r0) for r0, r1 in zip(starts, starts[1:] + [tm])]
    ups = ((upv_ref, wuv_ref, lastv_ref), (upg_ref, wug_ref, lastg_ref))

    if not sample:
        first = (i % blocks_per_seq) == 0
        for half, (up_ref, _, _) in enumerate(ups):
            up_ref[0:hist, :] = jnp.where(first, 0.0, carry_ref[j, half])

    def up_proj(c):
        r0, n = chunks[c]
        hc = h_ref[r0:r0 + n, :]
        for up_ref, wu_ref, _ in ups:
            up_ref[hist + r0:hist + r0 + n, :] = jnp.dot(hc, wu_ref[...],
                                                         preferred_element_type=F32)

    def conv(up_ref, st_ref, wd_ref, bd_ref, row):
        w0 = wd_ref[0:1, :]
        w1 = wd_ref[1:2, :]
        w2 = wd_ref[2:3, :]
        if sample:
            n = ROWS // SUBLANES
            s0 = _aligned(row // SUBLANES, n)
            cur3 = up_ref[pl.ds(hist + row, ROWS), :].reshape(n, SUBLANES, tf)
            sl = lax.broadcasted_iota(jnp.int32, (n, SUBLANES, tf), 1)
            st0 = st_ref[pl.ds(s0, n), 0:1, :]
            st1 = st_ref[pl.ds(s0, n), 1:2, :]
            m1 = jnp.where(sl == 0, st1, pltpu.roll(cur3, 1, 1))
            m2 = jnp.where(sl == 0, st0, jnp.where(sl == 1, st1, pltpu.roll(cur3, 2, 1)))
            out = w0 * m2 + w1 * m1 + w2 * cur3 + bd_ref[...]
            return out.reshape(ROWS, tf)
        n = ROWS // SUBLANES
        win3 = up_ref[pl.ds(row, ROWS + hist), :].reshape(n + 1, SUBLANES, tf)
        top = lax.broadcasted_iota(jnp.int32, (n, SUBLANES, tf), 1) == 0
        r0 = pltpu.roll(w0 * win3, 1, 1)
        s0 = jnp.concatenate([r0[0:1], jnp.where(top, r0[0:n], r0[1:n + 1])], axis=0)
        r1 = pltpu.roll(w1 * win3 + s0, 1, 1)
        s1 = jnp.where(top, r1[0:n], r1[1:n + 1])
        return (w2 * win3[1:n + 1] + s1 + bd_ref[...]).reshape(ROWS, tf)

    def gate_down(c):
        r0, n = chunks[c]
        for r in range(0, n, ROWS):
            cv = conv(upv_ref, stv_ref if sample else None, wdv_ref, bdv_ref, r0 + r)
            cg = conv(upg_ref, stg_ref if sample else None, wdg_ref, bdg_ref, r0 + r)
            gate_ref[r:r + ROWS, :] = (cv * jax.nn.sigmoid(cv) * cg).astype(BF16)
        gv = gate_ref[0:n, :]
        for g in range(len(wdn_ref)):
            cols = slice(g * DOT_N, (g + 1) * DOT_N)
            y_ref[r0:r0 + n, cols] += jnp.dot(gv, wdn_ref[g][...], preferred_element_type=F32)

    up_proj(0)
    for c in range(len(chunks) - 1):
        up_proj(c + 1)
        gate_down(c)
    gate_down(len(chunks) - 1)

    for half, (up_ref, _, last_ref) in enumerate(ups):
        if sample:
            up3 = up_ref[hist:hist + tm, :].reshape(tm // SUBLANES, SUBLANES, tf)
            last_ref[...] = up3[:, SUBLANES - (FFN_CONV - 1):, :]
        else:
            tail = up_ref[tm:tm + hist, :]
            carry_ref[j, half] = tail
            last_ref[0] = tail


def _ffn(x, state, seq_len, params, tm, tf, rc):
    g_ffn, w_up, w_dw, b_dw, w_down = params
    rows = x.shape[0]
    sample = state is not None
    nj = D_FF // tf
    grid = (rows // tm, nj)
    row_map = lambda i, j: (i, 0)
    val_map = lambda i, j: (0, j)
    gate_map = lambda i, j: (0, nj + j)
    in_specs = [pl.BlockSpec((tm, D_MODEL), row_map)]
    args = [x]
    if sample:
        spt = tm // seq_len
        in_specs += [pl.BlockSpec((spt, FFN_CONV - 1, tf), lambda i, j: (i, 0, j)),
                     pl.BlockSpec((spt, FFN_CONV - 1, tf), lambda i, j: (i, 0, nj + j))]
        args += [state, state]
    in_specs += [pl.BlockSpec((1, D_MODEL), lambda i, j: (0, 0)),
                 pl.BlockSpec((D_MODEL, tf), val_map), pl.BlockSpec((D_MODEL, tf), gate_map),
                 pl.BlockSpec((FFN_CONV, tf), val_map), pl.BlockSpec((FFN_CONV, tf), gate_map),
                 pl.BlockSpec((1, tf), val_map), pl.BlockSpec((1, tf), gate_map)]
    in_specs += _panel_specs(tf, D_MODEL, index=lambda g: (lambda i, j: (j, g)), resident=False)
    args += [g_ffn, w_up, w_up, w_dw, w_dw, b_dw, b_dw] + [w_down] * (D_MODEL // DOT_N)
    scratch = [pltpu.VMEM((tm, D_MODEL), BF16),
               pltpu.VMEM((tm + SUBLANES, tf), F32), pltpu.VMEM((tm + SUBLANES, tf), F32),
               pltpu.VMEM((rc, tf), BF16)]
    if sample:
        last_specs = [pl.BlockSpec((spt, FFN_CONV - 1, tf), lambda i, j: (i, 0, j))] * 2
        last_shape = [jax.ShapeDtypeStruct((rows // seq_len, FFN_CONV - 1, D_FF), F32)] * 2
        blocks_per_seq = 1
    else:
        blocks_per_seq = seq_len // tm
        last_specs = [pl.BlockSpec((1, SUBLANES, tf), lambda i, j: (i, 0, j))] * 2
        last_shape = [jax.ShapeDtypeStruct((rows // tm, SUBLANES, D_FF), F32)] * 2
        scratch += [pltpu.VMEM((nj, 2, SUBLANES, tf), F32)]
    return pl.pallas_call(
        functools.partial(_ffn_body, sample, tm, tf, rc, blocks_per_seq),
        grid=grid, in_specs=in_specs,
        out_specs=[pl.BlockSpec((tm, D_MODEL), row_map)] + last_specs,
        out_shape=[jax.ShapeDtypeStruct((rows, D_MODEL), F32)] + last_shape,
        scratch_shapes=scratch,
        compiler_params=pltpu.CompilerParams(dimension_semantics=("arbitrary", "arbitrary"),
                                             vmem_limit_bytes=VMEM_LIMIT),
        name="ffn_sample" if sample else "ffn_prompt",
    )(*args)


def _ple_body(tm, *refs):
    refs = list(refs)
    x_ref, p_ref, g_ref = refs[:3]
    n_panels = D_MODEL // DOT_N
    wg_ref = refs[3:3 + n_panels]
    wp_ref = refs[3 + n_panels:3 + 2 * n_panels]
    gf_ref, y_ref, h_ref, gate_ref, proj_ref = refs[3 + 2 * n_panels:]
    _ple_pipeline(tm, x_ref, p_ref, g_ref, wg_ref, wp_ref, gf_ref, y_ref, h_ref, gate_ref, proj_ref)


def _ple_pipeline(tm, x_ref, p_ref, g_ref, wg_ref, wp_ref, gf_ref, y_ref, h_ref, gate_ref, proj_ref):
    n_chunks = tm // CHUNK
    never = pl.program_id(0) < 0
    pieces = CHUNK // ROWS
    anchors = {}

    def norm(c):
        slot = (c % PLE_SLOTS) * CHUNK
        for k in range(pieces):
            r = c * CHUNK + k * ROWS
            xv = x_ref[r:r + ROWS, :]
            ms = jnp.mean(xv * xv, axis=-1, keepdims=True)
            hv = xv * lax.rsqrt(ms + EPS) * g_ref[...]
            anchor = anchors.pop((c - PLE_SKEW, k), None)
            if anchor is not None:
                head = jnp.where(never, anchor, hv[:, :HEAD_DIM])
                hv = jnp.concatenate([head, hv[:, HEAD_DIM:]], axis=-1)
            h_ref[slot + k * ROWS:slot + (k + 1) * ROWS, :] = hv.astype(BF16)

    def projections(c):
        slot = (c % PLE_SLOTS) * CHUNK
        rows = slice(c * CHUNK, (c + 1) * CHUNK)
        pv = p_ref[rows, :].astype(BF16)
        for g in range(D_MODEL // DOT_N):
            cols = slice(g * DOT_N, (g + 1) * DOT_N)
            gate_ref[slot:slot + CHUNK, cols] = jnp.dot(
                h_ref[slot:slot + CHUNK, :], wg_ref[g][...], preferred_element_type=F32)
            proj_ref[slot:slot + CHUNK, cols] = jnp.dot(pv, wp_ref[g][...],
                                                        preferred_element_type=F32)

    def finish(c):
        slot = (c % PLE_SLOTS) * CHUNK
        for k in range(pieces):
            r = c * CHUNK + k * ROWS
            s = slot + k * ROWS
            xv = x_ref[r:r + ROWS, :] + jax.nn.sigmoid(gate_ref[s:s + ROWS, :]) * proj_ref[s:s + ROWS, :]
            ms = jnp.mean(xv * xv, axis=-1, keepdims=True)
            yv = xv * lax.rsqrt(ms + EPS) * gf_ref[...]
            y_ref[r:r + ROWS, :] = yv
            anchors[(c, k)] = yv[:, :HEAD_DIM]

    norm(0)
    for c in range(n_chunks):
        projections(c)
        if c >= 1:
            finish(c - 1)
        if c + 1 < n_chunks:
            norm(c + 1)
    finish(n_chunks - 1)


def _ple(x, p, params, tm, name):
    g_ple, w_gate, w_proj, g_final = params
    rows = x.shape[0]
    row_map = lambda i: (i, 0)
    return pl.pallas_call(
        functools.partial(_ple_body, tm),
        grid=(rows // tm,),
        in_specs=([pl.BlockSpec((tm, D_MODEL), row_map), pl.BlockSpec((tm, D_PLE), row_map),
                   _resident((1, D_MODEL))] + _panel_specs(D_MODEL, D_MODEL)
                  + _panel_specs(D_PLE, D_MODEL) + [_resident((1, D_MODEL))]),
        out_specs=pl.BlockSpec((tm, D_MODEL), row_map),
        out_shape=jax.ShapeDtypeStruct((rows, D_MODEL), F32),
        scratch_shapes=[pltpu.VMEM((PLE_SLOTS * CHUNK, D_MODEL), BF16),
                        pltpu.VMEM((PLE_SLOTS * CHUNK, D_MODEL), F32),
                        pltpu.VMEM((PLE_SLOTS * CHUNK, D_MODEL), F32)],
        compiler_params=pltpu.CompilerParams(dimension_semantics=("arbitrary",),
                                             vmem_limit_bytes=VMEM_LIMIT),
        name=name,
    )(x, p, g_ple, *[w_gate] * (D_MODEL // DOT_N), *[w_proj] * (D_MODEL // DOT_N), g_final)


def _conv_coef(w):
    c = w.shape[1]
    off = HIST_A - (CONV_A - 1)
    n = SUBLANES * (CONV_TILES + 1)
    wp = jnp.zeros((n, c), w.dtype).at[off:off + CONV_A].set(w)
    wp_prev = jnp.concatenate([jnp.zeros((SUBLANES, c), w.dtype), wp[:n - SUBLANES]], axis=0)
    r = jnp.arange(SUBLANES)[:, None]
    q = jnp.arange(CONV_TILES)[None, :]
    d = SUBLANES * q + r
    s = jnp.arange(SUBLANES)[None, None, :, None]
    coef = jnp.where(s >= r[:, :, None, None], wp[d][:, :, None, :], wp_prev[d][:, :, None, :])
    return coef.reshape(SUBLANES * CONV_TILES * SUBLANES, c)


def kernel(x_prompt, x_sample, p_prompt, p_sample, state_conv_a, state_ffn_conv, g_mix, w_in, w_dw_a, b_dw_a, g_ln_a, b_ln_a, g_ln_v, b_ln_v, w_s, b_s, w_out, g_ffn, w_up, w_dw_f, b_dw_f, w_down, g_ple, w_ple_gate, w_ple_proj, g_final):
    depth = w_in.shape[0]
    assert depth == 1, "single-layer step"
    batch, seq, _ = x_prompt.shape
    dec_batch, dec_seq, _ = x_sample.shape
    row = lambda v: v.reshape(1, -1)

    def bias_rows(length):
        b = jnp.tile(b_s[0][:, :length], (1, CHUNK // length))
        return jnp.repeat(b.T, HEAD_DIM, axis=1)

    def mix_weights(length):
        if length == CHUNK:
            return w_s[0]
        sel = (jnp.arange(CHUNK)[:, None] % length == jnp.arange(length)[None, :]).astype(F32)
        return jnp.einsum('ti,hij,sj->hts', sel, w_s[0][:, :length, :length], sel,
                          precision=lax.Precision.HIGHEST)

    mixer_common = (row(g_mix[0]), w_in[0].astype(BF16), _conv_coef(w_dw_a[0]), row(b_dw_a[0]),
                    row(g_ln_a[0]), row(b_ln_a[0]), row(g_ln_v[0]), row(b_ln_v[0]))
    w_out_b = w_out[0].astype(BF16)
    ffn_params = (row(g_ffn[0]), w_up[0].astype(BF16), w_dw_f[0], row(b_dw_f[0]),
                  w_down[0].astype(BF16))
    ple_params = (row(g_ple[0]), w_ple_gate[0].astype(BF16), w_ple_proj[0].astype(BF16),
                  row(g_final))

    xp = x_prompt.reshape(batch * seq, D_MODEL)
    lp = min(seq, CHUNK)
    xp, conv_p, cv_p = _mixer(xp, None, seq,
                              mixer_common + (mix_weights(lp), bias_rows(lp), w_out_b), tm=256)
    xp, lastv_p, lastg_p = _ffn(xp, None, seq, ffn_params, tm=1024, tf=512, rc=256)
    yp = _ple(xp, p_prompt[0].reshape(batch * seq, D_PLE), ple_params, 1024, "ple_prompt")

    xs = x_sample.reshape(dec_batch * dec_seq, D_MODEL)
    ls = min(dec_seq, CHUNK)
    state_pad = jnp.pad(state_conv_a[0], ((0, 0), (HIST_A - (CONV_A - 1), 0), (0, 0)))
    xs, a_s, cv_s = _mixer(xs, state_pad, dec_seq,
                           mixer_common + (mix_weights(ls), bias_rows(ls), w_out_b), tm=256)
    xs, upv_s, upg_s = _ffn(xs, state_ffn_conv[0], dec_seq, ffn_params, tm=1024, tf=512, rc=256)
    ys = _ple(xs, p_sample[0].reshape(dec_batch * dec_seq, D_PLE), ple_params, 1024, "ple_sample")

    keep = FFN_CONV - 1
    conv_a_prompt = conv_p[:, HIST_A - (CONV_A - 1):]
    conv_a_sample = jnp.concatenate(
        [state_conv_a[0][:, dec_seq:], a_s.reshape(dec_batch, dec_seq, C_A)], axis=1)
    per_seq = lastv_p.shape[0] // batch
    ffn_prompt = jnp.concatenate([lastv_p, lastg_p], axis=-1)[per_seq - 1::per_seq, SUBLANES - keep:]
    ffn_sample = jnp.concatenate([upv_s, upg_s], axis=-1)
    return (yp.reshape(batch, seq, D_MODEL), ys.reshape(dec_batch, dec_seq, D_MODEL),
            conv_a_prompt[None], conv_a_sample[None], ffn_prompt[None], ffn_sample[None],
            cv_p[None], cv_s.reshape(dec_batch, dec_seq, C_B)[None])
```

```python
import functools

import jax
import jax.numpy as jnp
from jax import lax
from jax.experimental import pallas as pl
from jax.experimental.pallas import tpu as pltpu

D_MODEL = 2048
HEAD_DIM = 128
C_A = D_MODEL // 2
C_B = D_MODEL - C_A
N_HEADS = C_B // HEAD_DIM
CONV_A = 31
HIST_A = 32
CONV_TILES = HIST_A // 8 + 1
CHUNK = 128
FFN_CONV = 3
D_FF = 5632
D_PLE = 256
EPS = 1e-6

SUBLANES = 8
ROWS = 32
LANES_EW = 128
FFN_K_SPLIT = 2
DOT_N = 512
MIXER_SLOTS = 2
PLE_SKEW = 2
PLE_SLOTS = 3
VMEM_LIMIT = 56 * 1024 * 1024

F32 = jnp.float32
BF16 = jnp.bfloat16


def _resident(shape):
    return pl.BlockSpec(shape, lambda *_: (0,) * len(shape), pipeline_mode=pl.Buffered(1))


def _panel_specs(k, n, index=None, resident=True):
    specs = []
    for g in range(n // DOT_N):
        imap = index(g) if index is not None else (lambda *_, g=g: (0, g))
        specs.append(pl.BlockSpec((k, DOT_N), imap,
                                  pipeline_mode=pl.Buffered(1) if resident else None))
    return specs


def _aligned(x, m):
    return x if isinstance(x, int) else pl.multiple_of(x, m)


def _row_loop(n_rows, step, fn):
    def body(c, carry):
        fn(pl.multiple_of(c * step, step))
        return carry
    lax.fori_loop(0, n_rows // step, body, 0)


def _rms_to(x_ref, g_ref, dst_ref, n_rows):
    def piece(r):
        xv = x_ref[pl.ds(r, ROWS), :]
        ms = jnp.mean(xv * xv, axis=-1, keepdims=True)
        dst_ref[pl.ds(r, ROWS), :] = (xv * lax.rsqrt(ms + EPS) * g_ref[...]).astype(BF16)
    _row_loop(n_rows, ROWS, piece)


def _head_ln(x, g, b):
    outs = []
    for h in range(x.shape[-1] // HEAD_DIM):
        xh = x[:, h * HEAD_DIM:(h + 1) * HEAD_DIM]
        mu = jnp.mean(xh, axis=-1, keepdims=True)
        xc = xh - mu
        var = jnp.mean(xc * xc, axis=-1, keepdims=True)
        outs.append(xc * lax.rsqrt(var + EPS))
    return jnp.concatenate(outs, axis=-1) * g + b


def _conv31(win3, n_out, coef_ref, lanes):
    acc = None
    for r in range(SUBLANES):
        inner = None
        for q in range(CONV_TILES):
            if q == 0 and r < HIST_A - (CONV_A - 1):
                continue
            k = (r * CONV_TILES + q) * SUBLANES
            term = coef_ref[k:k + SUBLANES, lanes] * win3[q:q + n_out]
            inner = term if inner is None else inner + term
        if r:
            inner = pltpu.roll(inner, SUBLANES - r, 1)
        acc = inner if acc is None else acc + inner
    return acc


def _mixer_body(sample, tm, seq_len, *refs):
    refs = list(refs)
    x_ref = refs.pop(0)
    st_ref = refs.pop(0) if sample else None
    gmix_ref = refs.pop(0)
    win_ref = [refs.pop(0) for _ in range((2 * C_A + 2 * C_B) // DOT_N)]
    (coef_ref, bdw_ref, glna_ref, blna_ref, glnv_ref, blnv_ref, ws_ref,
     bsb_ref) = [refs.pop(0) for _ in range(8)]
    wout_ref = [refs.pop(0) for _ in range(D_MODEL // DOT_N)]
    y_ref, aout_ref, vout_ref, h_ref, z_ref, cat_ref, a_ref, vn_ref, wm_ref = refs
    n_chunks = tm // CHUNK
    lane_groups = [slice(g * LANES_EW, (g + 1) * LANES_EW) for g in range(C_A // LANES_EW)]

    def norm(c):
        for r in range(c * CHUNK, (c + 1) * CHUNK, ROWS):
            xv = x_ref[r:r + ROWS, :]
            ms = jnp.mean(xv * xv, axis=-1, keepdims=True)
            h_ref[r:r + ROWS, :] = (xv * lax.rsqrt(ms + EPS) * gmix_ref[...]).astype(BF16)

    ti = lax.broadcasted_iota(jnp.int32, (CHUNK, CHUNK), 0)
    si = lax.broadcasted_iota(jnp.int32, (CHUNK, CHUNK), 1)
    if seq_len >= CHUNK:
        mask = si <= ti
    else:
        mask = ((si // seq_len) == (ti // seq_len)) & ((si % seq_len) <= (ti % seq_len))
    for h in range(N_HEADS):
        wm_ref[h] = jnp.where(mask, ws_ref[h], 0.0).astype(BF16)

    if not sample:
        t = pl.program_id(1)

        @pl.when(t == 0)
        def _():
            a_ref[0:HIST_A, :] = jnp.zeros((HIST_A, C_A), F32)

        @pl.when(t > 0)
        def _():
            a_ref[0:HIST_A, :] = a_ref[tm:tm + HIST_A, :]

    a_base = 0 if sample else HIST_A

    def slot_rows(c):
        lo = (c % MIXER_SLOTS) * CHUNK
        return slice(lo, lo + CHUNK)

    def in_proj(c):
        rows = slice(c * CHUNK, (c + 1) * CHUNK)
        hv = h_ref[rows, :]
        for g in range(len(win_ref)):
            z_ref[slot_rows(c), g * DOT_N:(g + 1) * DOT_N] = jnp.dot(
                hv, win_ref[g][...], preferred_element_type=F32)

    def out_proj(c):
        rows = slice(c * CHUNK, (c + 1) * CHUNK)
        cv = cat_ref[slot_rows(c), :]
        for g in range(len(wout_ref)):
            cols = slice(g * DOT_N, (g + 1) * DOT_N)
            y_ref[rows, cols] = x_ref[rows, cols] + jnp.dot(cv, wout_ref[g][...],
                                                            preferred_element_type=F32)

    def elementwise(c):
        r0 = c * CHUNK
        s0 = slot_rows(c).start - r0
        for r in range(r0, r0 + CHUNK, ROWS):
            for lanes in lane_groups:
                zv = z_ref[s0 + r:s0 + r + ROWS, lanes]
                zg = z_ref[s0 + r:s0 + r + ROWS, C_A + lanes.start:C_A + lanes.stop]
                a = zv * jax.nn.sigmoid(zg)
                a_ref[a_base + r:a_base + r + ROWS, lanes] = a
                if sample:
                    aout_ref[r:r + ROWS, lanes] = a
        for r in range(r0, r0 + CHUNK, ROWS):
            for lanes in lane_groups:
                width = lanes.stop - lanes.start
                if sample:
                    outs = []
                    for rs in range(r, r + ROWS, seq_len):
                        hist = st_ref[rs // seq_len, :, lanes].reshape(
                            CONV_TILES - 1, SUBLANES, width)
                        new = a_ref[rs:rs + seq_len, lanes].reshape(1, SUBLANES, width)
                        win3 = jnp.concatenate([hist, new], axis=0)
                        outs.append(_conv31(win3, 1, coef_ref, lanes))
                    conv = jnp.concatenate(outs, axis=0).reshape(ROWS, width)
                else:
                    n_out = ROWS // SUBLANES
                    win3 = a_ref[r:r + ROWS + HIST_A, lanes].reshape(
                        n_out + CONV_TILES - 1, SUBLANES, width)
                    conv = _conv31(win3, n_out, coef_ref, lanes).reshape(ROWS, width)
                conv = conv + bdw_ref[:, lanes]
                yv = _head_ln(conv, glna_ref[:, lanes], blna_ref[:, lanes])
                cat_ref[s0 + r:s0 + r + ROWS, lanes] = (yv * jax.nn.sigmoid(yv)).astype(BF16)

        v_off = 2 * C_A + C_B
        for r in range(r0, r0 + CHUNK, ROWS):
            for lanes in lane_groups:
                v = jax.nn.gelu(z_ref[s0 + r:s0 + r + ROWS, v_off + lanes.start:v_off + lanes.stop])
                vn = _head_ln(v, glnv_ref[:, lanes], blnv_ref[:, lanes])
                vn_ref[r - r0:r - r0 + ROWS, lanes] = vn.astype(BF16)
                if sample:
                    vout_ref[r:r + ROWS, lanes] = vn
                else:
                    vout_ref[0, r - r0:r - r0 + ROWS, lanes] = vn
        for h in range(N_HEADS):
            lo = h * HEAD_DIM
            mixed = jnp.dot(wm_ref[h], vn_ref[:, lo:lo + HEAD_DIM], preferred_element_type=F32)
            mixed = mixed + bsb_ref[:, lo:lo + HEAD_DIM]
            u = jax.nn.gelu(z_ref[slot_rows(c), 2 * C_A + lo:2 * C_A + lo + HEAD_DIM])
            cat_ref[slot_rows(c), C_A + lo:C_A + lo + HEAD_DIM] = (u * mixed).astype(BF16)

    norm(0)
    in_proj(0)
    for c in range(n_chunks):
        if c + 1 < n_chunks:
            norm(c + 1)
            in_proj(c + 1)
        elementwise(c)
        out_proj(c)

    if not sample:
        aout_ref[0] = a_ref[tm:tm + HIST_A, :]


def _mixer(x, state, seq_len, params, tm):
    (g_mix, w_in, w_dw, b_dw, g_ln_a, b_ln_a, g_ln_v, b_ln_v, w_s, bias_rows, w_out) = params
    rows = x.shape[0]
    sample = state is not None
    vec = lambda n: _resident((1, n))
    common_in = ([vec(D_MODEL)] + _panel_specs(D_MODEL, 2 * C_A + 2 * C_B)
                 + [_resident((SUBLANES * CONV_TILES * SUBLANES, C_A)),
                    vec(C_A), vec(C_A), vec(C_A), vec(C_B), vec(C_B),
                    _resident((N_HEADS, CHUNK, CHUNK)), _resident((CHUNK, C_B))]
                 + _panel_specs(C_A + C_B, D_MODEL))
    ring = MIXER_SLOTS * CHUNK
    common_scratch = [pltpu.VMEM((tm, D_MODEL), BF16),
                      pltpu.VMEM((ring, 2 * C_A + 2 * C_B), F32),
                      pltpu.VMEM((ring, C_A + C_B), BF16)]
    tail_scratch = [pltpu.VMEM((CHUNK, C_B), BF16),
                    pltpu.VMEM((N_HEADS, CHUNK, CHUNK), BF16)]
    if sample:
        n_seq = rows // seq_len
        spt = tm // seq_len
        grid = (rows // tm,)
        row_map = lambda i: (i, 0)
        in_specs = [pl.BlockSpec((tm, D_MODEL), row_map),
                    pl.BlockSpec((spt, HIST_A, C_A), lambda i: (i, 0, 0))] + common_in
        out_specs = [pl.BlockSpec((tm, D_MODEL), row_map),
                     pl.BlockSpec((tm, C_A), row_map),
                     pl.BlockSpec((tm, C_B), row_map)]
        out_shape = [jax.ShapeDtypeStruct((rows, D_MODEL), F32),
                     jax.ShapeDtypeStruct((rows, C_A), F32),
                     jax.ShapeDtypeStruct((rows, C_B), F32)]
        scratch = common_scratch + [pltpu.VMEM((tm, C_A), F32)] + tail_scratch
        args = (x, state)
        sem = ("arbitrary",)
        del n_seq
    else:
        n_seq = rows // seq_len
        nt = seq_len // tm
        grid = (n_seq, nt)
        row_map = lambda b, t: (b * nt + t, 0)
        in_specs = [pl.BlockSpec((tm, D_MODEL), row_map)] + common_in
        out_specs = [pl.BlockSpec((tm, D_MODEL), row_map),
                     pl.BlockSpec((1, HIST_A, C_A), lambda b, t: (b, 0, 0)),
                     pl.BlockSpec((1, CHUNK, C_B), lambda b, t: (b, 0, 0))]
        out_shape = [jax.ShapeDtypeStruct((rows, D_MODEL), F32),
                     jax.ShapeDtypeStruct((n_seq, HIST_A, C_A), F32),
                     jax.ShapeDtypeStruct((n_seq, CHUNK, C_B), F32)]
        scratch = common_scratch + [pltpu.VMEM((tm + HIST_A, C_A), F32)] + tail_scratch
        args = (x,)
        sem = ("arbitrary", "arbitrary")
    return pl.pallas_call(
        functools.partial(_mixer_body, sample, tm, seq_len),
        grid=grid, in_specs=in_specs, out_specs=out_specs, out_shape=out_shape,
        scratch_shapes=scratch,
        compiler_params=pltpu.CompilerParams(dimension_semantics=sem,
                                             vmem_limit_bytes=VMEM_LIMIT),
        name="mixer_sample" if sample else "mixer_prompt",
    )(*args, g_mix, *[w_in] * (w_in.shape[1] // DOT_N), w_dw, b_dw, g_ln_a, b_ln_a, g_ln_v,
      b_ln_v, w_s, bias_rows, *[w_out] * (w_out.shape[1] // DOT_N))


def _ffn_body(sample, tm, tf, rc, blocks_per_seq, *refs):
    refs = list(refs)
    x_ref = refs.pop(0)
    stv_ref, stg_ref = (refs.pop(0), refs.pop(0)) if sample else (None, None)
    (g_ref, wuv_ref, wug_ref, wdv_ref, wdg_ref, bdv_ref,
     bdg_ref) = [refs.pop(0) for _ in range(7)]
    wdn_ref = [refs.pop(0) for _ in range(D_MODEL // DOT_N)]
    y_ref, lastv_ref, lastg_ref, h_ref, upv_ref, upg_ref, gate_ref = refs[:7]
    carry_ref = None if sample else refs[7]
    i = pl.program_id(0)
    j = pl.program_id(1)

    @pl.when(j == 0)
    def _():
        _rms_to(x_ref, g_ref, h_ref, tm)
        y_ref[...] = x_ref[...]
        if not sample:
            @pl.when(i == 0)
            def _():
                carry_ref[...] = jnp.zeros(carry_ref.shape, F32)

    hist = SUBLANES
    starts = list(range(0, tm - rc, rc)) + [tm - rc, tm - rc // 2]
    chunks = [(r0, r1 - r0) for r0, r1 in zip(starts, starts[1:] + [tm])]
    ups = ((upv_ref, wuv_ref, lastv_ref), (upg_ref, wug_ref, lastg_ref))

    if not sample:
        first = (i % blocks_per_seq) == 0
        for half, (up_ref, _, _) in enumerate(ups):
            up_ref[0:hist, :] = jnp.where(first, 0.0, carry_ref[j, half])

    def up_proj(c):
        r0, n = chunks[c]
        hc = h_ref[r0:r0 + n, :]
        for up_ref, wu_ref, _ in ups:
            up_ref[hist + r0:hist + r0 + n, :] = jnp.dot(hc, wu_ref[...],
                                                         preferred_element_type=F32)

    def conv(up_ref, st_ref, wd_ref, bd_ref, row, lanes):
        width = lanes.stop - lanes.start
        w0 = wd_ref[0:1, lanes]
        w1 = wd_ref[1:2, lanes]
        w2 = wd_ref[2:3, lanes]
        n = ROWS // SUBLANES
        if sample:
            s0 = row // SUBLANES
            cur3 = up_ref[hist + row:hist + row + ROWS, lanes].reshape(n, SUBLANES, width)
            sl = lax.broadcasted_iota(jnp.int32, (n, SUBLANES, width), 1)
            st0 = st_ref[s0:s0 + n, 0:1, lanes]
            st1 = st_ref[s0:s0 + n, 1:2, lanes]
            m1 = jnp.where(sl == 0, st1, pltpu.roll(cur3, 1, 1))
            m2 = jnp.where(sl == 0, st0, jnp.where(sl == 1, st1, pltpu.roll(cur3, 2, 1)))
            out = w0 * m2 + w1 * m1 + w2 * cur3 + bd_ref[:, lanes]
            return out.reshape(ROWS, width)
        win3 = up_ref[row:row + ROWS + hist, lanes].reshape(n + 1, SUBLANES, width)
        top = lax.broadcasted_iota(jnp.int32, (n, SUBLANES, width), 1) == 0
        r0 = pltpu.roll(w0 * win3, 1, 1)
        s0 = jnp.concatenate([r0[0:1], jnp.where(top, r0[0:n], r0[1:n + 1])], axis=0)
        r1 = pltpu.roll(w1 * win3 + s0, 1, 1)
        s1 = jnp.where(top, r1[0:n], r1[1:n + 1])
        return (w2 * win3[1:n + 1] + s1 + bd_ref[:, lanes]).reshape(ROWS, width)

    def gate_down(c):
        r0, n = chunks[c]
        kw = tf // FFN_K_SPLIT
        for kh in range(FFN_K_SPLIT):
            lanes = slice(kh * kw, (kh + 1) * kw)
            for r in range(0, n, ROWS):
                cv = conv(upv_ref, stv_ref if sample else None, wdv_ref, bdv_ref, r0 + r, lanes)
                cg = conv(upg_ref, stg_ref if sample else None, wdg_ref, bdg_ref, r0 + r, lanes)
                gate_ref[r:r + ROWS, lanes] = (cv * jax.nn.sigmoid(cv) * cg).astype(BF16)
            gv = gate_ref[0:n, lanes]
            for g in range(len(wdn_ref)):
                cols = slice(g * DOT_N, (g + 1) * DOT_N)
                y_ref[r0:r0 + n, cols] += jnp.dot(gv, wdn_ref[g][lanes, :],
                                                  preferred_element_type=F32)

    up_proj(0)
    for c in range(len(chunks) - 1):
        up_proj(c + 1)
        gate_down(c)
    gate_down(len(chunks) - 1)

    for half, (up_ref, _, last_ref) in enumerate(ups):
        if sample:
            up3 = up_ref[hist:hist + tm, :].reshape(tm // SUBLANES, SUBLANES, tf)
            last_ref[...] = up3[:, SUBLANES - (FFN_CONV - 1):, :]
        else:
            tail = up_ref[tm:tm + hist, :]
            carry_ref[j, half] = tail
            last_ref[0] = tail


def _ffn(x, state, seq_len, params, tm, tf, rc):
    g_ffn, w_up, w_dw, b_dw, w_down = params
    rows = x.shape[0]
    sample = state is not None
    nj = D_FF // tf
    grid = (rows // tm, nj)
    row_map = lambda i, j: (i, 0)
    val_map = lambda i, j: (0, j)
    gate_map = lambda i, j: (0, nj + j)
    in_specs = [pl.BlockSpec((tm, D_MODEL), row_map)]
    args = [x]
    if sample:
        spt = tm // seq_len
        in_specs += [pl.BlockSpec((spt, FFN_CONV - 1, tf), lambda i, j: (i, 0, j)),
                     pl.BlockSpec((spt, FFN_CONV - 1, tf), lambda i, j: (i, 0, nj + j))]
        args += [state, state]
    in_specs += [pl.BlockSpec((1, D_MODEL), lambda i, j: (0, 0)),
                 pl.BlockSpec((D_MODEL, tf), val_map), pl.BlockSpec((D_MODEL, tf), gate_map),
                 pl.BlockSpec((FFN_CONV, tf), val_map), pl.BlockSpec((FFN_CONV, tf), gate_map),
                 pl.BlockSpec((1, tf), val_map), pl.BlockSpec((1, tf), gate_map)]
    in_specs += _panel_specs(tf, D_MODEL, index=lambda g: (lambda i, j: (j, g)), resident=False)
    args += [g_ffn, w_up, w_up, w_dw, w_dw, b_dw, b_dw] + [w_down] * (D_MODEL // DOT_N)
    scratch = [pltpu.VMEM((tm, D_MODEL), BF16),
               pltpu.VMEM((tm + SUBLANES, tf), F32), pltpu.VMEM((tm + SUBLANES, tf), F32),
               pltpu.VMEM((rc, tf), BF16)]
    if sample:
        last_specs = [pl.BlockSpec((spt, FFN_CONV - 1, tf), lambda i, j: (i, 0, j))] * 2
        last_shape = [jax.ShapeDtypeStruct((rows // seq_len, FFN_CONV - 1, D_FF), F32)] * 2
        blocks_per_seq = 1
    else:
        blocks_per_seq = seq_len // tm
        last_specs = [pl.BlockSpec((1, SUBLANES, tf), lambda i, j: (i, 0, j))] * 2
        last_shape = [jax.ShapeDtypeStruct((rows // tm, SUBLANES, D_FF), F32)] * 2
        scratch += [pltpu.VMEM((nj, 2, SUBLANES, tf), F32)]
    return pl.pallas_call(
        functools.partial(_ffn_body, sample, tm, tf, rc, blocks_per_seq),
        grid=grid, in_specs=in_specs,
        out_specs=[pl.BlockSpec((tm, D_MODEL), row_map)] + last_specs,
        out_shape=[jax.ShapeDtypeStruct((rows, D_MODEL), F32)] + last_shape,
        scratch_shapes=scratch,
        compiler_params=pltpu.CompilerParams(dimension_semantics=("arbitrary", "arbitrary"),
                                             vmem_limit_bytes=VMEM_LIMIT),
        name="ffn_sample" if sample else "ffn_prompt",
    )(*args)


def _ple_body(tm, *refs):
    refs = list(refs)
    x_ref, p_ref, g_ref = refs[:3]
    n_panels = D_MODEL // DOT_N
    wg_ref = refs[3:3 + n_panels]
    wp_ref = refs[3 + n_panels:3 + 2 * n_panels]
    gf_ref, y_ref, h_ref, gate_ref, proj_ref = refs[3 + 2 * n_panels:]
    _ple_pipeline(tm, x_ref, p_ref, g_ref, wg_ref, wp_ref, gf_ref, y_ref, h_ref, gate_ref, proj_ref)


def _ple_pipeline(tm, x_ref, p_ref, g_ref, wg_ref, wp_ref, gf_ref, y_ref, h_ref, gate_ref, proj_ref):
    n_chunks = tm // CHUNK
    never = pl.program_id(0) < 0
    pieces = CHUNK // ROWS
    anchors = {}

    def norm(c):
        slot = (c % PLE_SLOTS) * CHUNK
        for k in range(pieces):
            r = c * CHUNK + k * ROWS
            xv = x_ref[r:r + ROWS, :]
            ms = jnp.mean(xv * xv, axis=-1, keepdims=True)
            hv = xv * lax.rsqrt(ms + EPS) * g_ref[...]
            anchor = anchors.pop((c - PLE_SKEW, k), None)
            if anchor is not None:
                head = jnp.where(never, anchor, hv[:, :HEAD_DIM])
                hv = jnp.concatenate([head, hv[:, HEAD_DIM:]], axis=-1)
            h_ref[slot + k * ROWS:slot + (k + 1) * ROWS, :] = hv.astype(BF16)

    def projections(c):
        slot = (c % PLE_SLOTS) * CHUNK
        rows = slice(c * CHUNK, (c + 1) * CHUNK)
        pv = p_ref[rows, :].astype(BF16)
        for g in range(D_MODEL // DOT_N):
            cols = slice(g * DOT_N, (g + 1) * DOT_N)
            gate_ref[slot:slot + CHUNK, cols] = jnp.dot(
                h_ref[slot:slot + CHUNK, :], wg_ref[g][...], preferred_element_type=F32)
            proj_ref[slot:slot + CHUNK, cols] = jnp.dot(pv, wp_ref[g][...],
                                                        preferred_element_type=F32)

    def finish(c):
        slot = (c % PLE_SLOTS) * CHUNK
        for k in range(pieces):
            r = c * CHUNK + k * ROWS
            s = slot + k * ROWS
            xv = x_ref[r:r + ROWS, :] + jax.nn.sigmoid(gate_ref[s:s + ROWS, :]) * proj_ref[s:s + ROWS, :]
            ms = jnp.mean(xv * xv, axis=-1, keepdims=True)
            yv = xv * lax.rsqrt(ms + EPS) * gf_ref[...]
            y_ref[r:r + ROWS, :] = yv
            anchors[(c, k)] = yv[:, :HEAD_DIM]

    norm(0)
    for c in range(n_chunks):
        projections(c)
        if c >= 1:
            finish(c - 1)
        if c + 1 < n_chunks:
            norm(c + 1)
    finish(n_chunks - 1)


def _ple(x, p, params, tm, name):
    g_ple, w_gate, w_proj, g_final = params
    rows = x.shape[0]
    row_map = lambda i: (i, 0)
    return pl.pallas_call(
        functools.partial(_ple_body, tm),
        grid=(rows // tm,),
        in_specs=([pl.BlockSpec((tm, D_MODEL), row_map), pl.BlockSpec((tm, D_PLE), row_map),
                   _resident((1, D_MODEL))] + _panel_specs(D_MODEL, D_MODEL)
                  + _panel_specs(D_PLE, D_MODEL) + [_resident((1, D_MODEL))]),
        out_specs=pl.BlockSpec((tm, D_MODEL), row_map),
        out_shape=jax.ShapeDtypeStruct((rows, D_MODEL), F32),
        scratch_shapes=[pltpu.VMEM((PLE_SLOTS * CHUNK, D_MODEL), BF16),
                        pltpu.VMEM((PLE_SLOTS * CHUNK, D_MODEL), F32),
                        pltpu.VMEM((PLE_SLOTS * CHUNK, D_MODEL), F32)],
        compiler_params=pltpu.CompilerParams(dimension_semantics=("arbitrary",),
                                             vmem_limit_bytes=VMEM_LIMIT),
        name=name,
    )(x, p, g_ple, *[w_gate] * (D_MODEL // DOT_N), *[w_proj] * (D_MODEL // DOT_N), g_final)


def _conv_coef(w):
    c = w.shape[1]
    off = HIST_A - (CONV_A - 1)
    n = SUBLANES * (CONV_TILES + 1)
    wp = jnp.zeros((n, c), w.dtype).at[off:off + CONV_A].set(w)
    wp_prev = jnp.concatenate([jnp.zeros((SUBLANES, c), w.dtype), wp[:n - SUBLANES]], axis=0)
    r = jnp.arange(SUBLANES)[:, None]
    q = jnp.arange(CONV_TILES)[None, :]
    d = SUBLANES * q + r
    s = jnp.arange(SUBLANES)[None, None, :, None]
    coef = jnp.where(s >= r[:, :, None, None], wp[d][:, :, None, :], wp_prev[d][:, :, None, :])
    return coef.reshape(SUBLANES * CONV_TILES * SUBLANES, c)


def kernel(x_prompt, x_sample, p_prompt, p_sample, state_conv_a, state_ffn_conv, g_mix, w_in, w_dw_a, b_dw_a, g_ln_a, b_ln_a, g_ln_v, b_ln_v, w_s, b_s, w_out, g_ffn, w_up, w_dw_f, b_dw_f, w_down, g_ple, w_ple_gate, w_ple_proj, g_final):
    depth = w_in.shape[0]
    assert depth == 1, "single-layer step"
    batch, seq, _ = x_prompt.shape
    dec_batch, dec_seq, _ = x_sample.shape
    row = lambda v: v.reshape(1, -1)

    def bias_rows(length):
        b = jnp.tile(b_s[0][:, :length], (1, CHUNK // length))
        return jnp.repeat(b.T, HEAD_DIM, axis=1)

    def mix_weights(length):
        if length == CHUNK:
            return w_s[0]
        sel = (jnp.arange(CHUNK)[:, None] % length == jnp.arange(length)[None, :]).astype(F32)
        return jnp.einsum('ti,hij,sj->hts', sel, w_s[0][:, :length, :length], sel,
                          precision=lax.Precision.HIGHEST)

    mixer_common = (row(g_mix[0]), w_in[0].astype(BF16), _conv_coef(w_dw_a[0]), row(b_dw_a[0]),
                    row(g_ln_a[0]), row(b_ln_a[0]), row(g_ln_v[0]), row(b_ln_v[0]))
    w_out_b = w_out[0].astype(BF16)
    ffn_params = (row(g_ffn[0]), w_up[0].astype(BF16), w_dw_f[0], row(b_dw_f[0]),
                  w_down[0].astype(BF16))
    ple_params = (row(g_ple[0]), w_ple_gate[0].astype(BF16), w_ple_proj[0].astype(BF16),
                  row(g_final))

    xp = x_prompt.reshape(batch * seq, D_MODEL)
    lp = min(seq, CHUNK)
    xp, conv_p, cv_p = _mixer(xp, None, seq,
                              mixer_common + (mix_weights(lp), bias_rows(lp), w_out_b), tm=256)
    xp, lastv_p, lastg_p = _ffn(xp, None, seq, ffn_params, tm=1024, tf=512, rc=256)
    yp = _ple(xp, p_prompt[0].reshape(batch * seq, D_PLE), ple_params, 1024, "ple_prompt")

    xs = x_sample.reshape(dec_batch * dec_seq, D_MODEL)
    ls = min(dec_seq, CHUNK)
    state_pad = jnp.pad(state_conv_a[0], ((0, 0), (HIST_A - (CONV_A - 1), 0), (0, 0)))
    xs, a_s, cv_s = _mixer(xs, state_pad, dec_seq,
                           mixer_common + (mix_weights(ls), bias_rows(ls), w_out_b), tm=256)
    xs, upv_s, upg_s = _ffn(xs, state_ffn_conv[0], dec_seq, ffn_params, tm=1024, tf=512, rc=256)
    ys = _ple(xs, p_sample[0].reshape(dec_batch * dec_seq, D_PLE), ple_params, 1024, "ple_sample")

    keep = FFN_CONV - 1
    conv_a_prompt = conv_p[:, HIST_A - (CONV_A - 1):]
    conv_a_sample = jnp.concatenate(
        [state_conv_a[0][:, dec_seq:], a_s.reshape(dec_batch, dec_seq, C_A)], axis=1)
    per_seq = lastv_p.shape[0] // batch
    ffn_prompt = jnp.concatenate([lastv_p, lastg_p], axis=-1)[per_seq - 1::per_seq, SUBLANES - keep:]
    ffn_sample = jnp.concatenate([upv_s, upg_s], axis=-1)
    return (yp.reshape(batch, seq, D_MODEL), ys.reshape(dec_batch, dec_seq, D_MODEL),
            conv_a_prompt[None], conv_a_sample[None], ffn_prompt[None], ffn_sample[None],
            cv_p[None], cv_s.reshape(dec_batch, dec_seq, C_B)[None])
```

```python
import functools

import jax
import jax.numpy as jnp
from jax import lax
from jax.experimental import pallas as pl
from jax.experimental.pallas import tpu as pltpu

D_MODEL = 2048
HEAD_DIM = 128
C_A = D_MODEL // 2
C_B = D_MODEL - C_A
N_HEADS = C_B // HEAD_DIM
CONV_A = 31
HIST_A = 32
CONV_TILES = HIST_A // 8 + 1
CHUNK = 128
FFN_CONV = 3
D_FF = 5632
D_PLE = 256
EPS = 1e-6

SUBLANES = 8
ROWS = 32
LANES_EW = 128
FFN_K_SPLIT = 2
DOT_N = 512
MIXER_SLOTS = 2
PLE_SKEW = 2
PLE_SLOTS = 3
VMEM_LIMIT = 56 * 1024 * 1024

F32 = jnp.float32
BF16 = jnp.bfloat16


def _resident(shape):
    return pl.BlockSpec(shape, lambda *_: (0,) * len(shape), pipeline_mode=pl.Buffered(1))


def _panel_specs(k, n, index=None, resident=True):
    specs = []
    for g in range(n // DOT_N):
        imap = index(g) if index is not None else (lambda *_, g=g: (0, g))
        specs.append(pl.BlockSpec((k, DOT_N), imap,
                                  pipeline_mode=pl.Buffered(1) if resident else None))
    return specs


def _aligned(x, m):
    return x if isinstance(x, int) else pl.multiple_of(x, m)


def _row_loop(n_rows, step, fn):
    def body(c, carry):
        fn(pl.multiple_of(c * step, step))
        return carry
    lax.fori_loop(0, n_rows // step, body, 0, unroll=2)


def _rms_to(x_ref, g_ref, dst_ref, n_rows):
    def piece(r):
        xv = x_ref[pl.ds(r, ROWS), :]
        ms = jnp.mean(xv * xv, axis=-1, keepdims=True)
        dst_ref[pl.ds(r, ROWS), :] = (xv * lax.rsqrt(ms + EPS) * g_ref[...]).astype(BF16)
    _row_loop(n_rows, ROWS, piece)


def _head_ln(x, g, b):
    outs = []
    for h in range(x.shape[-1] // HEAD_DIM):
        xh = x[:, h * HEAD_DIM:(h + 1) * HEAD_DIM]
        mu = jnp.mean(xh, axis=-1, keepdims=True)
        xc = xh - mu
        var = jnp.mean(xc * xc, axis=-1, keepdims=True)
        outs.append(xc * lax.rsqrt(var + EPS))
    return jnp.concatenate(outs, axis=-1) * g + b


def _conv31(win3, n_out, coef_ref, lanes):
    acc = None
    for r in range(SUBLANES):
        inner = None
        for q in range(CONV_TILES):
            if q == 0 and r < HIST_A - (CONV_A - 1):
                continue
            k = (r * CONV_TILES + q) * SUBLANES
            term = coef_ref[k:k + SUBLANES, lanes] * win3[q:q + n_out]
            inner = term if inner is None else inner + term
        if r:
            inner = pltpu.roll(inner, SUBLANES - r, 1)
        acc = inner if acc is None else acc + inner
    return acc


def _mixer_body(sample, tm, seq_len, *refs):
    refs = list(refs)
    x_ref = refs.pop(0)
    st_ref = refs.pop(0) if sample else None
    gmix_ref = refs.pop(0)
    win_ref = [refs.pop(0) for _ in range((2 * C_A + 2 * C_B) // DOT_N)]
    (coef_ref, bdw_ref, glna_ref, blna_ref, glnv_ref, blnv_ref, ws_ref,
     bsb_ref) = [refs.pop(0) for _ in range(8)]
    wout_ref = [refs.pop(0) for _ in range(D_MODEL // DOT_N)]
    y_ref, aout_ref, vout_ref, h_ref, z_ref, cat_ref, a_ref, vn_ref, wm_ref = refs
    n_chunks = tm // CHUNK
    lane_groups = [slice(g * LANES_EW, (g + 1) * LANES_EW) for g in range(C_A // LANES_EW)]

    def norm(c):
        for r in range(c * CHUNK, (c + 1) * CHUNK, ROWS):
            xv = x_ref[r:r + ROWS, :]
            ms = jnp.mean(xv * xv, axis=-1, keepdims=True)
            h_ref[r:r + ROWS, :] = (xv * lax.rsqrt(ms + EPS) * gmix_ref[...]).astype(BF16)

    ti = lax.broadcasted_iota(jnp.int32, (CHUNK, CHUNK), 0)
    si = lax.broadcasted_iota(jnp.int32, (CHUNK, CHUNK), 1)
    if seq_len >= CHUNK:
        mask = si <= ti
    else:
        mask = ((si // seq_len) == (ti // seq_len)) & ((si % seq_len) <= (ti % seq_len))
    for h in range(N_HEADS):
        wm_ref[h] = jnp.where(mask, ws_ref[h], 0.0).astype(BF16)

    if not sample:
        t = pl.program_id(1)

        @pl.when(t == 0)
        def _():
            a_ref[0:HIST_A, :] = jnp.zeros((HIST_A, C_A), F32)

        @pl.when(t > 0)
        def _():
            a_ref[0:HIST_A, :] = a_ref[tm:tm + HIST_A, :]

    a_base = 0 if sample else HIST_A

    def slot_rows(c):
        lo = (c % MIXER_SLOTS) * CHUNK
        return slice(lo, lo + CHUNK)

    def in_proj(c):
        rows = slice(c * CHUNK, (c + 1) * CHUNK)
        hv = h_ref[rows, :]
        for g in range(len(win_ref)):
            z_ref[slot_rows(c), g * DOT_N:(g + 1) * DOT_N] = jnp.dot(
                hv, win_ref[g][...], preferred_element_type=F32)

    def out_proj(c):
        rows = slice(c * CHUNK, (c + 1) * CHUNK)
        cv = cat_ref[slot_rows(c), :]
        for g in range(len(wout_ref)):
            cols = slice(g * DOT_N, (g + 1) * DOT_N)
            y_ref[rows, cols] = x_ref[rows, cols] + jnp.dot(cv, wout_ref[g][...],
                                                            preferred_element_type=F32)

    def elementwise(c):
        r0 = c * CHUNK
        s0 = slot_rows(c).start - r0
        for r in range(r0, r0 + CHUNK, ROWS):
            for lanes in lane_groups:
                zv = z_ref[s0 + r:s0 + r + ROWS, lanes]
                zg = z_ref[s0 + r:s0 + r + ROWS, C_A + lanes.start:C_A + lanes.stop]
                a = zv * jax.nn.sigmoid(zg)
                a_ref[a_base + r:a_base + r + ROWS, lanes] = a
                if sample:
                    keep = CONV_A - 1 - seq_len
                    for q in range(ROWS // seq_len):
                        s = r // seq_len + q
                        aout_ref[s, 0:keep, lanes] = st_ref[s, HIST_A - keep:HIST_A, lanes]
                        aout_ref[s, keep:CONV_A - 1, lanes] = a[q * seq_len:(q + 1) * seq_len]
        for r in range(r0, r0 + CHUNK, ROWS):
            for lanes in lane_groups:
                width = lanes.stop - lanes.start
                if sample:
                    outs = []
                    for rs in range(r, r + ROWS, seq_len):
                        hist = st_ref[rs // seq_len, :, lanes].reshape(
                            CONV_TILES - 1, SUBLANES, width)
                        new = a_ref[rs:rs + seq_len, lanes].reshape(1, SUBLANES, width)
                        win3 = jnp.concatenate([hist, new], axis=0)
                        outs.append(_conv31(win3, 1, coef_ref, lanes))
                    conv = jnp.concatenate(outs, axis=0).reshape(ROWS, width)
                else:
                    n_out = ROWS // SUBLANES
                    win3 = a_ref[r:r + ROWS + HIST_A, lanes].reshape(
                        n_out + CONV_TILES - 1, SUBLANES, width)
                    conv = _conv31(win3, n_out, coef_ref, lanes).reshape(ROWS, width)
                conv = conv + bdw_ref[:, lanes]
                yv = _head_ln(conv, glna_ref[:, lanes], blna_ref[:, lanes])
                cat_ref[s0 + r:s0 + r + ROWS, lanes] = (yv * jax.nn.sigmoid(yv)).astype(BF16)

        v_off = 2 * C_A + C_B
        for r in range(r0, r0 + CHUNK, ROWS):
            for lanes in lane_groups:
                v = jax.nn.gelu(z_ref[s0 + r:s0 + r + ROWS, v_off + lanes.start:v_off + lanes.stop])
                vn = _head_ln(v, glnv_ref[:, lanes], blnv_ref[:, lanes])
                vn_ref[r - r0:r - r0 + ROWS, lanes] = vn.astype(BF16)
                if sample:
                    vout_ref[r:r + ROWS, lanes] = vn
                else:
                    vout_ref[0, r - r0:r - r0 + ROWS, lanes] = vn
        for h in range(N_HEADS):
            lo = h * HEAD_DIM
            mixed = jnp.dot(wm_ref[h], vn_ref[:, lo:lo + HEAD_DIM], preferred_element_type=F32)
            mixed = mixed + bsb_ref[:, lo:lo + HEAD_DIM]
            u = jax.nn.gelu(z_ref[slot_rows(c), 2 * C_A + lo:2 * C_A + lo + HEAD_DIM])
            cat_ref[slot_rows(c), C_A + lo:C_A + lo + HEAD_DIM] = (u * mixed).astype(BF16)

    norm(0)
    in_proj(0)
    for c in range(n_chunks):
        if c + 1 < n_chunks:
            norm(c + 1)
            in_proj(c + 1)
        elementwise(c)
        out_proj(c)

    if not sample:
        aout_ref[0] = a_ref[tm:tm + HIST_A, :]


def _mixer(x, state, seq_len, params, tm):
    (g_mix, w_in, w_dw, b_dw, g_ln_a, b_ln_a, g_ln_v, b_ln_v, w_s, bias_rows, w_out) = params
    rows = x.shape[0]
    sample = state is not None
    vec = lambda n: _resident((1, n))
    common_in = ([vec(D_MODEL)] + _panel_specs(D_MODEL, 2 * C_A + 2 * C_B)
                 + [_resident((SUBLANES * CONV_TILES * SUBLANES, C_A)),
                    vec(C_A), vec(C_A), vec(C_A), vec(C_B), vec(C_B),
                    _resident((N_HEADS, CHUNK, CHUNK)), _resident((CHUNK, C_B))]
                 + _panel_specs(C_A + C_B, D_MODEL))
    ring = MIXER_SLOTS * CHUNK
    common_scratch = [pltpu.VMEM((tm, D_MODEL), BF16),
                      pltpu.VMEM((ring, 2 * C_A + 2 * C_B), F32),
                      pltpu.VMEM((ring, C_A + C_B), BF16)]
    tail_scratch = [pltpu.VMEM((CHUNK, C_B), BF16),
                    pltpu.VMEM((N_HEADS, CHUNK, CHUNK), BF16)]
    if sample:
        n_seq = rows // seq_len
        spt = tm // seq_len
        grid = (rows // tm,)
        row_map = lambda i: (i, 0)
        in_specs = [pl.BlockSpec((tm, D_MODEL), row_map),
                    pl.BlockSpec((spt, HIST_A, C_A), lambda i: (i, 0, 0),
                                 pipeline_mode=pl.Buffered(1))] + common_in
        out_specs = [pl.BlockSpec((tm, D_MODEL), row_map),
                     pl.BlockSpec((spt, CONV_A - 1, C_A), lambda i: (i, 0, 0)),
                     pl.BlockSpec((tm, C_B), row_map)]
        out_shape = [jax.ShapeDtypeStruct((rows, D_MODEL), F32),
                     jax.ShapeDtypeStruct((n_seq, CONV_A - 1, C_A), F32),
                     jax.ShapeDtypeStruct((rows, C_B), F32)]
        scratch = common_scratch + [pltpu.VMEM((tm, C_A), F32)] + tail_scratch
        args = (x, state)
        sem = ("arbitrary",)
        del n_seq
    else:
        n_seq = rows // seq_len
        nt = seq_len // tm
        grid = (n_seq, nt)
        row_map = lambda b, t: (b * nt + t, 0)
        in_specs = [pl.BlockSpec((tm, D_MODEL), row_map)] + common_in
        out_specs = [pl.BlockSpec((tm, D_MODEL), row_map),
                     pl.BlockSpec((1, HIST_A, C_A), lambda b, t: (b, 0, 0)),
                     pl.BlockSpec((1, CHUNK, C_B), lambda b, t: (b, 0, 0))]
        out_shape = [jax.ShapeDtypeStruct((rows, D_MODEL), F32),
                     jax.ShapeDtypeStruct((n_seq, HIST_A, C_A), F32),
                     jax.ShapeDtypeStruct((n_seq, CHUNK, C_B), F32)]
        scratch = common_scratch + [pltpu.VMEM((tm + HIST_A, C_A), F32)] + tail_scratch
        args = (x,)
        sem = ("arbitrary", "arbitrary")
    return pl.pallas_call(
        functools.partial(_mixer_body, sample, tm, seq_len),
        grid=grid, in_specs=in_specs, out_specs=out_specs, out_shape=out_shape,
        scratch_shapes=scratch,
        compiler_params=pltpu.CompilerParams(dimension_semantics=sem,
                                             vmem_limit_bytes=VMEM_LIMIT),
        name="mixer_sample" if sample else "mixer_prompt",
    )(*args, g_mix, *[w_in] * (w_in.shape[1] // DOT_N), w_dw, b_dw, g_ln_a, b_ln_a, g_ln_v,
      b_ln_v, w_s, bias_rows, *[w_out] * (w_out.shape[1] // DOT_N))


def _ffn_body(sample, tm, tf, rc, blocks_per_seq, *refs):
    refs = list(refs)
    x_ref = refs.pop(0)
    stv_ref, stg_ref = (refs.pop(0), refs.pop(0)) if sample else (None, None)
    (g_ref, wuv_ref, wug_ref, wdv_ref, wdg_ref, bdv_ref,
     bdg_ref) = [refs.pop(0) for _ in range(7)]
    wdn_ref = [refs.pop(0) for _ in range(D_MODEL // DOT_N)]
    y_ref, lastv_ref, lastg_ref, h_ref, upv_ref, upg_ref, gate_ref = refs[:7]
    carry_ref = None if sample else refs[7]
    i = pl.program_id(0)
    j = pl.program_id(1)

    @pl.when(j == 0)
    def _():
        _rms_to(x_ref, g_ref, h_ref, tm)
        y_ref[...] = x_ref[...]
        if not sample:
            @pl.when(i == 0)
            def _():
                carry_ref[...] = jnp.zeros(carry_ref.shape, F32)

    hist = SUBLANES
    starts = list(range(0, tm - rc, rc)) + [tm - rc, tm - rc // 2]
    chunks = [(r0, r1 - r0) for r0, r1 in zip(starts, starts[1:] + [tm])]
    ups = ((upv_ref, wuv_ref, lastv_ref), (upg_ref, wug_ref, lastg_ref))

    if not sample:
        first = (i % blocks_per_seq) == 0
        for half, (up_ref, _, _) in enumerate(ups):
            up_ref[0:hist, :] = jnp.where(first, 0.0, carry_ref[j, half])

    def up_proj(c, half):
        r0, n = chunks[c]
        up_ref, wu_ref, _ = ups[half]
        up_ref[hist + r0:hist + r0 + n, :] = jnp.dot(h_ref[r0:r0 + n, :], wu_ref[...],
                                                     preferred_element_type=F32)

    def conv(up_ref, st_ref, wd_ref, bd_ref, row, lanes):
        width = lanes.stop - lanes.start
        w0 = wd_ref[0:1, lanes]
        w1 = wd_ref[1:2, lanes]
        w2 = wd_ref[2:3, lanes]
        n = ROWS // SUBLANES
        if sample:
            s0 = row // SUBLANES
            cur3 = up_ref[hist + row:hist + row + ROWS, lanes].reshape(n, SUBLANES, width)
            sl = lax.broadcasted_iota(jnp.int32, (n, SUBLANES, width), 1)
            st0 = st_ref[s0:s0 + n, 0:1, lanes]
            st1 = st_ref[s0:s0 + n, 1:2, lanes]
            m1 = jnp.where(sl == 0, st1, pltpu.roll(cur3, 1, 1))
            m2 = jnp.where(sl == 0, st0, jnp.where(sl == 1, st1, pltpu.roll(cur3, 2, 1)))
            out = w0 * m2 + w1 * m1 + w2 * cur3 + bd_ref[:, lanes]
            return out.reshape(ROWS, width)
        win3 = up_ref[row:row + ROWS + hist, lanes].reshape(n + 1, SUBLANES, width)
        top = lax.broadcasted_iota(jnp.int32, (n, SUBLANES, width), 1) == 0
        r0 = pltpu.roll(w0 * win3, 1, 1)
        s0 = jnp.concatenate([r0[0:1], jnp.where(top, r0[0:n], r0[1:n + 1])], axis=0)
        r1 = pltpu.roll(w1 * win3 + s0, 1, 1)
        s1 = jnp.where(top, r1[0:n], r1[1:n + 1])
        return (w2 * win3[1:n + 1] + s1 + bd_ref[:, lanes]).reshape(ROWS, width)

    def gate_down(c, kh):
        r0, n = chunks[c]
        kw = tf // FFN_K_SPLIT
        lanes = slice(kh * kw, (kh + 1) * kw)
        for r in range(0, n, ROWS):
            cv = conv(upv_ref, stv_ref if sample else None, wdv_ref, bdv_ref, r0 + r, lanes)
            cg = conv(upg_ref, stg_ref if sample else None, wdg_ref, bdg_ref, r0 + r, lanes)
            gate_ref[r:r + ROWS, lanes] = (cv * jax.nn.sigmoid(cv) * cg).astype(BF16)
        gv = gate_ref[0:n, lanes]
        for g in range(len(wdn_ref)):
            cols = slice(g * DOT_N, (g + 1) * DOT_N)
            y_ref[r0:r0 + n, cols] += jnp.dot(gv, wdn_ref[g][lanes, :],
                                              preferred_element_type=F32)

    up_proj(0, 0)
    up_proj(0, 1)
    for c in range(len(chunks)):
        if c + 1 < len(chunks):
            up_proj(c + 1, 0)
            up_proj(c + 1, 1)
        for kh in range(FFN_K_SPLIT):
            gate_down(c, kh)

    for half, (up_ref, _, last_ref) in enumerate(ups):
        if sample:
            up3 = up_ref[hist:hist + tm, :].reshape(tm // SUBLANES, SUBLANES, tf)
            last_ref[...] = up3[:, SUBLANES - (FFN_CONV - 1):, :]
        else:
            tail = up_ref[tm:tm + hist, :]
            carry_ref[j, half] = tail
            last_ref[0] = tail


def _ffn(x, state, seq_len, params, tm, tf, rc):
    g_ffn, w_up, w_dw, b_dw, w_down = params
    rows = x.shape[0]
    sample = state is not None
    nj = D_FF // tf
    grid = (rows // tm, nj)
    row_map = lambda i, j: (i, 0)
    val_map = lambda i, j: (0, j)
    gate_map = lambda i, j: (0, nj + j)
    in_specs = [pl.BlockSpec((tm, D_MODEL), row_map)]
    args = [x]
    if sample:
        spt = tm // seq_len
        in_specs += [pl.BlockSpec((spt, FFN_CONV - 1, tf), lambda i, j: (i, 0, j)),
                     pl.BlockSpec((spt, FFN_CONV - 1, tf), lambda i, j: (i, 0, nj + j))]
        args += [state, state]
    in_specs += [pl.BlockSpec((1, D_MODEL), lambda i, j: (0, 0)),
                 pl.BlockSpec((D_MODEL, tf), val_map), pl.BlockSpec((D_MODEL, tf), gate_map),
                 pl.BlockSpec((FFN_CONV, tf), val_map), pl.BlockSpec((FFN_CONV, tf), gate_map),
                 pl.BlockSpec((1, tf), val_map), pl.BlockSpec((1, tf), gate_map)]
    in_specs += _panel_specs(tf, D_MODEL, index=lambda g: (lambda i, j: (j, g)), resident=False)
    args += [g_ffn, w_up, w_up, w_dw, w_dw, b_dw, b_dw] + [w_down] * (D_MODEL // DOT_N)
    scratch = [pltpu.VMEM((tm, D_MODEL), BF16),
               pltpu.VMEM((tm + SUBLANES, tf), F32), pltpu.VMEM((tm + SUBLANES, tf), F32),
               pltpu.VMEM((rc, tf), BF16)]
    if sample:
        last_specs = [pl.BlockSpec((spt, FFN_CONV - 1, tf), lambda i, j: (i, 0, j))] * 2
        last_shape = [jax.ShapeDtypeStruct((rows // seq_len, FFN_CONV - 1, D_FF), F32)] * 2
        blocks_per_seq = 1
    else:
        blocks_per_seq = seq_len // tm
        last_specs = [pl.BlockSpec((1, SUBLANES, tf), lambda i, j: (i, 0, j))] * 2
        last_shape = [jax.ShapeDtypeStruct((rows // tm, SUBLANES, D_FF), F32)] * 2
        scratch += [pltpu.VMEM((nj, 2, SUBLANES, tf), F32)]
    return pl.pallas_call(
        functools.partial(_ffn_body, sample, tm, tf, rc, blocks_per_seq),
        grid=grid, in_specs=in_specs,
        out_specs=[pl.BlockSpec((tm, D_MODEL), row_map)] + last_specs,
        out_shape=[jax.ShapeDtypeStruct((rows, D_MODEL), F32)] + last_shape,
        scratch_shapes=scratch,
        compiler_params=pltpu.CompilerParams(dimension_semantics=("arbitrary", "arbitrary"),
                                             vmem_limit_bytes=VMEM_LIMIT),
        name="ffn_sample" if sample else "ffn_prompt",
    )(*args)


def _ple_body(tm, *refs):
    refs = list(refs)
    x_ref, p_ref, g_ref = refs[:3]
    n_panels = D_MODEL // DOT_N
    wg_ref = refs[3:3 + n_panels]
    wp_ref = refs[3 + n_panels:3 + 2 * n_panels]
    gf_ref, y_ref, h_ref, gate_ref, proj_ref = refs[3 + 2 * n_panels:]
    _ple_pipeline(tm, x_ref, p_ref, g_ref, wg_ref, wp_ref, gf_ref, y_ref, h_ref, gate_ref, proj_ref)


def _ple_pipeline(tm, x_ref, p_ref, g_ref, wg_ref, wp_ref, gf_ref, y_ref, h_ref, gate_ref, proj_ref):
    n_chunks = tm // CHUNK
    never = pl.program_id(0) < 0
    pieces = CHUNK // ROWS
    anchors = {}

    def norm(c):
        slot = (c % PLE_SLOTS) * CHUNK
        for k in range(pieces):
            r = c * CHUNK + k * ROWS
            xv = x_ref[r:r + ROWS, :]
            ms = jnp.mean(xv * xv, axis=-1, keepdims=True)
            hv = xv * lax.rsqrt(ms + EPS) * g_ref[...]
            anchor = anchors.pop((c - PLE_SKEW, k), None)
            if anchor is not None:
                head = jnp.where(never, anchor, hv[:, :HEAD_DIM])
                hv = jnp.concatenate([head, hv[:, HEAD_DIM:]], axis=-1)
            h_ref[slot + k * ROWS:slot + (k + 1) * ROWS, :] = hv.astype(BF16)

    def projections(c):
        slot = (c % PLE_SLOTS) * CHUNK
        rows = slice(c * CHUNK, (c + 1) * CHUNK)
        pv = p_ref[rows, :].astype(BF16)
        for g in range(D_MODEL // DOT_N):
            cols = slice(g * DOT_N, (g + 1) * DOT_N)
            gate_ref[slot:slot + CHUNK, cols] = jnp.dot(
                h_ref[slot:slot + CHUNK, :], wg_ref[g][...], preferred_element_type=F32)
            proj_ref[slot:slot + CHUNK, cols] = jnp.dot(pv, wp_ref[g][...],
                                                        preferred_element_type=F32)

    def finish(c):
        slot = (c % PLE_SLOTS) * CHUNK
        for k in range(pieces):
            r = c * CHUNK + k * ROWS
            s = slot + k * ROWS
            xv = x_ref[r:r + ROWS, :] + jax.nn.sigmoid(gate_ref[s:s + ROWS, :]) * proj_ref[s:s + ROWS, :]
            ms = jnp.mean(xv * xv, axis=-1, keepdims=True)
            yv = xv * lax.rsqrt(ms + EPS) * gf_ref[...]
            y_ref[r:r + ROWS, :] = yv
            anchors[(c, k)] = yv[:, :HEAD_DIM]

    norm(0)
    for c in range(n_chunks):
        projections(c)
        if c >= 1:
            finish(c - 1)
        if c + 1 < n_chunks:
            norm(c + 1)
    finish(n_chunks - 1)


def _ple(x, p, params, tm, name):
    g_ple, w_gate, w_proj, g_final = params
    rows = x.shape[0]
    row_map = lambda i: (i, 0)
    return pl.pallas_call(
        functools.partial(_ple_body, tm),
        grid=(rows // tm,),
        in_specs=([pl.BlockSpec((tm, D_MODEL), row_map), pl.BlockSpec((tm, D_PLE), row_map),
                   _resident((1, D_MODEL))] + _panel_specs(D_MODEL, D_MODEL)
                  + _panel_specs(D_PLE, D_MODEL) + [_resident((1, D_MODEL))]),
        out_specs=pl.BlockSpec((tm, D_MODEL), row_map),
        out_shape=jax.ShapeDtypeStruct((rows, D_MODEL), F32),
        scratch_shapes=[pltpu.VMEM((PLE_SLOTS * CHUNK, D_MODEL), BF16),
                        pltpu.VMEM((PLE_SLOTS * CHUNK, D_MODEL), F32),
                        pltpu.VMEM((PLE_SLOTS * CHUNK, D_MODEL), F32)],
        compiler_params=pltpu.CompilerParams(dimension_semantics=("arbitrary",),
                                             vmem_limit_bytes=VMEM_LIMIT),
        name=name,
    )(x, p, g_ple, *[w_gate] * (D_MODEL // DOT_N), *[w_proj] * (D_MODEL // DOT_N), g_final)


def _conv_coef(w):
    c = w.shape[1]
    off = HIST_A - (CONV_A - 1)
    n = SUBLANES * (CONV_TILES + 1)
    wp = jnp.zeros((n, c), w.dtype).at[off:off + CONV_A].set(w)
    wp_prev = jnp.concatenate([jnp.zeros((SUBLANES, c), w.dtype), wp[:n - SUBLANES]], axis=0)
    r = jnp.arange(SUBLANES)[:, None]
    q = jnp.arange(CONV_TILES)[None, :]
    d = SUBLANES * q + r
    s = jnp.arange(SUBLANES)[None, None, :, None]
    coef = jnp.where(s >= r[:, :, None, None], wp[d][:, :, None, :], wp_prev[d][:, :, None, :])
    return coef.reshape(SUBLANES * CONV_TILES * SUBLANES, c)


def kernel(x_prompt, x_sample, p_prompt, p_sample, state_conv_a, state_ffn_conv, g_mix, w_in, w_dw_a, b_dw_a, g_ln_a, b_ln_a, g_ln_v, b_ln_v, w_s, b_s, w_out, g_ffn, w_up, w_dw_f, b_dw_f, w_down, g_ple, w_ple_gate, w_ple_proj, g_final):
    depth = w_in.shape[0]
    assert depth == 1, "single-layer step"
    batch, seq, _ = x_prompt.shape
    dec_batch, dec_seq, _ = x_sample.shape
    row = lambda v: v.reshape(1, -1)

    def bias_rows(length):
        b = jnp.tile(b_s[0][:, :length], (1, CHUNK // length))
        return jnp.repeat(b.T, HEAD_DIM, axis=1)

    def mix_weights(length):
        if length == CHUNK:
            return w_s[0]
        sel = (jnp.arange(CHUNK)[:, None] % length == jnp.arange(length)[None, :]).astype(F32)
        return jnp.einsum('ti,hij,sj->hts', sel, w_s[0][:, :length, :length], sel,
                          precision=lax.Precision.HIGHEST)

    mixer_common = (row(g_mix[0]), w_in[0].astype(BF16), _conv_coef(w_dw_a[0]), row(b_dw_a[0]),
                    row(g_ln_a[0]), row(b_ln_a[0]), row(g_ln_v[0]), row(b_ln_v[0]))
    w_out_b = w_out[0].astype(BF16)
    ffn_params = (row(g_ffn[0]), w_up[0].astype(BF16), w_dw_f[0], row(b_dw_f[0]),
                  w_down[0].astype(BF16))
    ple_params = (row(g_ple[0]), w_ple_gate[0].astype(BF16), w_ple_proj[0].astype(BF16),
                  row(g_final))

    xp = x_prompt.reshape(batch * seq, D_MODEL)
    lp = min(seq, CHUNK)
    xp, conv_p, cv_p = _mixer(xp, None, seq,
                              mixer_common + (mix_weights(lp), bias_rows(lp), w_out_b), tm=256)
    xp, lastv_p, lastg_p = _ffn(xp, None, seq, ffn_params, tm=1024, tf=512, rc=256)
    yp = _ple(xp, p_prompt[0].reshape(batch * seq, D_PLE), ple_params, 1024, "ple_prompt")

    xs = x_sample.reshape(dec_batch * dec_seq, D_MODEL)
    ls = min(dec_seq, CHUNK)
    state_pad = jnp.pad(state_conv_a[0], ((0, 0), (HIST_A - (CONV_A - 1), 0), (0, 0)))
    xs, a_s, cv_s = _mixer(xs, state_pad, dec_seq,
                           mixer_common + (mix_weights(ls), bias_rows(ls), w_out_b), tm=256)
    xs, upv_s, upg_s = _ffn(xs, state_ffn_conv[0], dec_seq, ffn_params, tm=1024, tf=512, rc=256)
    ys = _ple(xs, p_sample[0].reshape(dec_batch * dec_seq, D_PLE), ple_params, 1024, "ple_sample")

    keep = FFN_CONV - 1
    conv_a_prompt = conv_p[:, HIST_A - (CONV_A - 1):]
    conv_a_sample = a_s
    per_seq = lastv_p.shape[0] // batch
    ffn_prompt = jnp.concatenate([lastv_p, lastg_p], axis=-1)[per_seq - 1::per_seq, SUBLANES - keep:]
    ffn_sample = jnp.concatenate([upv_s, upg_s], axis=-1)
    return (yp.reshape(batch, seq, D_MODEL), ys.reshape(dec_batch, dec_seq, D_MODEL),
            conv_a_prompt[None], conv_a_sample[None], ffn_prompt[None], ffn_sample[None],
            cv_p[None], cv_s.reshape(dec_batch, dec_seq, C_B)[None])
```

```python
import functools

import jax
import jax.numpy as jnp
from jax import lax
from jax.experimental import pallas as pl
from jax.experimental.pallas import tpu as pltpu

D_MODEL = 2048
HEAD_DIM = 128
C_A = D_MODEL // 2
C_B = D_MODEL - C_A
N_HEADS = C_B // HEAD_DIM
CONV_A = 31
HIST_A = 32
CONV_TILES = HIST_A // 8 + 1
CHUNK = 128
FFN_CONV = 3
D_FF = 5632
D_PLE = 256
EPS = 1e-6

SUBLANES = 8
ROWS = 32
LANES_EW = 128
FFN_K_SPLIT = 2
DOT_N = 512
MIXER_SLOTS = 2
PLE_SKEW = 2
PLE_SLOTS = 3
VMEM_LIMIT = 56 * 1024 * 1024

F32 = jnp.float32
BF16 = jnp.bfloat16


def _resident(shape):
    return pl.BlockSpec(shape, lambda *_: (0,) * len(shape), pipeline_mode=pl.Buffered(1))


def _panel_specs(k, n, index=None, resident=True):
    specs = []
    for g in range(n // DOT_N):
        imap = index(g) if index is not None else (lambda *_, g=g: (0, g))
        specs.append(pl.BlockSpec((k, DOT_N), imap,
                                  pipeline_mode=pl.Buffered(1) if resident else None))
    return specs


def _aligned(x, m):
    return x if isinstance(x, int) else pl.multiple_of(x, m)


def _row_loop(n_rows, step, fn):
    def body(c, carry):
        fn(pl.multiple_of(c * step, step))
        return carry
    lax.fori_loop(0, n_rows // step, body, 0)


def _rms_to(x_ref, g_ref, dst_ref, n_rows):
    def piece(r):
        xv = x_ref[pl.ds(r, ROWS), :]
        ms = jnp.mean(xv * xv, axis=-1, keepdims=True)
        dst_ref[pl.ds(r, ROWS), :] = (xv * lax.rsqrt(ms + EPS) * g_ref[...]).astype(BF16)
    _row_loop(n_rows, ROWS, piece)


def _head_ln(x, g, b):
    outs = []
    for h in range(x.shape[-1] // HEAD_DIM):
        xh = x[:, h * HEAD_DIM:(h + 1) * HEAD_DIM]
        mu = jnp.mean(xh, axis=-1, keepdims=True)
        xc = xh - mu
        var = jnp.mean(xc * xc, axis=-1, keepdims=True)
        outs.append(xc * lax.rsqrt(var + EPS))
    return jnp.concatenate(outs, axis=-1) * g + b


def _conv31(win3, n_out, coef_ref, lanes):
    acc = None
    for r in range(SUBLANES):
        inner = None
        for q in range(CONV_TILES):
            if q == 0 and r < HIST_A - (CONV_A - 1):
                continue
            k = (r * CONV_TILES + q) * SUBLANES
            term = coef_ref[k:k + SUBLANES, lanes] * win3[q:q + n_out]
            inner = term if inner is None else inner + term
        if r:
            inner = pltpu.roll(inner, SUBLANES - r, 1)
        acc = inner if acc is None else acc + inner
    return acc


def _mixer_body(sample, tm, seq_len, *refs):
    refs = list(refs)
    x_ref = refs.pop(0)
    st_ref = refs.pop(0) if sample else None
    gmix_ref = refs.pop(0)
    win_ref = [refs.pop(0) for _ in range((2 * C_A + 2 * C_B) // DOT_N)]
    (coef_ref, bdw_ref, glna_ref, blna_ref, glnv_ref, blnv_ref, ws_ref,
     bsb_ref) = [refs.pop(0) for _ in range(8)]
    wout_ref = [refs.pop(0) for _ in range(D_MODEL // DOT_N)]
    y_ref, aout_ref, vout_ref, h_ref, z_ref, cat_ref, a_ref, vn_ref, wm_ref = refs
    n_chunks = tm // CHUNK
    lane_groups = [slice(g * LANES_EW, (g + 1) * LANES_EW) for g in range(C_A // LANES_EW)]

    def norm(c):
        for r in range(c * CHUNK, (c + 1) * CHUNK, ROWS):
            xv = x_ref[r:r + ROWS, :]
            ms = jnp.mean(xv * xv, axis=-1, keepdims=True)
            h_ref[r:r + ROWS, :] = (xv * lax.rsqrt(ms + EPS) * gmix_ref[...]).astype(BF16)

    ti = lax.broadcasted_iota(jnp.int32, (CHUNK, CHUNK), 0)
    si = lax.broadcasted_iota(jnp.int32, (CHUNK, CHUNK), 1)
    if seq_len >= CHUNK:
        mask = si <= ti
    else:
        mask = ((si // seq_len) == (ti // seq_len)) & ((si % seq_len) <= (ti % seq_len))
    for h in range(N_HEADS):
        wm_ref[h] = jnp.where(mask, ws_ref[h], 0.0).astype(BF16)

    if not sample:
        t = pl.program_id(1)

        @pl.when(t == 0)
        def _():
            a_ref[0:HIST_A, :] = jnp.zeros((HIST_A, C_A), F32)

        @pl.when(t > 0)
        def _():
            a_ref[0:HIST_A, :] = a_ref[tm:tm + HIST_A, :]

    a_base = 0 if sample else HIST_A

    def slot_rows(c):
        lo = (c % MIXER_SLOTS) * CHUNK
        return slice(lo, lo + CHUNK)

    def in_proj(c):
        rows = slice(c * CHUNK, (c + 1) * CHUNK)
        hv = h_ref[rows, :]
        for g in range(len(win_ref)):
            z_ref[slot_rows(c), g * DOT_N:(g + 1) * DOT_N] = jnp.dot(
                hv, win_ref[g][...], preferred_element_type=F32)

    def out_proj(c):
        rows = slice(c * CHUNK, (c + 1) * CHUNK)
        cv = cat_ref[slot_rows(c), :]
        for g in range(len(wout_ref)):
            cols = slice(g * DOT_N, (g + 1) * DOT_N)
            y_ref[rows, cols] = x_ref[rows, cols] + jnp.dot(cv, wout_ref[g][...],
                                                            preferred_element_type=F32)

    def elementwise(c):
        r0 = c * CHUNK
        s0 = slot_rows(c).start - r0
        for r in range(r0, r0 + CHUNK, ROWS):
            for lanes in lane_groups:
                zv = z_ref[s0 + r:s0 + r + ROWS, lanes]
                zg = z_ref[s0 + r:s0 + r + ROWS, C_A + lanes.start:C_A + lanes.stop]
                a = zv * jax.nn.sigmoid(zg)
                a_ref[a_base + r:a_base + r + ROWS, lanes] = a
                if sample:
                    keep = CONV_A - 1 - seq_len
                    for q in range(ROWS // seq_len):
                        s = r // seq_len + q
                        aout_ref[s, 0:keep, lanes] = st_ref[s, HIST_A - keep:HIST_A, lanes]
                        aout_ref[s, keep:CONV_A - 1, lanes] = a[q * seq_len:(q + 1) * seq_len]
        for r in range(r0, r0 + CHUNK, ROWS):
            for lanes in lane_groups:
                width = lanes.stop - lanes.start
                if sample:
                    outs = []
                    for rs in range(r, r + ROWS, seq_len):
                        hist = st_ref[rs // seq_len, :, lanes].reshape(
                            CONV_TILES - 1, SUBLANES, width)
                        new = a_ref[rs:rs + seq_len, lanes].reshape(1, SUBLANES, width)
                        win3 = jnp.concatenate([hist, new], axis=0)
                        outs.append(_conv31(win3, 1, coef_ref, lanes))
                    conv = jnp.concatenate(outs, axis=0).reshape(ROWS, width)
                else:
                    n_out = ROWS // SUBLANES
                    win3 = a_ref[r:r + ROWS + HIST_A, lanes].reshape(
                        n_out + CONV_TILES - 1, SUBLANES, width)
                    conv = _conv31(win3, n_out, coef_ref, lanes).reshape(ROWS, width)
                conv = conv + bdw_ref[:, lanes]
                yv = _head_ln(conv, glna_ref[:, lanes], blna_ref[:, lanes])
                cat_ref[s0 + r:s0 + r + ROWS, lanes] = (yv * jax.nn.sigmoid(yv)).astype(BF16)

        v_off = 2 * C_A + C_B
        for r in range(r0, r0 + CHUNK, ROWS):
            for lanes in lane_groups:
                v = jax.nn.gelu(z_ref[s0 + r:s0 + r + ROWS, v_off + lanes.start:v_off + lanes.stop])
                vn = _head_ln(v, glnv_ref[:, lanes], blnv_ref[:, lanes])
                vn_ref[r - r0:r - r0 + ROWS, lanes] = vn.astype(BF16)
                if sample:
                    vout_ref[r:r + ROWS, lanes] = vn
                else:
                    vout_ref[0, r - r0:r - r0 + ROWS, lanes] = vn
        for h in range(N_HEADS):
            lo = h * HEAD_DIM
            mixed = jnp.dot(wm_ref[h], vn_ref[:, lo:lo + HEAD_DIM], preferred_element_type=F32)
            mixed = mixed + bsb_ref[:, lo:lo + HEAD_DIM]
            u = jax.nn.gelu(z_ref[slot_rows(c), 2 * C_A + lo:2 * C_A + lo + HEAD_DIM])
            cat_ref[slot_rows(c), C_A + lo:C_A + lo + HEAD_DIM] = (u * mixed).astype(BF16)

    norm(0)
    in_proj(0)
    for c in range(n_chunks):
        if c + 1 < n_chunks:
            norm(c + 1)
            in_proj(c + 1)
        elementwise(c)
        out_proj(c)

    if not sample:
        aout_ref[0] = a_ref[tm:tm + HIST_A, :]


def _mixer(x, state, seq_len, params, tm):
    (g_mix, w_in, w_dw, b_dw, g_ln_a, b_ln_a, g_ln_v, b_ln_v, w_s, bias_rows, w_out) = params
    rows = x.shape[0]
    sample = state is not None
    vec = lambda n: _resident((1, n))
    common_in = ([vec(D_MODEL)] + _panel_specs(D_MODEL, 2 * C_A + 2 * C_B)
                 + [_resident((SUBLANES * CONV_TILES * SUBLANES, C_A)),
                    vec(C_A), vec(C_A), vec(C_A), vec(C_B), vec(C_B),
                    _resident((N_HEADS, CHUNK, CHUNK)), _resident((CHUNK, C_B))]
                 + _panel_specs(C_A + C_B, D_MODEL))
    ring = MIXER_SLOTS * CHUNK
    common_scratch = [pltpu.VMEM((tm, D_MODEL), BF16),
                      pltpu.VMEM((ring, 2 * C_A + 2 * C_B), F32),
                      pltpu.VMEM((ring, C_A + C_B), BF16)]
    tail_scratch = [pltpu.VMEM((CHUNK, C_B), BF16),
                    pltpu.VMEM((N_HEADS, CHUNK, CHUNK), BF16)]
    if sample:
        n_seq = rows // seq_len
        spt = tm // seq_len
        grid = (rows // tm,)
        row_map = lambda i: (i, 0)
        in_specs = [pl.BlockSpec((tm, D_MODEL), row_map),
                    pl.BlockSpec((spt, HIST_A, C_A), lambda i: (i, 0, 0),
                                 pipeline_mode=pl.Buffered(1))] + common_in
        out_specs = [pl.BlockSpec((tm, D_MODEL), row_map),
                     pl.BlockSpec((spt, CONV_A - 1, C_A), lambda i: (i, 0, 0)),
                     pl.BlockSpec((tm, C_B), row_map)]
        out_shape = [jax.ShapeDtypeStruct((rows, D_MODEL), F32),
                     jax.ShapeDtypeStruct((n_seq, CONV_A - 1, C_A), F32),
                     jax.ShapeDtypeStruct((rows, C_B), F32)]
        scratch = common_scratch + [pltpu.VMEM((tm, C_A), F32)] + tail_scratch
        args = (x, state)
        sem = ("arbitrary",)
        del n_seq
    else:
        n_seq = rows // seq_len
        nt = seq_len // tm
        grid = (n_seq, nt)
        row_map = lambda b, t: (b * nt + t, 0)
        in_specs = [pl.BlockSpec((tm, D_MODEL), row_map)] + common_in
        out_specs = [pl.BlockSpec((tm, D_MODEL), row_map),
                     pl.BlockSpec((1, HIST_A, C_A), lambda b, t: (b, 0, 0)),
                     pl.BlockSpec((1, CHUNK, C_B), lambda b, t: (b, 0, 0))]
        out_shape = [jax.ShapeDtypeStruct((rows, D_MODEL), F32),
                     jax.ShapeDtypeStruct((n_seq, HIST_A, C_A), F32),
                     jax.ShapeDtypeStruct((n_seq, CHUNK, C_B), F32)]
        scratch = common_scratch + [pltpu.VMEM((tm + HIST_A, C_A), F32)] + tail_scratch
        args = (x,)
        sem = ("arbitrary", "arbitrary")
    return pl.pallas_call(
        functools.partial(_mixer_body, sample, tm, seq_len),
        grid=grid, in_specs=in_specs, out_specs=out_specs, out_shape=out_shape,
        scratch_shapes=scratch,
        compiler_params=pltpu.CompilerParams(dimension_semantics=sem,
                                             vmem_limit_bytes=VMEM_LIMIT),
        name="mixer_sample" if sample else "mixer_prompt",
    )(*args, g_mix, *[w_in] * (w_in.shape[1] // DOT_N), w_dw, b_dw, g_ln_a, b_ln_a, g_ln_v,
      b_ln_v, w_s, bias_rows, *[w_out] * (w_out.shape[1] // DOT_N))


def _ffn_body(sample, tm, tf, rc, blocks_per_seq, *refs):
    refs = list(refs)
    x_ref = refs.pop(0)
    stv_ref, stg_ref = (refs.pop(0), refs.pop(0)) if sample else (None, None)
    (g_ref, wuv_ref, wug_ref, wdv_ref, wdg_ref, bdv_ref,
     bdg_ref) = [refs.pop(0) for _ in range(7)]
    wdn_ref = [refs.pop(0) for _ in range(D_MODEL // DOT_N)]
    y_ref, lastv_ref, lastg_ref, h_ref, upv_ref, upg_ref, gate_ref = refs[:7]
    carry_ref = None if sample else refs[7]
    i = pl.program_id(0)
    j = pl.program_id(1)

    @pl.when(j == 0)
    def _():
        _rms_to(x_ref, g_ref, h_ref, tm)
        y_ref[...] = x_ref[...]
        if not sample:
            @pl.when(i == 0)
            def _():
                carry_ref[...] = jnp.zeros(carry_ref.shape, F32)

    hist = SUBLANES
    starts = list(range(0, tm - rc, rc)) + [tm - rc, tm - rc // 2]
    chunks = [(r0, r1 - r0) for r0, r1 in zip(starts, starts[1:] + [tm])]
    ups = ((upv_ref, wuv_ref, lastv_ref), (upg_ref, wug_ref, lastg_ref))

    if not sample:
        first = (i % blocks_per_seq) == 0
        for half, (up_ref, _, _) in enumerate(ups):
            up_ref[0:hist, :] = jnp.where(first, 0.0, carry_ref[j, half])

    def up_proj(c):
        r0, n = chunks[c]
        hc = h_ref[r0:r0 + n, :]
        for up_ref, wu_ref, _ in ups:
            up_ref[hist + r0:hist + r0 + n, :] = jnp.dot(hc, wu_ref[...],
                                                         preferred_element_type=F32)

    def conv(up_ref, st_ref, wd_ref, bd_ref, row, lanes):
        width = lanes.stop - lanes.start
        w0 = wd_ref[0:1, lanes]
        w1 = wd_ref[1:2, lanes]
        w2 = wd_ref[2:3, lanes]
        n = ROWS // SUBLANES
        if sample:
            s0 = row // SUBLANES
            cur3 = up_ref[hist + row:hist + row + ROWS, lanes].reshape(n, SUBLANES, width)
            sl = lax.broadcasted_iota(jnp.int32, (n, SUBLANES, width), 1)
            st0 = st_ref[s0:s0 + n, 0:1, lanes]
            st1 = st_ref[s0:s0 + n, 1:2, lanes]
            m1 = jnp.where(sl == 0, st1, pltpu.roll(cur3, 1, 1))
            m2 = jnp.where(sl == 0, st0, jnp.where(sl == 1, st1, pltpu.roll(cur3, 2, 1)))
            out = w0 * m2 + w1 * m1 + w2 * cur3 + bd_ref[:, lanes]
            return out.reshape(ROWS, width)
        win3 = up_ref[row:row + ROWS + hist, lanes].reshape(n + 1, SUBLANES, width)
        top = lax.broadcasted_iota(jnp.int32, (n, SUBLANES, width), 1) == 0
        r0 = pltpu.roll(w0 * win3, 1, 1)
        s0 = jnp.concatenate([r0[0:1], jnp.where(top, r0[0:n], r0[1:n + 1])], axis=0)
        r1 = pltpu.roll(w1 * win3 + s0, 1, 1)
        s1 = jnp.where(top, r1[0:n], r1[1:n + 1])
        return (w2 * win3[1:n + 1] + s1 + bd_ref[:, lanes]).reshape(ROWS, width)

    def gate_down(c, kh):
        r0, n = chunks[c]
        kw = tf // FFN_K_SPLIT
        lanes = slice(kh * kw, (kh + 1) * kw)
        for r in range(0, n, ROWS):
            cv = conv(upv_ref, stv_ref if sample else None, wdv_ref, bdv_ref, r0 + r, lanes)
            cg = conv(upg_ref, stg_ref if sample else None, wdg_ref, bdg_ref, r0 + r, lanes)
            gate_ref[r:r + ROWS, lanes] = (cv * jax.nn.sigmoid(cv) * cg).astype(BF16)
        gv = gate_ref[0:n, lanes]
        for g in range(len(wdn_ref)):
            cols = slice(g * DOT_N, (g + 1) * DOT_N)
            y_ref[r0:r0 + n, cols] += jnp.dot(gv, wdn_ref[g][lanes, :],
                                              preferred_element_type=F32)

    up_proj(0)
    for c in range(len(chunks)):
        if c + 1 < len(chunks):
            up_proj(c + 1)
        for kh in range(FFN_K_SPLIT):
            gate_down(c, kh)

    for half, (up_ref, _, last_ref) in enumerate(ups):
        if sample:
            up3 = up_ref[hist:hist + tm, :].reshape(tm // SUBLANES, SUBLANES, tf)
            last_ref[...] = up3[:, SUBLANES - (FFN_CONV - 1):, :]
        else:
            tail = up_ref[tm:tm + hist, :]
            carry_ref[j, half] = tail
            last_ref[0] = tail


def _ffn(x, state, seq_len, params, tm, tf, rc):
    g_ffn, w_up, w_dw, b_dw, w_down = params
    rows = x.shape[0]
    sample = state is not None
    nj = D_FF // tf
    grid = (rows // tm, nj)
    row_map = lambda i, j: (i, 0)
    val_map = lambda i, j: (0, j)
    gate_map = lambda i, j: (0, nj + j)
    in_specs = [pl.BlockSpec((tm, D_MODEL), row_map)]
    args = [x]
    if sample:
        spt = tm // seq_len
        in_specs += [pl.BlockSpec((spt, FFN_CONV - 1, tf), lambda i, j: (i, 0, j)),
                     pl.BlockSpec((spt, FFN_CONV - 1, tf), lambda i, j: (i, 0, nj + j))]
        args += [state, state]
    in_specs += [pl.BlockSpec((1, D_MODEL), lambda i, j: (0, 0)),
                 pl.BlockSpec((D_MODEL, tf), val_map), pl.BlockSpec((D_MODEL, tf), gate_map),
                 pl.BlockSpec((FFN_CONV, tf), val_map), pl.BlockSpec((FFN_CONV, tf), gate_map),
                 pl.BlockSpec((1, tf), val_map), pl.BlockSpec((1, tf), gate_map)]
    in_specs += _panel_specs(tf, D_MODEL, index=lambda g: (lambda i, j: (j, g)), resident=False)
    args += [g_ffn, w_up, w_up, w_dw, w_dw, b_dw, b_dw] + [w_down] * (D_MODEL // DOT_N)
    scratch = [pltpu.VMEM((tm, D_MODEL), BF16),
               pltpu.VMEM((tm + SUBLANES, tf), F32), pltpu.VMEM((tm + SUBLANES, tf), F32),
               pltpu.VMEM((rc, tf), BF16)]
    if sample:
        last_specs = [pl.BlockSpec((spt, FFN_CONV - 1, tf), lambda i, j: (i, 0, j))] * 2
        last_shape = [jax.ShapeDtypeStruct((rows // seq_len, FFN_CONV - 1, D_FF), F32)] * 2
        blocks_per_seq = 1
    else:
        blocks_per_seq = seq_len // tm
        last_specs = [pl.BlockSpec((1, SUBLANES, tf), lambda i, j: (i, 0, j))] * 2
        last_shape = [jax.ShapeDtypeStruct((rows // tm, SUBLANES, D_FF), F32)] * 2
        scratch += [pltpu.VMEM((nj, 2, SUBLANES, tf), F32)]
    return pl.pallas_call(
        functools.partial(_ffn_body, sample, tm, tf, rc, blocks_per_seq),
        grid=grid, in_specs=in_specs,
        out_specs=[pl.BlockSpec((tm, D_MODEL), row_map)] + last_specs,
        out_shape=[jax.ShapeDtypeStruct((rows, D_MODEL), F32)] + last_shape,
        scratch_shapes=scratch,
        compiler_params=pltpu.CompilerParams(dimension_semantics=("arbitrary", "arbitrary"),
                                             vmem_limit_bytes=VMEM_LIMIT),
        name="ffn_sample" if sample else "ffn_prompt",
    )(*args)


def _ple_body(tm, *refs):
    refs = list(refs)
    x_ref, p_ref, g_ref = refs[:3]
    n_panels = D_MODEL // DOT_N
    wg_ref = refs[3:3 + n_panels]
    wp_ref = refs[3 + n_panels:3 + 2 * n_panels]
    gf_ref, y_ref, h_ref, gate_ref, proj_ref = refs[3 + 2 * n_panels:]
    _ple_pipeline(tm, x_ref, p_ref, g_ref, wg_ref, wp_ref, gf_ref, y_ref, h_ref, gate_ref, proj_ref)


def _ple_pipeline(tm, x_ref, p_ref, g_ref, wg_ref, wp_ref, gf_ref, y_ref, h_ref, gate_ref, proj_ref):
    n_chunks = tm // CHUNK
    never = pl.program_id(0) < 0
    pieces = CHUNK // ROWS
    anchors = {}

    def norm(c):
        slot = (c % PLE_SLOTS) * CHUNK
        for k in range(pieces):
            r = c * CHUNK + k * ROWS
            xv = x_ref[r:r + ROWS, :]
            ms = jnp.mean(xv * xv, axis=-1, keepdims=True)
            hv = xv * lax.rsqrt(ms + EPS) * g_ref[...]
            anchor = anchors.pop((c - PLE_SKEW, k), None)
            if anchor is not None:
                head = jnp.where(never, anchor, hv[:, :HEAD_DIM])
                hv = jnp.concatenate([head, hv[:, HEAD_DIM:]], axis=-1)
            h_ref[slot + k * ROWS:slot + (k + 1) * ROWS, :] = hv.astype(BF16)

    def projections(c):
        slot = (c % PLE_SLOTS) * CHUNK
        rows = slice(c * CHUNK, (c + 1) * CHUNK)
        pv = p_ref[rows, :].astype(BF16)
        for g in range(D_MODEL // DOT_N):
            cols = slice(g * DOT_N, (g + 1) * DOT_N)
            gate_ref[slot:slot + CHUNK, cols] = jnp.dot(
                h_ref[slot:slot + CHUNK, :], wg_ref[g][...], preferred_element_type=F32)
            proj_ref[slot:slot + CHUNK, cols] = jnp.dot(pv, wp_ref[g][...],
                                                        preferred_element_type=F32)

    def finish(c):
        slot = (c % PLE_SLOTS) * CHUNK
        for k in range(pieces):
            r = c * CHUNK + k * ROWS
            s = slot + k * ROWS
            xv = x_ref[r:r + ROWS, :] + jax.nn.sigmoid(gate_ref[s:s + ROWS, :]) * proj_ref[s:s + ROWS, :]
            ms = jnp.mean(xv * xv, axis=-1, keepdims=True)
            yv = xv * lax.rsqrt(ms + EPS) * gf_ref[...]
            y_ref[r:r + ROWS, :] = yv
            anchors[(c, k)] = yv[:, :HEAD_DIM]

    norm(0)
    for c in range(n_chunks):
        projections(c)
        if c >= 1:
            finish(c - 1)
        if c + 1 < n_chunks:
            norm(c + 1)
    finish(n_chunks - 1)


def _ple(x, p, params, tm, name):
    g_ple, w_gate, w_proj, g_final = params
    rows = x.shape[0]
    row_map = lambda i: (i, 0)
    return pl.pallas_call(
        functools.partial(_ple_body, tm),
        grid=(rows // tm,),
        in_specs=([pl.BlockSpec((tm, D_MODEL), row_map), pl.BlockSpec((tm, D_PLE), row_map),
                   _resident((1, D_MODEL))] + _panel_specs(D_MODEL, D_MODEL)
                  + _panel_specs(D_PLE, D_MODEL) + [_resident((1, D_MODEL))]),
        out_specs=pl.BlockSpec((tm, D_MODEL), row_map),
        out_shape=jax.ShapeDtypeStruct((rows, D_MODEL), F32),
        scratch_shapes=[pltpu.VMEM((PLE_SLOTS * CHUNK, D_MODEL), BF16),
                        pltpu.VMEM((PLE_SLOTS * CHUNK, D_MODEL), F32),
                        pltpu.VMEM((PLE_SLOTS * CHUNK, D_MODEL), F32)],
        compiler_params=pltpu.CompilerParams(dimension_semantics=("arbitrary",),
                                             vmem_limit_bytes=VMEM_LIMIT),
        name=name,
    )(x, p, g_ple, *[w_gate] * (D_MODEL // DOT_N), *[w_proj] * (D_MODEL // DOT_N), g_final)


def _conv_coef(w):
    c = w.shape[1]
    off = HIST_A - (CONV_A - 1)
    n = SUBLANES * (CONV_TILES + 1)
    wp = jnp.zeros((n, c), w.dtype).at[off:off + CONV_A].set(w)
    wp_prev = jnp.concatenate([jnp.zeros((SUBLANES, c), w.dtype), wp[:n - SUBLANES]], axis=0)
    r = jnp.arange(SUBLANES)[:, None]
    q = jnp.arange(CONV_TILES)[None, :]
    d = SUBLANES * q + r
    s = jnp.arange(SUBLANES)[None, None, :, None]
    coef = jnp.where(s >= r[:, :, None, None], wp[d][:, :, None, :], wp_prev[d][:, :, None, :])
    return coef.reshape(SUBLANES * CONV_TILES * SUBLANES, c)


def kernel(x_prompt, x_sample, p_prompt, p_sample, state_conv_a, state_ffn_conv, g_mix, w_in, w_dw_a, b_dw_a, g_ln_a, b_ln_a, g_ln_v, b_ln_v, w_s, b_s, w_out, g_ffn, w_up, w_dw_f, b_dw_f, w_down, g_ple, w_ple_gate, w_ple_proj, g_final):
    depth = w_in.shape[0]
    assert depth == 1, "single-layer step"
    batch, seq, _ = x_prompt.shape
    dec_batch, dec_seq, _ = x_sample.shape
    row = lambda v: v.reshape(1, -1)

    def bias_rows(length):
        b = jnp.tile(b_s[0][:, :length], (1, CHUNK // length))
        return jnp.repeat(b.T, HEAD_DIM, axis=1)

    def mix_weights(length):
        if length == CHUNK:
            return w_s[0]
        sel = (jnp.arange(CHUNK)[:, None] % length == jnp.arange(length)[None, :]).astype(F32)
        return jnp.einsum('ti,hij,sj->hts', sel, w_s[0][:, :length, :length], sel,
                          precision=lax.Precision.HIGHEST)

    mixer_common = (row(g_mix[0]), w_in[0].astype(BF16), _conv_coef(w_dw_a[0]), row(b_dw_a[0]),
                    row(g_ln_a[0]), row(b_ln_a[0]), row(g_ln_v[0]), row(b_ln_v[0]))
    w_out_b = w_out[0].astype(BF16)
    ffn_params = (row(g_ffn[0]), w_up[0].astype(BF16), w_dw_f[0], row(b_dw_f[0]),
                  w_down[0].astype(BF16))
    ple_params = (row(g_ple[0]), w_ple_gate[0].astype(BF16), w_ple_proj[0].astype(BF16),
                  row(g_final))

    xp = x_prompt.reshape(batch * seq, D_MODEL)
    lp = min(seq, CHUNK)
    xp, conv_p, cv_p = _mixer(xp, None, seq,
                              mixer_common + (mix_weights(lp), bias_rows(lp), w_out_b), tm=256)
    xp, lastv_p, lastg_p = _ffn(xp, None, seq, ffn_params, tm=1024, tf=512, rc=256)
    yp = _ple(xp, p_prompt[0].reshape(batch * seq, D_PLE), ple_params, 1024, "ple_prompt")

    xs = x_sample.reshape(dec_batch * dec_seq, D_MODEL)
    ls = min(dec_seq, CHUNK)
    state_pad = jnp.pad(state_conv_a[0], ((0, 0), (HIST_A - (CONV_A - 1), 0), (0, 0)))
    xs, a_s, cv_s = _mixer(xs, state_pad, dec_seq,
                           mixer_common + (mix_weights(ls), bias_rows(ls), w_out_b), tm=256)
    xs, upv_s, upg_s = _ffn(xs, state_ffn_conv[0], dec_seq, ffn_params, tm=1024, tf=512, rc=256)
    ys = _ple(xs, p_sample[0].reshape(dec_batch * dec_seq, D_PLE), ple_params, 1024, "ple_sample")

    keep = FFN_CONV - 1
    conv_a_prompt = conv_p[:, HIST_A - (CONV_A - 1):]
    conv_a_sample = a_s
    per_seq = lastv_p.shape[0] // batch
    ffn_prompt = jnp.concatenate([lastv_p, lastg_p], axis=-1)[per_seq - 1::per_seq, SUBLANES - keep:]
    ffn_sample = jnp.concatenate([upv_s, upg_s], axis=-1)
    return (yp.reshape(batch, seq, D_MODEL), ys.reshape(dec_batch, dec_seq, D_MODEL),
            conv_a_prompt[None], conv_a_sample[None], ffn_prompt[None], ffn_sample[None],
            cv_p[None], cv_s.reshape(dec_batch, dec_seq, C_B)[None])
```

```python
import functools

import jax
import jax.numpy as jnp
from jax import lax
from jax.experimental import pallas as pl
from jax.experimental.pallas import tpu as pltpu

D_MODEL = 2048
HEAD_DIM = 128
C_A = D_MODEL // 2
C_B = D_MODEL - C_A
N_HEADS = C_B // HEAD_DIM
CONV_A = 31
HIST_A = 32
CONV_TILES = HIST_A // 8 + 1
CHUNK = 128
FFN_CONV = 3
D_FF = 5632
D_PLE = 256
EPS = 1e-6

SUBLANES = 8
ROWS = 32
LANES_EW = 128
FFN_K_SPLIT = 2
DOT_N = 512
MIXER_SLOTS = 2
PLE_SKEW = 2
PLE_SLOTS = 3
VMEM_LIMIT = 56 * 1024 * 1024

F32 = jnp.float32
BF16 = jnp.bfloat16


def _resident(shape):
    return pl.BlockSpec(shape, lambda *_: (0,) * len(shape), pipeline_mode=pl.Buffered(1))


def _panel_specs(k, n, index=None, resident=True):
    specs = []
    for g in range(n // DOT_N):
        imap = index(g) if index is not None else (lambda *_, g=g: (0, g))
        specs.append(pl.BlockSpec((k, DOT_N), imap,
                                  pipeline_mode=pl.Buffered(1) if resident else None))
    return specs


def _aligned(x, m):
    return x if isinstance(x, int) else pl.multiple_of(x, m)


def _row_loop(n_rows, step, fn):
    def body(c, carry):
        fn(pl.multiple_of(c * step, step))
        return carry
    lax.fori_loop(0, n_rows // step, body, 0, unroll=2)


def _rms_to(x_ref, g_ref, dst_ref, n_rows):
    def piece(r):
        xv = x_ref[pl.ds(r, ROWS), :]
        ms = jnp.mean(xv * xv, axis=-1, keepdims=True)
        dst_ref[pl.ds(r, ROWS), :] = (xv * lax.rsqrt(ms + EPS) * g_ref[...]).astype(BF16)
    _row_loop(n_rows, ROWS, piece)


def _head_ln(x, g, b):
    outs = []
    for h in range(x.shape[-1] // HEAD_DIM):
        xh = x[:, h * HEAD_DIM:(h + 1) * HEAD_DIM]
        mu = jnp.mean(xh, axis=-1, keepdims=True)
        xc = xh - mu
        var = jnp.mean(xc * xc, axis=-1, keepdims=True)
        outs.append(xc * lax.rsqrt(var + EPS))
    return jnp.concatenate(outs, axis=-1) * g + b


def _conv31(win3, n_out, coef_ref, lanes):
    acc = None
    for r in range(SUBLANES):
        inner = None
        for q in range(CONV_TILES):
            if q == 0 and r < HIST_A - (CONV_A - 1):
                continue
            k = (r * CONV_TILES + q) * SUBLANES
            term = coef_ref[k:k + SUBLANES, lanes] * win3[q:q + n_out]
            inner = term if inner is None else inner + term
        if r:
            inner = pltpu.roll(inner, SUBLANES - r, 1)
        acc = inner if acc is None else acc + inner
    return acc


def _mixer_body(sample, tm, seq_len, *refs):
    refs = list(refs)
    x_ref = refs.pop(0)
    st_ref = refs.pop(0) if sample else None
    gmix_ref = refs.pop(0)
    win_ref = [refs.pop(0) for _ in range((2 * C_A + 2 * C_B) // DOT_N)]
    (coef_ref, bdw_ref, glna_ref, blna_ref, glnv_ref, blnv_ref, ws_ref,
     bsb_ref) = [refs.pop(0) for _ in range(8)]
    wout_ref = [refs.pop(0) for _ in range(D_MODEL // DOT_N)]
    y_ref, aout_ref, vout_ref, h_ref, z_ref, cat_ref, a_ref, vn_ref, wm_ref = refs
    n_chunks = tm // CHUNK
    lane_groups = [slice(g * LANES_EW, (g + 1) * LANES_EW) for g in range(C_A // LANES_EW)]

    def norm(c):
        for r in range(c * CHUNK, (c + 1) * CHUNK, ROWS):
            xv = x_ref[r:r + ROWS, :]
            ms = jnp.mean(xv * xv, axis=-1, keepdims=True)
            h_ref[r:r + ROWS, :] = (xv * lax.rsqrt(ms + EPS) * gmix_ref[...]).astype(BF16)

    ti = lax.broadcasted_iota(jnp.int32, (CHUNK, CHUNK), 0)
    si = lax.broadcasted_iota(jnp.int32, (CHUNK, CHUNK), 1)
    if seq_len >= CHUNK:
        mask = si <= ti
    else:
        mask = ((si // seq_len) == (ti // seq_len)) & ((si % seq_len) <= (ti % seq_len))
    for h in range(N_HEADS):
        wm_ref[h] = jnp.where(mask, ws_ref[h], 0.0).astype(BF16)

    if not sample:
        t = pl.program_id(1)

        @pl.when(t == 0)
        def _():
            a_ref[0:HIST_A, :] = jnp.zeros((HIST_A, C_A), F32)

        @pl.when(t > 0)
        def _():
            a_ref[0:HIST_A, :] = a_ref[tm:tm + HIST_A, :]

    a_base = 0 if sample else HIST_A

    def slot_rows(c):
        lo = (c % MIXER_SLOTS) * CHUNK
        return slice(lo, lo + CHUNK)

    def in_proj(c):
        rows = slice(c * CHUNK, (c + 1) * CHUNK)
        hv = h_ref[rows, :]
        for g in range(len(win_ref)):
            z_ref[slot_rows(c), g * DOT_N:(g + 1) * DOT_N] = jnp.dot(
                hv, win_ref[g][...], preferred_element_type=F32)

    def out_proj(c):
        rows = slice(c * CHUNK, (c + 1) * CHUNK)
        cv = cat_ref[slot_rows(c), :]
        for g in range(len(wout_ref)):
            cols = slice(g * DOT_N, (g + 1) * DOT_N)
            y_ref[rows, cols] = x_ref[rows, cols] + jnp.dot(cv, wout_ref[g][...],
                                                            preferred_element_type=F32)

    def elementwise(c):
        r0 = c * CHUNK
        s0 = slot_rows(c).start - r0
        for r in range(r0, r0 + CHUNK, ROWS):
            for lanes in lane_groups:
                zv = z_ref[s0 + r:s0 + r + ROWS, lanes]
                zg = z_ref[s0 + r:s0 + r + ROWS, C_A + lanes.start:C_A + lanes.stop]
                a = zv * jax.nn.sigmoid(zg)
                a_ref[a_base + r:a_base + r + ROWS, lanes] = a
                if sample:
                    keep = CONV_A - 1 - seq_len
                    for q in range(ROWS // seq_len):
                        s = r // seq_len + q
                        aout_ref[s, 0:keep, lanes] = st_ref[s, HIST_A - keep:HIST_A, lanes]
                        aout_ref[s, keep:CONV_A - 1, lanes] = a[q * seq_len:(q + 1) * seq_len]
        for r in range(r0, r0 + CHUNK, ROWS):
            for lanes in lane_groups:
                width = lanes.stop - lanes.start
                if sample:
                    outs = []
                    for rs in range(r, r + ROWS, seq_len):
                        hist = st_ref[rs // seq_len, :, lanes].reshape(
                            CONV_TILES - 1, SUBLANES, width)
                        new = a_ref[rs:rs + seq_len, lanes].reshape(1, SUBLANES, width)
                        win3 = jnp.concatenate([hist, new], axis=0)
                        outs.append(_conv31(win3, 1, coef_ref, lanes))
                    conv = jnp.concatenate(outs, axis=0).reshape(ROWS, width)
                else:
                    n_out = ROWS // SUBLANES
                    win3 = a_ref[r:r + ROWS + HIST_A, lanes].reshape(
                        n_out + CONV_TILES - 1, SUBLANES, width)
                    conv = _conv31(win3, n_out, coef_ref, lanes).reshape(ROWS, width)
                conv = conv + bdw_ref[:, lanes]
                yv = _head_ln(conv, glna_ref[:, lanes], blna_ref[:, lanes])
                cat_ref[s0 + r:s0 + r + ROWS, lanes] = (yv * jax.nn.sigmoid(yv)).astype(BF16)

        v_off = 2 * C_A + C_B
        for r in range(r0, r0 + CHUNK, ROWS):
            for lanes in lane_groups:
                v = jax.nn.gelu(z_ref[s0 + r:s0 + r + ROWS, v_off + lanes.start:v_off + lanes.stop])
                vn = _head_ln(v, glnv_ref[:, lanes], blnv_ref[:, lanes])
                vn_ref[r - r0:r - r0 + ROWS, lanes] = vn.astype(BF16)
                if sample:
                    vout_ref[r:r + ROWS, lanes] = vn
                else:
                    vout_ref[0, r - r0:r - r0 + ROWS, lanes] = vn
        for h in range(N_HEADS):
            lo = h * HEAD_DIM
            mixed = jnp.dot(wm_ref[h], vn_ref[:, lo:lo + HEAD_DIM], preferred_element_type=F32)
            mixed = mixed + bsb_ref[:, lo:lo + HEAD_DIM]
            u = jax.nn.gelu(z_ref[slot_rows(c), 2 * C_A + lo:2 * C_A + lo + HEAD_DIM])
            cat_ref[slot_rows(c), C_A + lo:C_A + lo + HEAD_DIM] = (u * mixed).astype(BF16)

    norm(0)
    in_proj(0)
    for c in range(n_chunks):
        if c + 1 < n_chunks:
            norm(c + 1)
            in_proj(c + 1)
        elementwise(c)
        out_proj(c)

    if not sample:
        aout_ref[0] = a_ref[tm:tm + HIST_A, :]


def _mixer(x, state, seq_len, params, tm):
    (g_mix, w_in, w_dw, b_dw, g_ln_a, b_ln_a, g_ln_v, b_ln_v, w_s, bias_rows, w_out) = params
    rows = x.shape[0]
    sample = state is not None
    vec = lambda n: _resident((1, n))
    common_in = ([vec(D_MODEL)] + _panel_specs(D_MODEL, 2 * C_A + 2 * C_B)
                 + [_resident((SUBLANES * CONV_TILES * SUBLANES, C_A)),
                    vec(C_A), vec(C_A), vec(C_A), vec(C_B), vec(C_B),
                    _resident((N_HEADS, CHUNK, CHUNK)), _resident((CHUNK, C_B))]
                 + _panel_specs(C_A + C_B, D_MODEL))
    ring = MIXER_SLOTS * CHUNK
    common_scratch = [pltpu.VMEM((tm, D_MODEL), BF16),
                      pltpu.VMEM((ring, 2 * C_A + 2 * C_B), F32),
                      pltpu.VMEM((ring, C_A + C_B), BF16)]
    tail_scratch = [pltpu.VMEM((CHUNK, C_B), BF16),
                    pltpu.VMEM((N_HEADS, CHUNK, CHUNK), BF16)]
    if sample:
        n_seq = rows // seq_len
        spt = tm // seq_len
        grid = (rows // tm,)
        row_map = lambda i: (i, 0)
        in_specs = [pl.BlockSpec((tm, D_MODEL), row_map),
                    pl.BlockSpec((spt, HIST_A, C_A), lambda i: (i, 0, 0),
                                 pipeline_mode=pl.Buffered(1))] + common_in
        out_specs = [pl.BlockSpec((tm, D_MODEL), row_map),
                     pl.BlockSpec((spt, CONV_A - 1, C_A), lambda i: (i, 0, 0)),
                     pl.BlockSpec((tm, C_B), row_map)]
        out_shape = [jax.ShapeDtypeStruct((rows, D_MODEL), F32),
                     jax.ShapeDtypeStruct((n_seq, CONV_A - 1, C_A), F32),
                     jax.ShapeDtypeStruct((rows, C_B), F32)]
        scratch = common_scratch + [pltpu.VMEM((tm, C_A), F32)] + tail_scratch
        args = (x, state)
        sem = ("arbitrary",)
        del n_seq
    else:
        n_seq = rows // seq_len
        nt = seq_len // tm
        grid = (n_seq, nt)
        row_map = lambda b, t: (b * nt + t, 0)
        in_specs = [pl.BlockSpec((tm, D_MODEL), row_map)] + common_in
        out_specs = [pl.BlockSpec((tm, D_MODEL), row_map),
                     pl.BlockSpec((1, HIST_A, C_A), lambda b, t: (b, 0, 0)),
                     pl.BlockSpec((1, CHUNK, C_B), lambda b, t: (b, 0, 0))]
        out_shape = [jax.ShapeDtypeStruct((rows, D_MODEL), F32),
                     jax.ShapeDtypeStruct((n_seq, HIST_A, C_A), F32),
                     jax.ShapeDtypeStruct((n_seq, CHUNK, C_B), F32)]
        scratch = common_scratch + [pltpu.VMEM((tm + HIST_A, C_A), F32)] + tail_scratch
        args = (x,)
        sem = ("arbitrary", "arbitrary")
    return pl.pallas_call(
        functools.partial(_mixer_body, sample, tm, seq_len),
        grid=grid, in_specs=in_specs, out_specs=out_specs, out_shape=out_shape,
        scratch_shapes=scratch,
        compiler_params=pltpu.CompilerParams(dimension_semantics=sem,
                                             vmem_limit_bytes=VMEM_LIMIT),
        name="mixer_sample" if sample else "mixer_prompt",
    )(*args, g_mix, *[w_in] * (w_in.shape[1] // DOT_N), w_dw, b_dw, g_ln_a, b_ln_a, g_ln_v,
      b_ln_v, w_s, bias_rows, *[w_out] * (w_out.shape[1] // DOT_N))


def _ffn_body(sample, tm, tf, rc, blocks_per_seq, *refs):
    refs = list(refs)
    x_ref = refs.pop(0)
    stv_ref, stg_ref = (refs.pop(0), refs.pop(0)) if sample else (None, None)
    (g_ref, wuv_ref, wug_ref, wdv_ref, wdg_ref, bdv_ref,
     bdg_ref) = [refs.pop(0) for _ in range(7)]
    wdn_ref = [refs.pop(0) for _ in range(D_MODEL // DOT_N)]
    y_ref, lastv_ref, lastg_ref, h_ref, upv_ref, upg_ref, gate_ref = refs[:7]
    carry_ref = None if sample else refs[7]
    i = pl.program_id(0)
    j = pl.program_id(1)

    @pl.when(j == 0)
    def _():
        _rms_to(x_ref, g_ref, h_ref, tm)
        y_ref[...] = x_ref[...]
        if not sample:
            @pl.when(i == 0)
            def _():
                carry_ref[...] = jnp.zeros(carry_ref.shape, F32)

    hist = SUBLANES
    starts = list(range(0, tm - rc, rc)) + [tm - rc, tm - rc // 2]
    chunks = [(r0, r1 - r0) for r0, r1 in zip(starts, starts[1:] + [tm])]
    ups = ((upv_ref, wuv_ref, lastv_ref), (upg_ref, wug_ref, lastg_ref))

    if not sample:
        first = (i % blocks_per_seq) == 0
        for half, (up_ref, _, _) in enumerate(ups):
            up_ref[0:hist, :] = jnp.where(first, 0.0, carry_ref[j, half])

    def up_proj(c):
        r0, n = chunks[c]
        hc = h_ref[r0:r0 + n, :]
        for up_ref, wu_ref, _ in ups:
            up_ref[hist + r0:hist + r0 + n, :] = jnp.dot(hc, wu_ref[...],
                                                         preferred_element_type=F32)

    def conv(up_ref, st_ref, wd_ref, bd_ref, row, lanes):
        width = lanes.stop - lanes.start
        w0 = wd_ref[0:1, lanes]
        w1 = wd_ref[1:2, lanes]
        w2 = wd_ref[2:3, lanes]
        n = ROWS // SUBLANES
        if sample:
            s0 = row // SUBLANES
            cur3 = up_ref[hist + row:hist + row + ROWS, lanes].reshape(n, SUBLANES, width)
            sl = lax.broadcasted_iota(jnp.int32, (n, SUBLANES, width), 1)
            st0 = st_ref[s0:s0 + n, 0:1, lanes]
            st1 = st_ref[s0:s0 + n, 1:2, lanes]
            m1 = jnp.where(sl == 0, st1, pltpu.roll(cur3, 1, 1))
            m2 = jnp.where(sl == 0, st0, jnp.where(sl == 1, st1, pltpu.roll(cur3, 2, 1)))
            out = w0 * m2 + w1 * m1 + w2 * cur3 + bd_ref[:, lanes]
            return out.reshape(ROWS, width)
        win3 = up_ref[row:row + ROWS + hist, lanes].reshape(n + 1, SUBLANES, width)
        top = lax.broadcasted_iota(jnp.int32, (n, SUBLANES, width), 1) == 0
        r0 = pltpu.roll(w0 * win3, 1, 1)
        s0 = jnp.concatenate([r0[0:1], jnp.where(top, r0[0:n], r0[1:n + 1])], axis=0)
        r1 = pltpu.roll(w1 * win3 + s0, 1, 1)
        s1 = jnp.where(top, r1[0:n], r1[1:n + 1])
        return (w2 * win3[1:n + 1] + s1 + bd_ref[:, lanes]).reshape(ROWS, width)

    def gate_down(c, kh):
        r0, n = chunks[c]
        kw = tf // FFN_K_SPLIT
        lanes = slice(kh * kw, (kh + 1) * kw)
        for r in range(0, n, ROWS):
            cv = conv(upv_ref, stv_ref if sample else None, wdv_ref, bdv_ref, r0 + r, lanes)
            cg = conv(upg_ref, stg_ref if sample else None, wdg_ref, bdg_ref, r0 + r, lanes)
            gate_ref[r:r + ROWS, lanes] = (cv * jax.nn.sigmoid(cv) * cg).astype(BF16)
        gv = gate_ref[0:n, lanes]
        for g in range(len(wdn_ref)):
            cols = slice(g * DOT_N, (g + 1) * DOT_N)
            y_ref[r0:r0 + n, cols] += jnp.dot(gv, wdn_ref[g][lanes, :],
                                              preferred_element_type=F32)

    up_proj(0)
    for c in range(len(chunks)):
        if c + 1 < len(chunks):
            up_proj(c + 1)
        for kh in range(FFN_K_SPLIT):
            gate_down(c, kh)

    for half, (up_ref, _, last_ref) in enumerate(ups):
        if sample:
            up3 = up_ref[hist:hist + tm, :].reshape(tm // SUBLANES, SUBLANES, tf)
            last_ref[...] = up3[:, SUBLANES - (FFN_CONV - 1):, :]
        else:
            tail = up_ref[tm:tm + hist, :]
            carry_ref[j, half] = tail
            last_ref[0] = tail


def _ffn(x, state, seq_len, params, tm, tf, rc):
    g_ffn, w_up, w_dw, b_dw, w_down = params
    rows = x.shape[0]
    sample = state is not None
    nj = D_FF // tf
    grid = (rows // tm, nj)
    row_map = lambda i, j: (i, 0)
    val_map = lambda i, j: (0, j)
    gate_map = lambda i, j: (0, nj + j)
    in_specs = [pl.BlockSpec((tm, D_MODEL), row_map)]
    args = [x]
    if sample:
        spt = tm // seq_len
        in_specs += [pl.BlockSpec((spt, FFN_CONV - 1, tf), lambda i, j: (i, 0, j)),
                     pl.BlockSpec((spt, FFN_CONV - 1, tf), lambda i, j: (i, 0, nj + j))]
        args += [state, state]
    in_specs += [pl.BlockSpec((1, D_MODEL), lambda i, j: (0, 0)),
                 pl.BlockSpec((D_MODEL, tf), val_map), pl.BlockSpec((D_MODEL, tf), gate_map),
                 pl.BlockSpec((FFN_CONV, tf), val_map), pl.BlockSpec((FFN_CONV, tf), gate_map),
                 pl.BlockSpec((1, tf), val_map), pl.BlockSpec((1, tf), gate_map)]
    in_specs += _panel_specs(tf, D_MODEL, index=lambda g: (lambda i, j: (j, g)), resident=False)
    args += [g_ffn, w_up, w_up, w_dw, w_dw, b_dw, b_dw] + [w_down] * (D_MODEL // DOT_N)
    scratch = [pltpu.VMEM((tm, D_MODEL), BF16),
               pltpu.VMEM((tm + SUBLANES, tf), F32), pltpu.VMEM((tm + SUBLANES, tf), F32),
               pltpu.VMEM((rc, tf), BF16)]
    if sample:
        last_specs = [pl.BlockSpec((spt, FFN_CONV - 1, tf), lambda i, j: (i, 0, j))] * 2
        last_shape = [jax.ShapeDtypeStruct((rows // seq_len, FFN_CONV - 1, D_FF), F32)] * 2
        blocks_per_seq = 1
    else:
        blocks_per_seq = seq_len // tm
        last_specs = [pl.BlockSpec((1, SUBLANES, tf), lambda i, j: (i, 0, j))] * 2
        last_shape = [jax.ShapeDtypeStruct((rows // tm, SUBLANES, D_FF), F32)] * 2
        scratch += [pltpu.VMEM((nj, 2, SUBLANES, tf), F32)]
    return pl.pallas_call(
        functools.partial(_ffn_body, sample, tm, tf, rc, blocks_per_seq),
        grid=grid, in_specs=in_specs,
        out_specs=[pl.BlockSpec((tm, D_MODEL), row_map)] + last_specs,
        out_shape=[jax.ShapeDtypeStruct((rows, D_MODEL), F32)] + last_shape,
        scratch_shapes=scratch,
        compiler_params=pltpu.CompilerParams(dimension_semantics=("arbitrary", "arbitrary"),
                                             vmem_limit_bytes=VMEM_LIMIT),
        name="ffn_sample" if sample else "ffn_prompt",
    )(*args)


def _ple_body(tm, *refs):
    refs = list(refs)
    x_ref, p_ref, g_ref = refs[:3]
    n_panels = D_MODEL // DOT_N
    wg_ref = refs[3:3 + n_panels]
    wp_ref = refs[3 + n_panels:3 + 2 * n_panels]
    gf_ref, y_ref, h_ref, gate_ref, proj_ref = refs[3 + 2 * n_panels:]
    _ple_pipeline(tm, x_ref, p_ref, g_ref, wg_ref, wp_ref, gf_ref, y_ref, h_ref, gate_ref, proj_ref)


def _ple_pipeline(tm, x_ref, p_ref, g_ref, wg_ref, wp_ref, gf_ref, y_ref, h_ref, gate_ref, proj_ref):
    n_chunks = tm // CHUNK
    never = pl.program_id(0) < 0
    pieces = CHUNK // ROWS
    anchors = {}

    def norm(c):
        slot = (c % PLE_SLOTS) * CHUNK
        for k in range(pieces):
            r = c * CHUNK + k * ROWS
            xv = x_ref[r:r + ROWS, :]
            ms = jnp.mean(xv * xv, axis=-1, keepdims=True)
            hv = xv * lax.rsqrt(ms + EPS) * g_ref[...]
            anchor = anchors.pop((c - PLE_SKEW, k), None)
            if anchor is not None:
                head = jnp.where(never, anchor, hv[:, :HEAD_DIM])
                hv = jnp.concatenate([head, hv[:, HEAD_DIM:]], axis=-1)
            h_ref[slot + k * ROWS:slot + (k + 1) * ROWS, :] = hv.astype(BF16)

    def projections(c):
        slot = (c % PLE_SLOTS) * CHUNK
        rows = slice(c * CHUNK, (c + 1) * CHUNK)
        pv = p_ref[rows, :].astype(BF16)
        for g in range(D_MODEL // DOT_N):
            cols = slice(g * DOT_N, (g + 1) * DOT_N)
            gate_ref[slot:slot + CHUNK, cols] = jnp.dot(
                h_ref[slot:slot + CHUNK, :], wg_ref[g][...], preferred_element_type=F32)
            proj_ref[slot:slot + CHUNK, cols] = jnp.dot(pv, wp_ref[g][...],
                                                        preferred_element_type=F32)

    def finish(c):
        slot = (c % PLE_SLOTS) * CHUNK
        for k in range(pieces):
            r = c * CHUNK + k * ROWS
            s = slot + k * ROWS
            xv = x_ref[r:r + ROWS, :] + jax.nn.sigmoid(gate_ref[s:s + ROWS, :]) * proj_ref[s:s + ROWS, :]
            ms = jnp.mean(xv * xv, axis=-1, keepdims=True)
            yv = xv * lax.rsqrt(ms + EPS) * gf_ref[...]
            y_ref[r:r + ROWS, :] = yv
            anchors[(c, k)] = yv[:, :HEAD_DIM]

    norm(0)
    for c in range(n_chunks):
        projections(c)
        if c >= 1:
            finish(c - 1)
        if c + 1 < n_chunks:
            norm(c + 1)
    finish(n_chunks - 1)


def _ple(x, p, params, tm, name):
    g_ple, w_gate, w_proj, g_final = params
    rows = x.shape[0]
    row_map = lambda i: (i, 0)
    return pl.pallas_call(
        functools.partial(_ple_body, tm),
        grid=(rows // tm,),
        in_specs=([pl.BlockSpec((tm, D_MODEL), row_map), pl.BlockSpec((tm, D_PLE), row_map),
                   _resident((1, D_MODEL))] + _panel_specs(D_MODEL, D_MODEL)
                  + _panel_specs(D_PLE, D_MODEL) + [_resident((1, D_MODEL))]),
        out_specs=pl.BlockSpec((tm, D_MODEL), row_map),
        out_shape=jax.ShapeDtypeStruct((rows, D_MODEL), F32),
        scratch_shapes=[pltpu.VMEM((PLE_SLOTS * CHUNK, D_MODEL), BF16),
                        pltpu.VMEM((PLE_SLOTS * CHUNK, D_MODEL), F32),
                        pltpu.VMEM((PLE_SLOTS * CHUNK, D_MODEL), F32)],
        compiler_params=pltpu.CompilerParams(dimension_semantics=("arbitrary",),
                                             vmem_limit_bytes=VMEM_LIMIT),
        name=name,
    )(x, p, g_ple, *[w_gate] * (D_MODEL // DOT_N), *[w_proj] * (D_MODEL // DOT_N), g_final)


def _conv_coef(w):
    c = w.shape[1]
    off = HIST_A - (CONV_A - 1)
    n = SUBLANES * (CONV_TILES + 1)
    wp = jnp.zeros((n, c), w.dtype).at[off:off + CONV_A].set(w)
    wp_prev = jnp.concatenate([jnp.zeros((SUBLANES, c), w.dtype), wp[:n - SUBLANES]], axis=0)
    r = jnp.arange(SUBLANES)[:, None]
    q = jnp.arange(CONV_TILES)[None, :]
    d = SUBLANES * q + r
    s = jnp.arange(SUBLANES)[None, None, :, None]
    coef = jnp.where(s >= r[:, :, None, None], wp[d][:, :, None, :], wp_prev[d][:, :, None, :])
    return coef.reshape(SUBLANES * CONV_TILES * SUBLANES, c)


def kernel(x_prompt, x_sample, p_prompt, p_sample, state_conv_a, state_ffn_conv, g_mix, w_in, w_dw_a, b_dw_a, g_ln_a, b_ln_a, g_ln_v, b_ln_v, w_s, b_s, w_out, g_ffn, w_up, w_dw_f, b_dw_f, w_down, g_ple, w_ple_gate, w_ple_proj, g_final):
    depth = w_in.shape[0]
    assert depth == 1, "single-layer step"
    batch, seq, _ = x_prompt.shape
    dec_batch, dec_seq, _ = x_sample.shape
    row = lambda v: v.reshape(1, -1)

    def bias_rows(length):
        b = jnp.tile(b_s[0][:, :length], (1, CHUNK // length))
        return jnp.repeat(b.T, HEAD_DIM, axis=1)

    def mix_weights(length):
        if length == CHUNK:
            return w_s[0]
        sel = (jnp.arange(CHUNK)[:, None] % length == jnp.arange(length)[None, :]).astype(F32)
        return jnp.einsum('ti,hij,sj->hts', sel, w_s[0][:, :length, :length], sel,
                          precision=lax.Precision.HIGHEST)

    mixer_common = (row(g_mix[0]), w_in[0].astype(BF16), _conv_coef(w_dw_a[0]), row(b_dw_a[0]),
                    row(g_ln_a[0]), row(b_ln_a[0]), row(g_ln_v[0]), row(b_ln_v[0]))
    w_out_b = w_out[0].astype(BF16)
    ffn_params = (row(g_ffn[0]), w_up[0].astype(BF16), w_dw_f[0], row(b_dw_f[0]),
                  w_down[0].astype(BF16))
    ple_params = (row(g_ple[0]), w_ple_gate[0].astype(BF16), w_ple_proj[0].astype(BF16),
                  row(g_final))

    xp = x_prompt.reshape(batch * seq, D_MODEL)
    lp = min(seq, CHUNK)
    xp, conv_p, cv_p = _mixer(xp, None, seq,
                              mixer_common + (mix_weights(lp), bias_rows(lp), w_out_b), tm=256)
    xp, lastv_p, lastg_p = _ffn(xp, None, seq, ffn_params, tm=1024, tf=512, rc=256)
    yp = _ple(xp, p_prompt[0].reshape(batch * seq, D_PLE), ple_params, 1024, "ple_prompt")

    xs = x_sample.reshape(dec_batch * dec_seq, D_MODEL)
    ls = min(dec_seq, CHUNK)
    state_pad = jnp.pad(state_conv_a[0], ((0, 0), (HIST_A - (CONV_A - 1), 0), (0, 0)))
    xs, a_s, cv_s = _mixer(xs, state_pad, dec_seq,
                           mixer_common + (mix_weights(ls), bias_rows(ls), w_out_b), tm=256)
    xs, upv_s, upg_s = _ffn(xs, state_ffn_conv[0], dec_seq, ffn_params, tm=1024, tf=512, rc=256)
    ys = _ple(xs, p_sample[0].reshape(dec_batch * dec_seq, D_PLE), ple_params, 1024, "ple_sample")

    keep = FFN_CONV - 1
    conv_a_prompt = conv_p[:, HIST_A - (CONV_A - 1):]
    conv_a_sample = a_s
    per_seq = lastv_p.shape[0] // batch
    ffn_prompt = jnp.concatenate([lastv_p, lastg_p], axis=-1)[per_seq - 1::per_seq, SUBLANES - keep:]
    ffn_sample = jnp.concatenate([upv_s, upg_s], axis=-1)
    return (yp.reshape(batch, seq, D_MODEL), ys.reshape(dec_batch, dec_seq, D_MODEL),
            conv_a_prompt[None], conv_a_sample[None], ffn_prompt[None], ffn_sample[None],
            cv_p[None], cv_s.reshape(dec_batch, dec_seq, C_B)[None])
```

```python
import functools

import jax
import jax.numpy as jnp
from jax import lax
from jax.experimental import pallas as pl
from jax.experimental.pallas import tpu as pltpu

D_MODEL = 2048
HEAD_DIM = 128
C_A = D_MODEL // 2
C_B = D_MODEL - C_A
N_HEADS = C_B // HEAD_DIM
CONV_A = 31
HIST_A = 32
CONV_TILES = HIST_A // 8 + 1
CHUNK = 128
FFN_CONV = 3
D_FF = 5632
D_PLE = 256
EPS = 1e-6

SUBLANES = 8
ROWS = 32
LANES_EW = 128
FFN_K_SPLIT = 2
DOT_N = 512
MIXER_SLOTS = 2
PLE_SKEW = 2
PLE_SLOTS = 3
VMEM_LIMIT = 56 * 1024 * 1024

F32 = jnp.float32
BF16 = jnp.bfloat16


def _resident(shape):
    return pl.BlockSpec(shape, lambda *_: (0,) * len(shape), pipeline_mode=pl.Buffered(1))


def _panel_specs(k, n, index=None, resident=True):
    specs = []
    for g in range(n // DOT_N):
        imap = index(g) if index is not None else (lambda *_, g=g: (0, g))
        specs.append(pl.BlockSpec((k, DOT_N), imap,
                                  pipeline_mode=pl.Buffered(1) if resident else None))
    return specs


def _aligned(x, m):
    return x if isinstance(x, int) else pl.multiple_of(x, m)


def _row_loop(n_rows, step, fn):
    def body(c, carry):
        fn(pl.multiple_of(c * step, step))
        return carry
    lax.fori_loop(0, n_rows // step, body, 0, unroll=4)


def _rms_to(x_ref, g_ref, dst_ref, n_rows):
    def piece(r):
        xv = x_ref[pl.ds(r, ROWS), :]
        ms = jnp.mean(xv * xv, axis=-1, keepdims=True)
        dst_ref[pl.ds(r, ROWS), :] = (xv * lax.rsqrt(ms + EPS) * g_ref[...]).astype(BF16)
    _row_loop(n_rows, ROWS, piece)


def _head_ln(x, g, b):
    outs = []
    for h in range(x.shape[-1] // HEAD_DIM):
        xh = x[:, h * HEAD_DIM:(h + 1) * HEAD_DIM]
        mu = jnp.mean(xh, axis=-1, keepdims=True)
        xc = xh - mu
        var = jnp.mean(xc * xc, axis=-1, keepdims=True)
        outs.append(xc * lax.rsqrt(var + EPS))
    return jnp.concatenate(outs, axis=-1) * g + b


def _conv31(win3, n_out, coef_ref, lanes):
    acc = None
    for r in range(SUBLANES):
        inner = None
        for q in range(CONV_TILES):
            if q == 0 and r < HIST_A - (CONV_A - 1):
                continue
            k = (r * CONV_TILES + q) * SUBLANES
            term = coef_ref[k:k + SUBLANES, lanes] * win3[q:q + n_out]
            inner = term if inner is None else inner + term
        if r:
            inner = pltpu.roll(inner, SUBLANES - r, 1)
        acc = inner if acc is None else acc + inner
    return acc


def _mixer_body(sample, tm, seq_len, *refs):
    refs = list(refs)
    x_ref = refs.pop(0)
    st_ref = refs.pop(0) if sample else None
    gmix_ref = refs.pop(0)
    win_ref = [refs.pop(0) for _ in range((2 * C_A + 2 * C_B) // DOT_N)]
    (coef_ref, bdw_ref, glna_ref, blna_ref, glnv_ref, blnv_ref, ws_ref,
     bsb_ref) = [refs.pop(0) for _ in range(8)]
    wout_ref = [refs.pop(0) for _ in range(D_MODEL // DOT_N)]
    y_ref, aout_ref, vout_ref, h_ref, z_ref, cat_ref, a_ref, vn_ref, wm_ref = refs
    n_chunks = tm // CHUNK
    lane_groups = [slice(g * LANES_EW, (g + 1) * LANES_EW) for g in range(C_A // LANES_EW)]

    def norm(c):
        for r in range(c * CHUNK, (c + 1) * CHUNK, ROWS):
            xv = x_ref[r:r + ROWS, :]
            ms = jnp.mean(xv * xv, axis=-1, keepdims=True)
            h_ref[r:r + ROWS, :] = (xv * lax.rsqrt(ms + EPS) * gmix_ref[...]).astype(BF16)

    ti = lax.broadcasted_iota(jnp.int32, (CHUNK, CHUNK), 0)
    si = lax.broadcasted_iota(jnp.int32, (CHUNK, CHUNK), 1)
    if seq_len >= CHUNK:
        mask = si <= ti
    else:
        mask = ((si // seq_len) == (ti // seq_len)) & ((si % seq_len) <= (ti % seq_len))
    for h in range(N_HEADS):
        wm_ref[h] = jnp.where(mask, ws_ref[h], 0.0).astype(BF16)

    if not sample:
        t = pl.program_id(1)

        @pl.when(t == 0)
        def _():
            a_ref[0:HIST_A, :] = jnp.zeros((HIST_A, C_A), F32)

        @pl.when(t > 0)
        def _():
            a_ref[0:HIST_A, :] = a_ref[tm:tm + HIST_A, :]

    a_base = 0 if sample else HIST_A

    def slot_rows(c):
        lo = (c % MIXER_SLOTS) * CHUNK
        return slice(lo, lo + CHUNK)

    def in_proj(c):
        rows = slice(c * CHUNK, (c + 1) * CHUNK)
        hv = h_ref[rows, :]
        for g in range(len(win_ref)):
            z_ref[slot_rows(c), g * DOT_N:(g + 1) * DOT_N] = jnp.dot(
                hv, win_ref[g][...], preferred_element_type=F32)

    def out_proj(c):
        rows = slice(c * CHUNK, (c + 1) * CHUNK)
        cv = cat_ref[slot_rows(c), :]
        for g in range(len(wout_ref)):
            cols = slice(g * DOT_N, (g + 1) * DOT_N)
            y_ref[rows, cols] = x_ref[rows, cols] + jnp.dot(cv, wout_ref[g][...],
                                                            preferred_element_type=F32)

    def elementwise(c):
        r0 = c * CHUNK
        s0 = slot_rows(c).start - r0
        for r in range(r0, r0 + CHUNK, ROWS):
            for lanes in lane_groups:
                zv = z_ref[s0 + r:s0 + r + ROWS, lanes]
                zg = z_ref[s0 + r:s0 + r + ROWS, C_A + lanes.start:C_A + lanes.stop]
                a = zv * jax.nn.sigmoid(zg)
                a_ref[a_base + r:a_base + r + ROWS, lanes] = a
                if sample:
                    keep = CONV_A - 1 - seq_len
                    for q in range(ROWS // seq_len):
                        s = r // seq_len + q
                        aout_ref[s, 0:keep, lanes] = st_ref[s, HIST_A - keep:HIST_A, lanes]
                        aout_ref[s, keep:CONV_A - 1, lanes] = a[q * seq_len:(q + 1) * seq_len]
        for r in range(r0, r0 + CHUNK, ROWS):
            for lanes in lane_groups:
                width = lanes.stop - lanes.start
                if sample:
                    outs = []
                    for rs in range(r, r + ROWS, seq_len):
                        hist = st_ref[rs // seq_len, :, lanes].reshape(
                            CONV_TILES - 1, SUBLANES, width)
                        new = a_ref[rs:rs + seq_len, lanes].reshape(1, SUBLANES, width)
                        win3 = jnp.concatenate([hist, new], axis=0)
                        outs.append(_conv31(win3, 1, coef_ref, lanes))
                    conv = jnp.concatenate(outs, axis=0).reshape(ROWS, width)
                else:
                    n_out = ROWS // SUBLANES
                    win3 = a_ref[r:r + ROWS + HIST_A, lanes].reshape(
                        n_out + CONV_TILES - 1, SUBLANES, width)
                    conv = _conv31(win3, n_out, coef_ref, lanes).reshape(ROWS, width)
                conv = conv + bdw_ref[:, lanes]
                yv = _head_ln(conv, glna_ref[:, lanes], blna_ref[:, lanes])
                cat_ref[s0 + r:s0 + r + ROWS, lanes] = (yv * jax.nn.sigmoid(yv)).astype(BF16)

        v_off = 2 * C_A + C_B
        for r in range(r0, r0 + CHUNK, ROWS):
            for lanes in lane_groups:
                v = jax.nn.gelu(z_ref[s0 + r:s0 + r + ROWS, v_off + lanes.start:v_off + lanes.stop])
                vn = _head_ln(v, glnv_ref[:, lanes], blnv_ref[:, lanes])
                vn_ref[r - r0:r - r0 + ROWS, lanes] = vn.astype(BF16)
                if sample:
                    vout_ref[r:r + ROWS, lanes] = vn
                else:
                    vout_ref[0, r - r0:r - r0 + ROWS, lanes] = vn
        for h in range(N_HEADS):
            lo = h * HEAD_DIM
            mixed = jnp.dot(wm_ref[h], vn_ref[:, lo:lo + HEAD_DIM], preferred_element_type=F32)
            mixed = mixed + bsb_ref[:, lo:lo + HEAD_DIM]
            u = jax.nn.gelu(z_ref[slot_rows(c), 2 * C_A + lo:2 * C_A + lo + HEAD_DIM])
            cat_ref[slot_rows(c), C_A + lo:C_A + lo + HEAD_DIM] = (u * mixed).astype(BF16)

    norm(0)
    in_proj(0)
    for c in range(n_chunks):
        if c + 1 < n_chunks:
            norm(c + 1)
            in_proj(c + 1)
        elementwise(c)
        out_proj(c)

    if not sample:
        aout_ref[0] = a_ref[tm:tm + HIST_A, :]


def _mixer(x, state, seq_len, params, tm):
    (g_mix, w_in, w_dw, b_dw, g_ln_a, b_ln_a, g_ln_v, b_ln_v, w_s, bias_rows, w_out) = params
    rows = x.shape[0]
    sample = state is not None
    vec = lambda n: _resident((1, n))
    common_in = ([vec(D_MODEL)] + _panel_specs(D_MODEL, 2 * C_A + 2 * C_B)
                 + [_resident((SUBLANES * CONV_TILES * SUBLANES, C_A)),
                    vec(C_A), vec(C_A), vec(C_A), vec(C_B), vec(C_B),
                    _resident((N_HEADS, CHUNK, CHUNK)), _resident((CHUNK, C_B))]
                 + _panel_specs(C_A + C_B, D_MODEL))
    ring = MIXER_SLOTS * CHUNK
    common_scratch = [pltpu.VMEM((tm, D_MODEL), BF16),
                      pltpu.VMEM((ring, 2 * C_A + 2 * C_B), F32),
                      pltpu.VMEM((ring, C_A + C_B), BF16)]
    tail_scratch = [pltpu.VMEM((CHUNK, C_B), BF16),
                    pltpu.VMEM((N_HEADS, CHUNK, CHUNK), BF16)]
    if sample:
        n_seq = rows // seq_len
        spt = tm // seq_len
        grid = (rows // tm,)
        row_map = lambda i: (i, 0)
        in_specs = [pl.BlockSpec((tm, D_MODEL), row_map),
                    pl.BlockSpec((spt, HIST_A, C_A), lambda i: (i, 0, 0),
                                 pipeline_mode=pl.Buffered(1))] + common_in
        out_specs = [pl.BlockSpec((tm, D_MODEL), row_map),
                     pl.BlockSpec((spt, CONV_A - 1, C_A), lambda i: (i, 0, 0)),
                     pl.BlockSpec((tm, C_B), row_map)]
        out_shape = [jax.ShapeDtypeStruct((rows, D_MODEL), F32),
                     jax.ShapeDtypeStruct((n_seq, CONV_A - 1, C_A), F32),
                     jax.ShapeDtypeStruct((rows, C_B), F32)]
        scratch = common_scratch + [pltpu.VMEM((tm, C_A), F32)] + tail_scratch
        args = (x, state)
        sem = ("arbitrary",)
        del n_seq
    else:
        n_seq = rows // seq_len
        nt = seq_len // tm
        grid = (n_seq, nt)
        row_map = lambda b, t: (b * nt + t, 0)
        in_specs = [pl.BlockSpec((tm, D_MODEL), row_map)] + common_in
        out_specs = [pl.BlockSpec((tm, D_MODEL), row_map),
                     pl.BlockSpec((1, HIST_A, C_A), lambda b, t: (b, 0, 0)),
                     pl.BlockSpec((1, CHUNK, C_B), lambda b, t: (b, 0, 0))]
        out_shape = [jax.ShapeDtypeStruct((rows, D_MODEL), F32),
                     jax.ShapeDtypeStruct((n_seq, HIST_A, C_A), F32),
                     jax.ShapeDtypeStruct((n_seq, CHUNK, C_B), F32)]
        scratch = common_scratch + [pltpu.VMEM((tm + HIST_A, C_A), F32)] + tail_scratch
        args = (x,)
        sem = ("arbitrary", "arbitrary")
    return pl.pallas_call(
        functools.partial(_mixer_body, sample, tm, seq_len),
        grid=grid, in_specs=in_specs, out_specs=out_specs, out_shape=out_shape,
        scratch_shapes=scratch,
        compiler_params=pltpu.CompilerParams(dimension_semantics=sem,
                                             vmem_limit_bytes=VMEM_LIMIT),
        name="mixer_sample" if sample else "mixer_prompt",
    )(*args, g_mix, *[w_in] * (w_in.shape[1] // DOT_N), w_dw, b_dw, g_ln_a, b_ln_a, g_ln_v,
      b_ln_v, w_s, bias_rows, *[w_out] * (w_out.shape[1] // DOT_N))


def _ffn_body(sample, tm, tf, rc, blocks_per_seq, *refs):
    refs = list(refs)
    x_ref = refs.pop(0)
    stv_ref, stg_ref = (refs.pop(0), refs.pop(0)) if sample else (None, None)
    (g_ref, wuv_ref, wug_ref, wdv_ref, wdg_ref, bdv_ref,
     bdg_ref) = [refs.pop(0) for _ in range(7)]
    wdn_ref = [refs.pop(0) for _ in range(D_MODEL // DOT_N)]
    y_ref, lastv_ref, lastg_ref, h_ref, upv_ref, upg_ref, gate_ref = refs[:7]
    carry_ref = None if sample else refs[7]
    i = pl.program_id(0)
    j = pl.program_id(1)

    @pl.when(j == 0)
    def _():
        _rms_to(x_ref, g_ref, h_ref, tm)
        y_ref[...] = x_ref[...]
        if not sample:
            @pl.when(i == 0)
            def _():
                carry_ref[...] = jnp.zeros(carry_ref.shape, F32)

    hist = SUBLANES
    starts = list(range(0, tm - rc, rc)) + [tm - rc, tm - rc // 2]
    chunks = [(r0, r1 - r0) for r0, r1 in zip(starts, starts[1:] + [tm])]
    ups = ((upv_ref, wuv_ref, lastv_ref), (upg_ref, wug_ref, lastg_ref))

    if not sample:
        first = (i % blocks_per_seq) == 0
        for half, (up_ref, _, _) in enumerate(ups):
            up_ref[0:hist, :] = jnp.where(first, 0.0, carry_ref[j, half])

    def up_proj(c):
        r0, n = chunks[c]
        hc = h_ref[r0:r0 + n, :]
        for up_ref, wu_ref, _ in ups:
            up_ref[hist + r0:hist + r0 + n, :] = jnp.dot(hc, wu_ref[...],
                                                         preferred_element_type=F32)

    def conv(up_ref, st_ref, wd_ref, bd_ref, row, lanes):
        width = lanes.stop - lanes.start
        w0 = wd_ref[0:1, lanes]
        w1 = wd_ref[1:2, lanes]
        w2 = wd_ref[2:3, lanes]
        n = ROWS // SUBLANES
        if sample:
            s0 = row // SUBLANES
            cur3 = up_ref[hist + row:hist + row + ROWS, lanes].reshape(n, SUBLANES, width)
            sl = lax.broadcasted_iota(jnp.int32, (n, SUBLANES, width), 1)
            st0 = st_ref[s0:s0 + n, 0:1, lanes]
            st1 = st_ref[s0:s0 + n, 1:2, lanes]
            m1 = jnp.where(sl == 0, st1, pltpu.roll(cur3, 1, 1))
            m2 = jnp.where(sl == 0, st0, jnp.where(sl == 1, st1, pltpu.roll(cur3, 2, 1)))
            out = w0 * m2 + w1 * m1 + w2 * cur3 + bd_ref[:, lanes]
            return out.reshape(ROWS, width)
        win3 = up_ref[row:row + ROWS + hist, lanes].reshape(n + 1, SUBLANES, width)
        top = lax.broadcasted_iota(jnp.int32, (n, SUBLANES, width), 1) == 0
        r0 = pltpu.roll(w0 * win3, 1, 1)
        s0 = jnp.concatenate([r0[0:1], jnp.where(top, r0[0:n], r0[1:n + 1])], axis=0)
        r1 = pltpu.roll(w1 * win3 + s0, 1, 1)
        s1 = jnp.where(top, r1[0:n], r1[1:n + 1])
        return (w2 * win3[1:n + 1] + s1 + bd_ref[:, lanes]).reshape(ROWS, width)

    def gate_down(c, kh):
        r0, n = chunks[c]
        kw = tf // FFN_K_SPLIT
        lanes = slice(kh * kw, (kh + 1) * kw)
        for r in range(0, n, ROWS):
            cv = conv(upv_ref, stv_ref if sample else None, wdv_ref, bdv_ref, r0 + r, lanes)
            cg = conv(upg_ref, stg_ref if sample else None, wdg_ref, bdg_ref, r0 + r, lanes)
            gate_ref[r:r + ROWS, lanes] = (cv * jax.nn.sigmoid(cv) * cg).astype(BF16)
        gv = gate_ref[0:n, lanes]
        for g in range(len(wdn_ref)):
            cols = slice(g * DOT_N, (g + 1) * DOT_N)
            y_ref[r0:r0 + n, cols] += jnp.dot(gv, wdn_ref[g][lanes, :],
                                              preferred_element_type=F32)

    up_proj(0)
    for c in range(len(chunks)):
        if c + 1 < len(chunks):
            up_proj(c + 1)
        for kh in range(FFN_K_SPLIT):
            gate_down(c, kh)

    for half, (up_ref, _, last_ref) in enumerate(ups):
        if sample:
            up3 = up_ref[hist:hist + tm, :].reshape(tm // SUBLANES, SUBLANES, tf)
            last_ref[...] = up3[:, SUBLANES - (FFN_CONV - 1):, :]
        else:
            tail = up_ref[tm:tm + hist, :]
            carry_ref[j, half] = tail
            last_ref[0] = tail


def _ffn(x, state, seq_len, params, tm, tf, rc):
    g_ffn, w_up, w_dw, b_dw, w_down = params
    rows = x.shape[0]
    sample = state is not None
    nj = D_FF // tf
    grid = (rows // tm, nj)
    row_map = lambda i, j: (i, 0)
    val_map = lambda i, j: (0, j)
    gate_map = lambda i, j: (0, nj + j)
    in_specs = [pl.BlockSpec((tm, D_MODEL), row_map)]
    args = [x]
    if sample:
        spt = tm // seq_len
        in_specs += [pl.BlockSpec((spt, FFN_CONV - 1, tf), lambda i, j: (i, 0, j)),
                     pl.BlockSpec((spt, FFN_CONV - 1, tf), lambda i, j: (i, 0, nj + j))]
        args += [state, state]
    in_specs += [pl.BlockSpec((1, D_MODEL), lambda i, j: (0, 0)),
                 pl.BlockSpec((D_MODEL, tf), val_map), pl.BlockSpec((D_MODEL, tf), gate_map),
                 pl.BlockSpec((FFN_CONV, tf), val_map), pl.BlockSpec((FFN_CONV, tf), gate_map),
                 pl.BlockSpec((1, tf), val_map), pl.BlockSpec((1, tf), gate_map)]
    in_specs += _panel_specs(tf, D_MODEL, index=lambda g: (lambda i, j: (j, g)), resident=False)
    args += [g_ffn, w_up, w_up, w_dw, w_dw, b_dw, b_dw] + [w_down] * (D_MODEL // DOT_N)
    scratch = [pltpu.VMEM((tm, D_MODEL), BF16),
               pltpu.VMEM((tm + SUBLANES, tf), F32), pltpu.VMEM((tm + SUBLANES, tf), F32),
               pltpu.VMEM((rc, tf), BF16)]
    if sample:
        last_specs = [pl.BlockSpec((spt, FFN_CONV - 1, tf), lambda i, j: (i, 0, j))] * 2
        last_shape = [jax.ShapeDtypeStruct((rows // seq_len, FFN_CONV - 1, D_FF), F32)] * 2
        blocks_per_seq = 1
    else:
        blocks_per_seq = seq_len // tm
        last_specs = [pl.BlockSpec((1, SUBLANES, tf), lambda i, j: (i, 0, j))] * 2
        last_shape = [jax.ShapeDtypeStruct((rows // tm, SUBLANES, D_FF), F32)] * 2
        scratch += [pltpu.VMEM((nj, 2, SUBLANES, tf), F32)]
    return pl.pallas_call(
        functools.partial(_ffn_body, sample, tm, tf, rc, blocks_per_seq),
        grid=grid, in_specs=in_specs,
        out_specs=[pl.BlockSpec((tm, D_MODEL), row_map)] + last_specs,
        out_shape=[jax.ShapeDtypeStruct((rows, D_MODEL), F32)] + last_shape,
        scratch_shapes=scratch,
        compiler_params=pltpu.CompilerParams(dimension_semantics=("arbitrary", "arbitrary"),
                                             vmem_limit_bytes=VMEM_LIMIT),
        name="ffn_sample" if sample else "ffn_prompt",
    )(*args)


def _ple_body(tm, *refs):
    refs = list(refs)
    x_ref, p_ref, g_ref = refs[:3]
    n_panels = D_MODEL // DOT_N
    wg_ref = refs[3:3 + n_panels]
    wp_ref = refs[3 + n_panels:3 + 2 * n_panels]
    gf_ref, y_ref, h_ref, gate_ref, proj_ref = refs[3 + 2 * n_panels:]
    _ple_pipeline(tm, x_ref, p_ref, g_ref, wg_ref, wp_ref, gf_ref, y_ref, h_ref, gate_ref, proj_ref)


def _ple_pipeline(tm, x_ref, p_ref, g_ref, wg_ref, wp_ref, gf_ref, y_ref, h_ref, gate_ref, proj_ref):
    n_chunks = tm // CHUNK
    never = pl.program_id(0) < 0
    pieces = CHUNK // ROWS
    anchors = {}

    def norm(c):
        slot = (c % PLE_SLOTS) * CHUNK
        for k in range(pieces):
            r = c * CHUNK + k * ROWS
            xv = x_ref[r:r + ROWS, :]
            ms = jnp.mean(xv * xv, axis=-1, keepdims=True)
            hv = xv * lax.rsqrt(ms + EPS) * g_ref[...]
            anchor = anchors.pop((c - PLE_SKEW, k), None)
            if anchor is not None:
                head = jnp.where(never, anchor, hv[:, :HEAD_DIM])
                hv = jnp.concatenate([head, hv[:, HEAD_DIM:]], axis=-1)
            h_ref[slot + k * ROWS:slot + (k + 1) * ROWS, :] = hv.astype(BF16)

    def projections(c):
        slot = (c % PLE_SLOTS) * CHUNK
        rows = slice(c * CHUNK, (c + 1) * CHUNK)
        pv = p_ref[rows, :].astype(BF16)
        hv = h_ref[slot:slot + CHUNK, :]
        for g in range(D_MODEL // DOT_N):
            cols = slice(g * DOT_N, (g + 1) * DOT_N)
            gate_ref[slot:slot + CHUNK, cols] = jnp.dot(hv, wg_ref[g][...],
                                                        preferred_element_type=F32)
        for g in range(D_MODEL // DOT_N):
            cols = slice(g * DOT_N, (g + 1) * DOT_N)
            proj_ref[slot:slot + CHUNK, cols] = jnp.dot(pv, wp_ref[g][...],
                                                        preferred_element_type=F32)

    def finish(c):
        slot = (c % PLE_SLOTS) * CHUNK
        for k in range(pieces):
            r = c * CHUNK + k * ROWS
            s = slot + k * ROWS
            xv = x_ref[r:r + ROWS, :] + jax.nn.sigmoid(gate_ref[s:s + ROWS, :]) * proj_ref[s:s + ROWS, :]
            ms = jnp.mean(xv * xv, axis=-1, keepdims=True)
            yv = xv * lax.rsqrt(ms + EPS) * gf_ref[...]
            y_ref[r:r + ROWS, :] = yv
            anchors[(c, k)] = yv[:, :HEAD_DIM]

    norm(0)
    for c in range(n_chunks):
        projections(c)
        if c >= 1:
            finish(c - 1)
        if c + 1 < n_chunks:
            norm(c + 1)
    finish(n_chunks - 1)


def _ple(x, p, params, tm, name):
    g_ple, w_gate, w_proj, g_final = params
    rows = x.shape[0]
    row_map = lambda i: (i, 0)
    return pl.pallas_call(
        functools.partial(_ple_body, tm),
        grid=(rows // tm,),
        in_specs=([pl.BlockSpec((tm, D_MODEL), row_map), pl.BlockSpec((tm, D_PLE), row_map),
                   _resident((1, D_MODEL))] + _panel_specs(D_MODEL, D_MODEL)
                  + _panel_specs(D_PLE, D_MODEL) + [_resident((1, D_MODEL))]),
        out_specs=pl.BlockSpec((tm, D_MODEL), row_map),
        out_shape=jax.ShapeDtypeStruct((rows, D_MODEL), F32),
        scratch_shapes=[pltpu.VMEM((PLE_SLOTS * CHUNK, D_MODEL), BF16),
                        pltpu.VMEM((PLE_SLOTS * CHUNK, D_MODEL), F32),
                        pltpu.VMEM((PLE_SLOTS * CHUNK, D_MODEL), F32)],
        compiler_params=pltpu.CompilerParams(dimension_semantics=("arbitrary",),
                                             vmem_limit_bytes=VMEM_LIMIT),
        name=name,
    )(x, p, g_ple, *[w_gate] * (D_MODEL // DOT_N), *[w_proj] * (D_MODEL // DOT_N), g_final)


def _conv_coef(w):
    c = w.shape[1]
    off = HIST_A - (CONV_A - 1)
    n = SUBLANES * (CONV_TILES + 1)
    wp = jnp.zeros((n, c), w.dtype).at[off:off + CONV_A].set(w)
    wp_prev = jnp.concatenate([jnp.zeros((SUBLANES, c), w.dtype), wp[:n - SUBLANES]], axis=0)
    r = jnp.arange(SUBLANES)[:, None]
    q = jnp.arange(CONV_TILES)[None, :]
    d = SUBLANES * q + r
    s = jnp.arange(SUBLANES)[None, None, :, None]
    coef = jnp.where(s >= r[:, :, None, None], wp[d][:, :, None, :], wp_prev[d][:, :, None, :])
    return coef.reshape(SUBLANES * CONV_TILES * SUBLANES, c)


def kernel(x_prompt, x_sample, p_prompt, p_sample, state_conv_a, state_ffn_conv, g_mix, w_in, w_dw_a, b_dw_a, g_ln_a, b_ln_a, g_ln_v, b_ln_v, w_s, b_s, w_out, g_ffn, w_up, w_dw_f, b_dw_f, w_down, g_ple, w_ple_gate, w_ple_proj, g_final):
    depth = w_in.shape[0]
    assert depth == 1, "single-layer step"
    batch, seq, _ = x_prompt.shape
    dec_batch, dec_seq, _ = x_sample.shape
    row = lambda v: v.reshape(1, -1)

    def bias_rows(length):
        b = jnp.tile(b_s[0][:, :length], (1, CHUNK // length))
        return jnp.repeat(b.T, HEAD_DIM, axis=1)

    def mix_weights(length):
        if length == CHUNK:
            return w_s[0]
        sel = (jnp.arange(CHUNK)[:, None] % length == jnp.arange(length)[None, :]).astype(F32)
        return jnp.einsum('ti,hij,sj->hts', sel, w_s[0][:, :length, :length], sel,
                          precision=lax.Precision.HIGHEST)

    mixer_common = (row(g_mix[0]), w_in[0].astype(BF16), _conv_coef(w_dw_a[0]), row(b_dw_a[0]),
                    row(g_ln_a[0]), row(b_ln_a[0]), row(g_ln_v[0]), row(b_ln_v[0]))
    w_out_b = w_out[0].astype(BF16)
    ffn_params = (row(g_ffn[0]), w_up[0].astype(BF16), w_dw_f[0], row(b_dw_f[0]),
                  w_down[0].astype(BF16))
    ple_params = (row(g_ple[0]), w_ple_gate[0].astype(BF16), w_ple_proj[0].astype(BF16),
                  row(g_final))

    xp = x_prompt.reshape(batch * seq, D_MODEL)
    lp = min(seq, CHUNK)
    xp, conv_p, cv_p = _mixer(xp, None, seq,
                              mixer_common + (mix_weights(lp), bias_rows(lp), w_out_b), tm=256)
    xp, lastv_p, lastg_p = _ffn(xp, None, seq, ffn_params, tm=1024, tf=512, rc=256)
    yp = _ple(xp, p_prompt[0].reshape(batch * seq, D_PLE), ple_params, 1024, "ple_prompt")

    xs = x_sample.reshape(dec_batch * dec_seq, D_MODEL)
    ls = min(dec_seq, CHUNK)
    state_pad = jnp.pad(state_conv_a[0], ((0, 0), (HIST_A - (CONV_A - 1), 0), (0, 0)))
    xs, a_s, cv_s = _mixer(xs, state_pad, dec_seq,
                           mixer_common + (mix_weights(ls), bias_rows(ls), w_out_b), tm=256)
    xs, upv_s, upg_s = _ffn(xs, state_ffn_conv[0], dec_seq, ffn_params, tm=1024, tf=512, rc=256)
    ys = _ple(xs, p_sample[0].reshape(dec_batch * dec_seq, D_PLE), ple_params, 1024, "ple_sample")

    keep = FFN_CONV - 1
    conv_a_prompt = conv_p[:, HIST_A - (CONV_A - 1):]
    conv_a_sample = a_s
    per_seq = lastv_p.shape[0] // batch
    ffn_prompt = jnp.concatenate([lastv_p, lastg_p], axis=-1)[per_seq - 1::per_seq, SUBLANES - keep:]
    ffn_sample = jnp.concatenate([upv_s, upg_s], axis=-1)
    return (yp.reshape(batch, seq, D_MODEL), ys.reshape(dec_batch, dec_seq, D_MODEL),
            conv_a_prompt[None], conv_a_sample[None], ffn_prompt[None], ffn_sample[None],
            cv_p[None], cv_s.reshape(dec_batch, dec_seq, C_B)[None])
```

```python
import functools

import jax
import jax.numpy as jnp
from jax import lax
from jax.experimental import pallas as pl
from jax.experimental.pallas import tpu as pltpu

D_MODEL = 2048
HEAD_DIM = 128
C_A = D_MODEL // 2
C_B = D_MODEL - C_A
N_HEADS = C_B // HEAD_DIM
CONV_A = 31
HIST_A = 32
CONV_TILES = HIST_A // 8 + 1
CHUNK = 128
FFN_CONV = 3
D_FF = 5632
D_PLE = 256
EPS = 1e-6

SUBLANES = 8
ROWS = 32
LANES_EW = 256
FFN_K_SPLIT = 2
DOT_N = 512
MIXER_SLOTS = 2
PLE_SKEW = 2
PLE_SLOTS = 3
VMEM_LIMIT = 56 * 1024 * 1024

F32 = jnp.float32
BF16 = jnp.bfloat16


def _resident(shape):
    return pl.BlockSpec(shape, lambda *_: (0,) * len(shape), pipeline_mode=pl.Buffered(1))


def _panel_specs(k, n, index=None, resident=True):
    specs = []
    for g in range(n // DOT_N):
        imap = index(g) if index is not None else (lambda *_, g=g: (0, g))
        specs.append(pl.BlockSpec((k, DOT_N), imap,
                                  pipeline_mode=pl.Buffered(1) if resident else None))
    return specs


def _aligned(x, m):
    return x if isinstance(x, int) else pl.multiple_of(x, m)


def _row_loop(n_rows, step, fn):
    def body(c, carry):
        fn(pl.multiple_of(c * step, step))
        return carry
    lax.fori_loop(0, n_rows // step, body, 0, unroll=8)


def _rms_to(x_ref, g_ref, dst_ref, n_rows):
    def piece(r):
        xv = x_ref[pl.ds(r, ROWS), :]
        ms = jnp.mean(xv * xv, axis=-1, keepdims=True)
        dst_ref[pl.ds(r, ROWS), :] = (xv * lax.rsqrt(ms + EPS) * g_ref[...]).astype(BF16)
    _row_loop(n_rows, ROWS, piece)


def _head_ln(x, g, b):
    outs = []
    for h in range(x.shape[-1] // HEAD_DIM):
        xh = x[:, h * HEAD_DIM:(h + 1) * HEAD_DIM]
        mu = jnp.mean(xh, axis=-1, keepdims=True)
        xc = xh - mu
        var = jnp.mean(xc * xc, axis=-1, keepdims=True)
        outs.append(xc * lax.rsqrt(var + EPS))
    return jnp.concatenate(outs, axis=-1) * g + b


def _conv31(win3, n_out, coef_ref, lanes):
    acc = None
    for r in range(SUBLANES):
        inner = None
        for q in range(CONV_TILES):
            if q == 0 and r < HIST_A - (CONV_A - 1):
                continue
            k = (r * CONV_TILES + q) * SUBLANES
            term = coef_ref[k:k + SUBLANES, lanes] * win3[q:q + n_out]
            inner = term if inner is None else inner + term
        if r:
            inner = pltpu.roll(inner, SUBLANES - r, 1)
        acc = inner if acc is None else acc + inner
    return acc


def _mixer_body(sample, tm, seq_len, *refs):
    refs = list(refs)
    x_ref = refs.pop(0)
    st_ref = refs.pop(0) if sample else None
    gmix_ref = refs.pop(0)
    win_ref = [refs.pop(0) for _ in range((2 * C_A + 2 * C_B) // DOT_N)]
    (coef_ref, bdw_ref, glna_ref, blna_ref, glnv_ref, blnv_ref, ws_ref,
     bsb_ref) = [refs.pop(0) for _ in range(8)]
    wout_ref = [refs.pop(0) for _ in range(D_MODEL // DOT_N)]
    y_ref, aout_ref, vout_ref, h_ref, z_ref, cat_ref, a_ref, vn_ref, wm_ref = refs
    n_chunks = tm // CHUNK
    lane_groups = [slice(g * LANES_EW, (g + 1) * LANES_EW) for g in range(C_A // LANES_EW)]

    def norm(c):
        for r in range(c * CHUNK, (c + 1) * CHUNK, ROWS):
            xv = x_ref[r:r + ROWS, :]
            ms = jnp.mean(xv * xv, axis=-1, keepdims=True)
            h_ref[r:r + ROWS, :] = (xv * lax.rsqrt(ms + EPS) * gmix_ref[...]).astype(BF16)

    ti = lax.broadcasted_iota(jnp.int32, (CHUNK, CHUNK), 0)
    si = lax.broadcasted_iota(jnp.int32, (CHUNK, CHUNK), 1)
    if seq_len >= CHUNK:
        mask = si <= ti
    else:
        mask = ((si // seq_len) == (ti // seq_len)) & ((si % seq_len) <= (ti % seq_len))
    for h in range(N_HEADS):
        wm_ref[h] = jnp.where(mask, ws_ref[h], 0.0).astype(BF16)

    if not sample:
        t = pl.program_id(1)

        @pl.when(t == 0)
        def _():
            a_ref[0:HIST_A, :] = jnp.zeros((HIST_A, C_A), F32)

        @pl.when(t > 0)
        def _():
            a_ref[0:HIST_A, :] = a_ref[tm:tm + HIST_A, :]

    a_base = 0 if sample else HIST_A

    def slot_rows(c):
        lo = (c % MIXER_SLOTS) * CHUNK
        return slice(lo, lo + CHUNK)

    def in_proj(c):
        rows = slice(c * CHUNK, (c + 1) * CHUNK)
        hv = h_ref[rows, :]
        for g in range(len(win_ref)):
            z_ref[slot_rows(c), g * DOT_N:(g + 1) * DOT_N] = jnp.dot(
                hv, win_ref[g][...], preferred_element_type=F32)

    def out_proj(c):
        rows = slice(c * CHUNK, (c + 1) * CHUNK)
        cv = cat_ref[slot_rows(c), :]
        for g in range(len(wout_ref)):
            cols = slice(g * DOT_N, (g + 1) * DOT_N)
            y_ref[rows, cols] = x_ref[rows, cols] + jnp.dot(cv, wout_ref[g][...],
                                                            preferred_element_type=F32)

    def elementwise(c):
        r0 = c * CHUNK
        s0 = slot_rows(c).start - r0
        for r in range(r0, r0 + CHUNK, ROWS):
            for lanes in lane_groups:
                zv = z_ref[s0 + r:s0 + r + ROWS, lanes]
                zg = z_ref[s0 + r:s0 + r + ROWS, C_A + lanes.start:C_A + lanes.stop]
                a = zv * jax.nn.sigmoid(zg)
                a_ref[a_base + r:a_base + r + ROWS, lanes] = a
                if sample:
                    keep = CONV_A - 1 - seq_len
                    for q in range(ROWS // seq_len):
                        s = r // seq_len + q
                        aout_ref[s, 0:keep, lanes] = st_ref[s, HIST_A - keep:HIST_A, lanes]
                        aout_ref[s, keep:CONV_A - 1, lanes] = a[q * seq_len:(q + 1) * seq_len]
        for r in range(r0, r0 + CHUNK, ROWS):
            for lanes in lane_groups:
                width = lanes.stop - lanes.start
                if sample:
                    outs = []
                    for rs in range(r, r + ROWS, seq_len):
                        hist = st_ref[rs // seq_len, :, lanes].reshape(
                            CONV_TILES - 1, SUBLANES, width)
                        new = a_ref[rs:rs + seq_len, lanes].reshape(1, SUBLANES, width)
                        win3 = jnp.concatenate([hist, new], axis=0)
                        outs.append(_conv31(win3, 1, coef_ref, lanes))
                    conv = jnp.concatenate(outs, axis=0).reshape(ROWS, width)
                else:
                    n_out = ROWS // SUBLANES
                    win3 = a_ref[r:r + ROWS + HIST_A, lanes].reshape(
                        n_out + CONV_TILES - 1, SUBLANES, width)
                    conv = _conv31(win3, n_out, coef_ref, lanes).reshape(ROWS, width)
                conv = conv + bdw_ref[:, lanes]
                yv = _head_ln(conv, glna_ref[:, lanes], blna_ref[:, lanes])
                cat_ref[s0 + r:s0 + r + ROWS, lanes] = (yv * jax.nn.sigmoid(yv)).astype(BF16)

        v_off = 2 * C_A + C_B
        for r in range(r0, r0 + CHUNK, ROWS):
            for lanes in lane_groups:
                v = jax.nn.gelu(z_ref[s0 + r:s0 + r + ROWS, v_off + lanes.start:v_off + lanes.stop])
                vn = _head_ln(v, glnv_ref[:, lanes], blnv_ref[:, lanes])
                vn_ref[r - r0:r - r0 + ROWS, lanes] = vn.astype(BF16)
                if sample:
                    vout_ref[r:r + ROWS, lanes] = vn
                else:
                    vout_ref[0, r - r0:r - r0 + ROWS, lanes] = vn
        for h in range(N_HEADS):
            lo = h * HEAD_DIM
            mixed = jnp.dot(wm_ref[h], vn_ref[:, lo:lo + HEAD_DIM], preferred_element_type=F32)
            mixed = mixed + bsb_ref[:, lo:lo + HEAD_DIM]
            u = jax.nn.gelu(z_ref[slot_rows(c), 2 * C_A + lo:2 * C_A + lo + HEAD_DIM])
            cat_ref[slot_rows(c), C_A + lo:C_A + lo + HEAD_DIM] = (u * mixed).astype(BF16)

    norm(0)
    in_proj(0)
    for c in range(n_chunks):
        if c + 1 < n_chunks:
            norm(c + 1)
            in_proj(c + 1)
        elementwise(c)
        out_proj(c)

    if not sample:
        aout_ref[0] = a_ref[tm:tm + HIST_A, :]


def _mixer(x, state, seq_len, params, tm):
    (g_mix, w_in, w_dw, b_dw, g_ln_a, b_ln_a, g_ln_v, b_ln_v, w_s, bias_rows, w_out) = params
    rows = x.shape[0]
    sample = state is not None
    vec = lambda n: _resident((1, n))
    common_in = ([vec(D_MODEL)] + _panel_specs(D_MODEL, 2 * C_A + 2 * C_B)
                 + [_resident((SUBLANES * CONV_TILES * SUBLANES, C_A)),
                    vec(C_A), vec(C_A), vec(C_A), vec(C_B), vec(C_B),
                    _resident((N_HEADS, CHUNK, CHUNK)), _resident((CHUNK, C_B))]
                 + _panel_specs(C_A + C_B, D_MODEL))
    ring = MIXER_SLOTS * CHUNK
    common_scratch = [pltpu.VMEM((tm, D_MODEL), BF16),
                      pltpu.VMEM((ring, 2 * C_A + 2 * C_B), F32),
                      pltpu.VMEM((ring, C_A + C_B), BF16)]
    tail_scratch = [pltpu.VMEM((CHUNK, C_B), BF16),
                    pltpu.VMEM((N_HEADS, CHUNK, CHUNK), BF16)]
    if sample:
        n_seq = rows // seq_len
        spt = tm // seq_len
        grid = (rows // tm,)
        row_map = lambda i: (i, 0)
        in_specs = [pl.BlockSpec((tm, D_MODEL), row_map),
                    pl.BlockSpec((spt, HIST_A, C_A), lambda i: (i, 0, 0),
                                 pipeline_mode=pl.Buffered(1))] + common_in
        out_specs = [pl.BlockSpec((tm, D_MODEL), row_map),
                     pl.BlockSpec((spt, CONV_A - 1, C_A), lambda i: (i, 0, 0)),
                     pl.BlockSpec((tm, C_B), row_map)]
        out_shape = [jax.ShapeDtypeStruct((rows, D_MODEL), F32),
                     jax.ShapeDtypeStruct((n_seq, CONV_A - 1, C_A), F32),
                     jax.ShapeDtypeStruct((rows, C_B), F32)]
        scratch = common_scratch + [pltpu.VMEM((tm, C_A), F32)] + tail_scratch
        args = (x, state)
        sem = ("arbitrary",)
        del n_seq
    else:
        n_seq = rows // seq_len
        nt = seq_len // tm
        grid = (n_seq, nt)
        row_map = lambda b, t: (b * nt + t, 0)
        in_specs = [pl.BlockSpec((tm, D_MODEL), row_map)] + common_in
        out_specs = [pl.BlockSpec((tm, D_MODEL), row_map),
                     pl.BlockSpec((1, HIST_A, C_A), lambda b, t: (b, 0, 0)),
                     pl.BlockSpec((1, CHUNK, C_B), lambda b, t: (b, 0, 0))]
        out_shape = [jax.ShapeDtypeStruct((rows, D_MODEL), F32),
                     jax.ShapeDtypeStruct((n_seq, HIST_A, C_A), F32),
                     jax.ShapeDtypeStruct((n_seq, CHUNK, C_B), F32)]
        scratch = common_scratch + [pltpu.VMEM((tm + HIST_A, C_A), F32)] + tail_scratch
        args = (x,)
        sem = ("arbitrary", "arbitrary")
    return pl.pallas_call(
        functools.partial(_mixer_body, sample, tm, seq_len),
        grid=grid, in_specs=in_specs, out_specs=out_specs, out_shape=out_shape,
        scratch_shapes=scratch,
        compiler_params=pltpu.CompilerParams(dimension_semantics=sem,
                                             vmem_limit_bytes=VMEM_LIMIT),
        name="mixer_sample" if sample else "mixer_prompt",
    )(*args, g_mix, *[w_in] * (w_in.shape[1] // DOT_N), w_dw, b_dw, g_ln_a, b_ln_a, g_ln_v,
      b_ln_v, w_s, bias_rows, *[w_out] * (w_out.shape[1] // DOT_N))


def _ffn_body(sample, tm, tf, rc, blocks_per_seq, *refs):
    refs = list(refs)
    x_ref = refs.pop(0)
    stv_ref, stg_ref = (refs.pop(0), refs.pop(0)) if sample else (None, None)
    (g_ref, wuv_ref, wug_ref, wdv_ref, wdg_ref, bdv_ref,
     bdg_ref) = [refs.pop(0) for _ in range(7)]
    wdn_ref = [refs.pop(0) for _ in range(D_MODEL // DOT_N)]
    y_ref, lastv_ref, lastg_ref, h_ref, upv_ref, upg_ref, gate_ref = refs[:7]
    carry_ref = None if sample else refs[7]
    i = pl.program_id(0)
    j = pl.program_id(1)

    @pl.when(j == 0)
    def _():
        _rms_to(x_ref, g_ref, h_ref, tm)
        y_ref[...] = x_ref[...]
        if not sample:
            @pl.when(i == 0)
            def _():
                carry_ref[...] = jnp.zeros(carry_ref.shape, F32)

    hist = SUBLANES
    starts = list(range(0, tm - rc, rc)) + [tm - rc, tm - rc // 2]
    chunks = [(r0, r1 - r0) for r0, r1 in zip(starts, starts[1:] + [tm])]
    ups = ((upv_ref, wuv_ref, lastv_ref), (upg_ref, wug_ref, lastg_ref))

    if not sample:
        first = (i % blocks_per_seq) == 0
        for half, (up_ref, _, _) in enumerate(ups):
            up_ref[0:hist, :] = jnp.where(first, 0.0, carry_ref[j, half])

    def up_proj(c):
        r0, n = chunks[c]
        hc = h_ref[r0:r0 + n, :]
        for up_ref, wu_ref, _ in ups:
            up_ref[hist + r0:hist + r0 + n, :] = jnp.dot(hc, wu_ref[...],
                                                         preferred_element_type=F32)

    def conv(up_ref, st_ref, wd_ref, bd_ref, row, lanes):
        width = lanes.stop - lanes.start
        w0 = wd_ref[0:1, lanes]
        w1 = wd_ref[1:2, lanes]
        w2 = wd_ref[2:3, lanes]
        n = ROWS // SUBLANES
        if sample:
            s0 = row // SUBLANES
            cur3 = up_ref[hist + row:hist + row + ROWS, lanes].reshape(n, SUBLANES, width)
            sl = lax.broadcasted_iota(jnp.int32, (n, SUBLANES, width), 1)
            st0 = st_ref[s0:s0 + n, 0:1, lanes]
            st1 = st_ref[s0:s0 + n, 1:2, lanes]
            m1 = jnp.where(sl == 0, st1, pltpu.roll(cur3, 1, 1))
            m2 = jnp.where(sl == 0, st0, jnp.where(sl == 1, st1, pltpu.roll(cur3, 2, 1)))
            out = w0 * m2 + w1 * m1 + w2 * cur3 + bd_ref[:, lanes]
            return out.reshape(ROWS, width)
        win3 = up_ref[row:row + ROWS + hist, lanes].reshape(n + 1, SUBLANES, width)
        top = lax.broadcasted_iota(jnp.int32, (n, SUBLANES, width), 1) == 0
        r0 = pltpu.roll(w0 * win3, 1, 1)
        s0 = jnp.concatenate([r0[0:1], jnp.where(top, r0[0:n], r0[1:n + 1])], axis=0)
        r1 = pltpu.roll(w1 * win3 + s0, 1, 1)
        s1 = jnp.where(top, r1[0:n], r1[1:n + 1])
        return (w2 * win3[1:n + 1] + s1 + bd_ref[:, lanes]).reshape(ROWS, width)

    def gate_down(c, kh):
        r0, n = chunks[c]
        kw = tf // FFN_K_SPLIT
        lanes = slice(kh * kw, (kh + 1) * kw)
        for r in range(0, n, ROWS):
            cv = conv(upv_ref, stv_ref if sample else None, wdv_ref, bdv_ref, r0 + r, lanes)
            cg = conv(upg_ref, stg_ref if sample else None, wdg_ref, bdg_ref, r0 + r, lanes)
            gate_ref[r:r + ROWS, lanes] = (cv * jax.nn.sigmoid(cv) * cg).astype(BF16)
        gv = gate_ref[0:n, lanes]
        for g in range(len(wdn_ref)):
            cols = slice(g * DOT_N, (g + 1) * DOT_N)
            y_ref[r0:r0 + n, cols] += jnp.dot(gv, wdn_ref[g][lanes, :],
                                              preferred_element_type=F32)

    up_proj(0)
    for c in range(len(chunks)):
        if c + 1 < len(chunks):
            up_proj(c + 1)
        for kh in range(FFN_K_SPLIT):
            gate_down(c, kh)

    for half, (up_ref, _, last_ref) in enumerate(ups):
        if sample:
            up3 = up_ref[hist:hist + tm, :].reshape(tm // SUBLANES, SUBLANES, tf)
            last_ref[...] = up3[:, SUBLANES - (FFN_CONV - 1):, :]
        else:
            tail = up_ref[tm:tm + hist, :]
            carry_ref[j, half] = tail
            last_ref[0] = tail


def _ffn(x, state, seq_len, params, tm, tf, rc):
    g_ffn, w_up, w_dw, b_dw, w_down = params
    rows = x.shape[0]
    sample = state is not None
    nj = D_FF // tf
    grid = (rows // tm, nj)
    row_map = lambda i, j: (i, 0)
    val_map = lambda i, j: (0, j)
    gate_map = lambda i, j: (0, nj + j)
    in_specs = [pl.BlockSpec((tm, D_MODEL), row_map)]
    args = [x]
    if sample:
        spt = tm // seq_len
        in_specs += [pl.BlockSpec((spt, FFN_CONV - 1, tf), lambda i, j: (i, 0, j)),
                     pl.BlockSpec((spt, FFN_CONV - 1, tf), lambda i, j: (i, 0, nj + j))]
        args += [state, state]
    in_specs += [pl.BlockSpec((1, D_MODEL), lambda i, j: (0, 0)),
                 pl.BlockSpec((D_MODEL, tf), val_map), pl.BlockSpec((D_MODEL, tf), gate_map),
                 pl.BlockSpec((FFN_CONV, tf), val_map), pl.BlockSpec((FFN_CONV, tf), gate_map),
                 pl.BlockSpec((1, tf), val_map), pl.BlockSpec((1, tf), gate_map)]
    in_specs += _panel_specs(tf, D_MODEL, index=lambda g: (lambda i, j: (j, g)), resident=False)
    args += [g_ffn, w_up, w_up, w_dw, w_dw, b_dw, b_dw] + [w_down] * (D_MODEL // DOT_N)
    scratch = [pltpu.VMEM((tm, D_MODEL), BF16),
               pltpu.VMEM((tm + SUBLANES, tf), F32), pltpu.VMEM((tm + SUBLANES, tf), F32),
               pltpu.VMEM((rc, tf), BF16)]
    if sample:
        last_specs = [pl.BlockSpec((spt, FFN_CONV - 1, tf), lambda i, j: (i, 0, j))] * 2
        last_shape = [jax.ShapeDtypeStruct((rows // seq_len, FFN_CONV - 1, D_FF), F32)] * 2
        blocks_per_seq = 1
    else:
        blocks_per_seq = seq_len // tm
        last_specs = [pl.BlockSpec((1, SUBLANES, tf), lambda i, j: (i, 0, j))] * 2
        last_shape = [jax.ShapeDtypeStruct((rows // tm, SUBLANES, D_FF), F32)] * 2
        scratch += [pltpu.VMEM((nj, 2, SUBLANES, tf), F32)]
    return pl.pallas_call(
        functools.partial(_ffn_body, sample, tm, tf, rc, blocks_per_seq),
        grid=grid, in_specs=in_specs,
        out_specs=[pl.BlockSpec((tm, D_MODEL), row_map)] + last_specs,
        out_shape=[jax.ShapeDtypeStruct((rows, D_MODEL), F32)] + last_shape,
        scratch_shapes=scratch,
        compiler_params=pltpu.CompilerParams(dimension_semantics=("arbitrary", "arbitrary"),
                                             vmem_limit_bytes=VMEM_LIMIT),
        name="ffn_sample" if sample else "ffn_prompt",
    )(*args)


def _ple_body(tm, *refs):
    refs = list(refs)
    x_ref, p_ref, g_ref = refs[:3]
    n_panels = D_MODEL // DOT_N
    wg_ref = refs[3:3 + n_panels]
    wp_ref = refs[3 + n_panels:3 + 2 * n_panels]
    gf_ref, y_ref, h_ref, gate_ref, proj_ref = refs[3 + 2 * n_panels:]
    _ple_pipeline(tm, x_ref, p_ref, g_ref, wg_ref, wp_ref, gf_ref, y_ref, h_ref, gate_ref, proj_ref)


def _ple_pipeline(tm, x_ref, p_ref, g_ref, wg_ref, wp_ref, gf_ref, y_ref, h_ref, gate_ref, proj_ref):
    n_chunks = tm // CHUNK
    never = pl.program_id(0) < 0
    pieces = CHUNK // ROWS
    anchors = {}

    def norm(c):
        slot = (c % PLE_SLOTS) * CHUNK
        for k in range(pieces):
            r = c * CHUNK + k * ROWS
            xv = x_ref[r:r + ROWS, :]
            ms = jnp.mean(xv * xv, axis=-1, keepdims=True)
            hv = xv * lax.rsqrt(ms + EPS) * g_ref[...]
            anchor = anchors.pop((c - PLE_SKEW, k), None)
            if anchor is not None:
                head = jnp.where(never, anchor, hv[:, :HEAD_DIM])
                hv = jnp.concatenate([head, hv[:, HEAD_DIM:]], axis=-1)
            h_ref[slot + k * ROWS:slot + (k + 1) * ROWS, :] = hv.astype(BF16)

    def projections(c):
        slot = (c % PLE_SLOTS) * CHUNK
        rows = slice(c * CHUNK, (c + 1) * CHUNK)
        pv = p_ref[rows, :].astype(BF16)
        for g in range(D_MODEL // DOT_N):
            cols = slice(g * DOT_N, (g + 1) * DOT_N)
            gate_ref[slot:slot + CHUNK, cols] = jnp.dot(
                h_ref[slot:slot + CHUNK, :], wg_ref[g][...], preferred_element_type=F32)
            proj_ref[slot:slot + CHUNK, cols] = jnp.dot(pv, wp_ref[g][...],
                                                        preferred_element_type=F32)

    def finish(c):
        slot = (c % PLE_SLOTS) * CHUNK
        for k in range(pieces):
            r = c * CHUNK + k * ROWS
            s = slot + k * ROWS
            xv = x_ref[r:r + ROWS, :] + jax.nn.sigmoid(gate_ref[s:s + ROWS, :]) * proj_ref[s:s + ROWS, :]
            ms = jnp.mean(xv * xv, axis=-1, keepdims=True)
            yv = xv * lax.rsqrt(ms + EPS) * gf_ref[...]
            y_ref[r:r + ROWS, :] = yv
            anchors[(c, k)] = yv[:, :HEAD_DIM]

    norm(0)
    for c in range(n_chunks):
        projections(c)
        if c >= 1:
            finish(c - 1)
        if c + 1 < n_chunks:
            norm(c + 1)
    finish(n_chunks - 1)


def _ple(x, p, params, tm, name):
    g_ple, w_gate, w_proj, g_final = params
    rows = x.shape[0]
    row_map = lambda i: (i, 0)
    return pl.pallas_call(
        functools.partial(_ple_body, tm),
        grid=(rows // tm,),
        in_specs=([pl.BlockSpec((tm, D_MODEL), row_map), pl.BlockSpec((tm, D_PLE), row_map),
                   _resident((1, D_MODEL))] + _panel_specs(D_MODEL, D_MODEL)
                  + _panel_specs(D_PLE, D_MODEL) + [_resident((1, D_MODEL))]),
        out_specs=pl.BlockSpec((tm, D_MODEL), row_map),
        out_shape=jax.ShapeDtypeStruct((rows, D_MODEL), F32),
        scratch_shapes=[pltpu.VMEM((PLE_SLOTS * CHUNK, D_MODEL), BF16),
                        pltpu.VMEM((PLE_SLOTS * CHUNK, D_MODEL), F32),
                        pltpu.VMEM((PLE_SLOTS * CHUNK, D_MODEL), F32)],
        compiler_params=pltpu.CompilerParams(dimension_semantics=("arbitrary",),
                                             vmem_limit_bytes=VMEM_LIMIT),
        name=name,
    )(x, p, g_ple, *[w_gate] * (D_MODEL // DOT_N), *[w_proj] * (D_MODEL // DOT_N), g_final)


def _conv_coef(w):
    c = w.shape[1]
    off = HIST_A - (CONV_A - 1)
    n = SUBLANES * (CONV_TILES + 1)
    wp = jnp.zeros((n, c), w.dtype).at[off:off + CONV_A].set(w)
    wp_prev = jnp.concatenate([jnp.zeros((SUBLANES, c), w.dtype), wp[:n - SUBLANES]], axis=0)
    r = jnp.arange(SUBLANES)[:, None]
    q = jnp.arange(CONV_TILES)[None, :]
    d = SUBLANES * q + r
    s = jnp.arange(SUBLANES)[None, None, :, None]
    coef = jnp.where(s >= r[:, :, None, None], wp[d][:, :, None, :], wp_prev[d][:, :, None, :])
    return coef.reshape(SUBLANES * CONV_TILES * SUBLANES, c)


def kernel(x_prompt, x_sample, p_prompt, p_sample, state_conv_a, state_ffn_conv, g_mix, w_in, w_dw_a, b_dw_a, g_ln_a, b_ln_a, g_ln_v, b_ln_v, w_s, b_s, w_out, g_ffn, w_up, w_dw_f, b_dw_f, w_down, g_ple, w_ple_gate, w_ple_proj, g_final):
    depth = w_in.shape[0]
    assert depth == 1, "single-layer step"
    batch, seq, _ = x_prompt.shape
    dec_batch, dec_seq, _ = x_sample.shape
    row = lambda v: v.reshape(1, -1)

    def bias_rows(length):
        b = jnp.tile(b_s[0][:, :length], (1, CHUNK // length))
        return jnp.repeat(b.T, HEAD_DIM, axis=1)

    def mix_weights(length):
        if length == CHUNK:
            return w_s[0]
        sel = (jnp.arange(CHUNK)[:, None] % length == jnp.arange(length)[None, :]).astype(F32)
        return jnp.einsum('ti,hij,sj->hts', sel, w_s[0][:, :length, :length], sel,
                          precision=lax.Precision.HIGHEST)

    mixer_common = (row(g_mix[0]), w_in[0].astype(BF16), _conv_coef(w_dw_a[0]), row(b_dw_a[0]),
                    row(g_ln_a[0]), row(b_ln_a[0]), row(g_ln_v[0]), row(b_ln_v[0]))
    w_out_b = w_out[0].astype(BF16)
    ffn_params = (row(g_ffn[0]), w_up[0].astype(BF16), w_dw_f[0], row(b_dw_f[0]),
                  w_down[0].astype(BF16))
    ple_params = (row(g_ple[0]), w_ple_gate[0].astype(BF16), w_ple_proj[0].astype(BF16),
                  row(g_final))

    xp = x_prompt.reshape(batch * seq, D_MODEL)
    lp = min(seq, CHUNK)
    xp, conv_p, cv_p = _mixer(xp, None, seq,
                              mixer_common + (mix_weights(lp), bias_rows(lp), w_out_b), tm=256)
    xp, lastv_p, lastg_p = _ffn(xp, None, seq, ffn_params, tm=1024, tf=512, rc=256)
    yp = _ple(xp, p_prompt[0].reshape(batch * seq, D_PLE), ple_params, 1024, "ple_prompt")

    xs = x_sample.reshape(dec_batch * dec_seq, D_MODEL)
    ls = min(dec_seq, CHUNK)
    state_pad = jnp.pad(state_conv_a[0], ((0, 0), (HIST_A - (CONV_A - 1), 0), (0, 0)))
    xs, a_s, cv_s = _mixer(xs, state_pad, dec_seq,
                           mixer_common + (mix_weights(ls), bias_rows(ls), w_out_b), tm=256)
    xs, upv_s, upg_s = _ffn(xs, state_ffn_conv[0], dec_seq, ffn_params, tm=1024, tf=512, rc=256)
    ys = _ple(xs, p_sample[0].reshape(dec_batch * dec_seq, D_PLE), ple_params, 1024, "ple_sample")

    keep = FFN_CONV - 1
    conv_a_prompt = conv_p[:, HIST_A - (CONV_A - 1):]
    conv_a_sample = a_s
    per_seq = lastv_p.shape[0] // batch
    ffn_prompt = jnp.concatenate([lastv_p, lastg_p], axis=-1)[per_seq - 1::per_seq, SUBLANES - keep:]
    ffn_sample = jnp.concatenate([upv_s, upg_s], axis=-1)
    return (yp.reshape(batch, seq, D_MODEL), ys.reshape(dec_batch, dec_seq, D_MODEL),
            conv_a_prompt[None], conv_a_sample[None], ffn_prompt[None], ffn_sample[None],
            cv_p[None], cv_s.reshape(dec_batch, dec_seq, C_B)[None])
```

```python
import functools

import jax
import jax.numpy as jnp
from jax import lax
from jax.experimental import pallas as pl
from jax.experimental.pallas import tpu as pltpu

D_MODEL = 2048
HEAD_DIM = 128
C_A = D_MODEL // 2
C_B = D_MODEL - C_A
N_HEADS = C_B // HEAD_DIM
CONV_A = 31
HIST_A = 32
CONV_TILES = HIST_A // 8 + 1
CHUNK = 128
FFN_CONV = 3
D_FF = 5632
D_PLE = 256
EPS = 1e-6

SUBLANES = 8
ROWS = 32
LANES_EW = 128
FFN_K_SPLIT = 2
DOT_N = 512
MIXER_SLOTS = 2
PLE_SKEW = 2
PLE_SLOTS = 3
VMEM_LIMIT = 56 * 1024 * 1024

F32 = jnp.float32
BF16 = jnp.bfloat16


def _resident(shape):
    return pl.BlockSpec(shape, lambda *_: (0,) * len(shape), pipeline_mode=pl.Buffered(1))


def _panel_specs(k, n, index=None, resident=True):
    specs = []
    for g in range(n // DOT_N):
        imap = index(g) if index is not None else (lambda *_, g=g: (0, g))
        specs.append(pl.BlockSpec((k, DOT_N), imap,
                                  pipeline_mode=pl.Buffered(1) if resident else None))
    return specs


def _aligned(x, m):
    return x if isinstance(x, int) else pl.multiple_of(x, m)


def _row_loop(n_rows, step, fn):
    def body(c, carry):
        fn(pl.multiple_of(c * step, step))
        return carry
    lax.fori_loop(0, n_rows // step, body, 0, unroll=16)


def _rms_to(x_ref, g_ref, dst_ref, n_rows):
    def piece(r):
        xv = x_ref[pl.ds(r, ROWS), :]
        ms = jnp.mean(xv * xv, axis=-1, keepdims=True)
        dst_ref[pl.ds(r, ROWS), :] = (xv * lax.rsqrt(ms + EPS) * g_ref[...]).astype(BF16)
    _row_loop(n_rows, ROWS, piece)


def _head_ln(x, g, b):
    outs = []
    for h in range(x.shape[-1] // HEAD_DIM):
        xh = x[:, h * HEAD_DIM:(h + 1) * HEAD_DIM]
        mu = jnp.mean(xh, axis=-1, keepdims=True)
        xc = xh - mu
        var = jnp.mean(xc * xc, axis=-1, keepdims=True)
        outs.append(xc * lax.rsqrt(var + EPS))
    return jnp.concatenate(outs, axis=-1) * g + b


def _conv31(win3, n_out, coef_ref, lanes):
    acc = None
    for r in range(SUBLANES):
        inner = None
        for q in range(CONV_TILES):
            if q == 0 and r < HIST_A - (CONV_A - 1):
                continue
            k = (r * CONV_TILES + q) * SUBLANES
            term = coef_ref[k:k + SUBLANES, lanes] * win3[q:q + n_out]
            inner = term if inner is None else inner + term
        if r:
            inner = pltpu.roll(inner, SUBLANES - r, 1)
        acc = inner if acc is None else acc + inner
    return acc


def _mixer_body(sample, tm, seq_len, *refs):
    refs = list(refs)
    x_ref = refs.pop(0)
    st_ref = refs.pop(0) if sample else None
    gmix_ref = refs.pop(0)
    win_ref = [refs.pop(0) for _ in range((2 * C_A + 2 * C_B) // DOT_N)]
    (coef_ref, bdw_ref, glna_ref, blna_ref, glnv_ref, blnv_ref, ws_ref,
     bsb_ref) = [refs.pop(0) for _ in range(8)]
    wout_ref = [refs.pop(0) for _ in range(D_MODEL // DOT_N)]
    y_ref, aout_ref, vout_ref, h_ref, z_ref, cat_ref, a_ref, vn_ref, wm_ref = refs
    n_chunks = tm // CHUNK
    lane_groups = [slice(g * LANES_EW, (g + 1) * LANES_EW) for g in range(C_A // LANES_EW)]

    def norm(c):
        for r in range(c * CHUNK, (c + 1) * CHUNK, ROWS):
            xv = x_ref[r:r + ROWS, :]
            ms = jnp.mean(xv * xv, axis=-1, keepdims=True)
            h_ref[r:r + ROWS, :] = (xv * lax.rsqrt(ms + EPS) * gmix_ref[...]).astype(BF16)

    ti = lax.broadcasted_iota(jnp.int32, (CHUNK, CHUNK), 0)
    si = lax.broadcasted_iota(jnp.int32, (CHUNK, CHUNK), 1)
    if seq_len >= CHUNK:
        mask = si <= ti
    else:
        mask = ((si // seq_len) == (ti // seq_len)) & ((si % seq_len) <= (ti % seq_len))
    for h in range(N_HEADS):
        wm_ref[h] = jnp.where(mask, ws_ref[h], 0.0).astype(BF16)

    if not sample:
        t = pl.program_id(1)

        @pl.when(t == 0)
        def _():
            a_ref[0:HIST_A, :] = jnp.zeros((HIST_A, C_A), F32)

        @pl.when(t > 0)
        def _():
            a_ref[0:HIST_A, :] = a_ref[tm:tm + HIST_A, :]

    a_base = 0 if sample else HIST_A

    def slot_rows(c):
        lo = (c % MIXER_SLOTS) * CHUNK
        return slice(lo, lo + CHUNK)

    def in_proj(c):
        rows = slice(c * CHUNK, (c + 1) * CHUNK)
        hv = h_ref[rows, :]
        for g in range(len(win_ref)):
            z_ref[slot_rows(c), g * DOT_N:(g + 1) * DOT_N] = jnp.dot(
                hv, win_ref[g][...], preferred_element_type=F32)

    def out_proj(c):
        rows = slice(c * CHUNK, (c + 1) * CHUNK)
        cv = cat_ref[slot_rows(c), :]
        for g in range(len(wout_ref)):
            cols = slice(g * DOT_N, (g + 1) * DOT_N)
            y_ref[rows, cols] = x_ref[rows, cols] + jnp.dot(cv, wout_ref[g][...],
                                                            preferred_element_type=F32)

    def elementwise(c):
        r0 = c * CHUNK
        s0 = slot_rows(c).start - r0
        for r in range(r0, r0 + CHUNK, ROWS):
            for lanes in lane_groups:
                zv = z_ref[s0 + r:s0 + r + ROWS, lanes]
                zg = z_ref[s0 + r:s0 + r + ROWS, C_A + lanes.start:C_A + lanes.stop]
                a = zv * jax.nn.sigmoid(zg)
                a_ref[a_base + r:a_base + r + ROWS, lanes] = a
                if sample:
                    keep = CONV_A - 1 - seq_len
                    for q in range(ROWS // seq_len):
                        s = r // seq_len + q
                        aout_ref[s, 0:keep, lanes] = st_ref[s, HIST_A - keep:HIST_A, lanes]
                        aout_ref[s, keep:CONV_A - 1, lanes] = a[q * seq_len:(q + 1) * seq_len]
        for r in range(r0, r0 + CHUNK, ROWS):
            for lanes in lane_groups:
                width = lanes.stop - lanes.start
                if sample:
                    outs = []
                    for rs in range(r, r + ROWS, seq_len):
                        hist = st_ref[rs // seq_len, :, lanes].reshape(
                            CONV_TILES - 1, SUBLANES, width)
                        new = a_ref[rs:rs + seq_len, lanes].reshape(1, SUBLANES, width)
                        win3 = jnp.concatenate([hist, new], axis=0)
                        outs.append(_conv31(win3, 1, coef_ref, lanes))
                    conv = jnp.concatenate(outs, axis=0).reshape(ROWS, width)
                else:
                    n_out = ROWS // SUBLANES
                    win3 = a_ref[r:r + ROWS + HIST_A, lanes].reshape(
                        n_out + CONV_TILES - 1, SUBLANES, width)
                    conv = _conv31(win3, n_out, coef_ref, lanes).reshape(ROWS, width)
                conv = conv + bdw_ref[:, lanes]
                yv = _head_ln(conv, glna_ref[:, lanes], blna_ref[:, lanes])
                cat_ref[s0 + r:s0 + r + ROWS, lanes] = (yv * jax.nn.sigmoid(yv)).astype(BF16)

        v_off = 2 * C_A + C_B
        for r in range(r0, r0 + CHUNK, ROWS):
            for lanes in lane_groups:
                v = jax.nn.gelu(z_ref[s0 + r:s0 + r + ROWS, v_off + lanes.start:v_off + lanes.stop])
                vn = _head_ln(v, glnv_ref[:, lanes], blnv_ref[:, lanes])
                vn_ref[r - r0:r - r0 + ROWS, lanes] = vn.astype(BF16)
                if sample:
                    vout_ref[r:r + ROWS, lanes] = vn
                else:
                    vout_ref[0, r - r0:r - r0 + ROWS, lanes] = vn
        for h in range(N_HEADS):
            lo = h * HEAD_DIM
            mixed = jnp.dot(wm_ref[h], vn_ref[:, lo:lo + HEAD_DIM], preferred_element_type=F32)
            mixed = mixed + bsb_ref[:, lo:lo + HEAD_DIM]
            u = jax.nn.gelu(z_ref[slot_rows(c), 2 * C_A + lo:2 * C_A + lo + HEAD_DIM])
            cat_ref[slot_rows(c), C_A + lo:C_A + lo + HEAD_DIM] = (u * mixed).astype(BF16)

    norm(0)
    in_proj(0)
    for c in range(n_chunks):
        if c + 1 < n_chunks:
            norm(c + 1)
            in_proj(c + 1)
        elementwise(c)
        out_proj(c)

    if not sample:
        aout_ref[0] = a_ref[tm:tm + HIST_A, :]


def _mixer(x, state, seq_len, params, tm):
    (g_mix, w_in, w_dw, b_dw, g_ln_a, b_ln_a, g_ln_v, b_ln_v, w_s, bias_rows, w_out) = params
    rows = x.shape[0]
    sample = state is not None
    vec = lambda n: _resident((1, n))
    common_in = ([vec(D_MODEL)] + _panel_specs(D_MODEL, 2 * C_A + 2 * C_B)
                 + [_resident((SUBLANES * CONV_TILES * SUBLANES, C_A)),
                    vec(C_A), vec(C_A), vec(C_A), vec(C_B), vec(C_B),
                    _resident((N_HEADS, CHUNK, CHUNK)), _resident((CHUNK, C_B))]
                 + _panel_specs(C_A + C_B, D_MODEL))
    ring = MIXER_SLOTS * CHUNK
    common_scratch = [pltpu.VMEM((tm, D_MODEL), BF16),
                      pltpu.VMEM((ring, 2 * C_A + 2 * C_B), F32),
                      pltpu.VMEM((ring, C_A + C_B), BF16)]
    tail_scratch = [pltpu.VMEM((CHUNK, C_B), BF16),
                    pltpu.VMEM((N_HEADS, CHUNK, CHUNK), BF16)]
    if sample:
        n_seq = rows // seq_len
        spt = tm // seq_len
        grid = (rows // tm,)
        row_map = lambda i: (i, 0)
        in_specs = [pl.BlockSpec((tm, D_MODEL), row_map),
                    pl.BlockSpec((spt, HIST_A, C_A), lambda i: (i, 0, 0),
                                 pipeline_mode=pl.Buffered(1))] + common_in
        out_specs = [pl.BlockSpec((tm, D_MODEL), row_map),
                     pl.BlockSpec((spt, CONV_A - 1, C_A), lambda i: (i, 0, 0)),
                     pl.BlockSpec((tm, C_B), row_map)]
        out_shape = [jax.ShapeDtypeStruct((rows, D_MODEL), F32),
                     jax.ShapeDtypeStruct((n_seq, CONV_A - 1, C_A), F32),
                     jax.ShapeDtypeStruct((rows, C_B), F32)]
        scratch = common_scratch + [pltpu.VMEM((tm, C_A), F32)] + tail_scratch
        args = (x, state)
        sem = ("arbitrary",)
        del n_seq
    else:
        n_seq = rows // seq_len
        nt = seq_len // tm
        grid = (n_seq, nt)
        row_map = lambda b, t: (b * nt + t, 0)
        in_specs = [pl.BlockSpec((tm, D_MODEL), row_map)] + common_in
        out_specs = [pl.BlockSpec((tm, D_MODEL), row_map),
                     pl.BlockSpec((1, HIST_A, C_A), lambda b, t: (b, 0, 0)),
                     pl.BlockSpec((1, CHUNK, C_B), lambda b, t: (b, 0, 0))]
        out_shape = [jax.ShapeDtypeStruct((rows, D_MODEL), F32),
                     jax.ShapeDtypeStruct((n_seq, HIST_A, C_A), F32),
                     jax.ShapeDtypeStruct((n_seq, CHUNK, C_B), F32)]
        scratch = common_scratch + [pltpu.VMEM((tm + HIST_A, C_A), F32)] + tail_scratch
        args = (x,)
        sem = ("arbitrary", "arbitrary")
    return pl.pallas_call(
        functools.partial(_mixer_body, sample, tm, seq_len),
        grid=grid, in_specs=in_specs, out_specs=out_specs, out_shape=out_shape,
        scratch_shapes=scratch,
        compiler_params=pltpu.CompilerParams(dimension_semantics=sem,
                                             vmem_limit_bytes=VMEM_LIMIT),
        name="mixer_sample" if sample else "mixer_prompt",
    )(*args, g_mix, *[w_in] * (w_in.shape[1] // DOT_N), w_dw, b_dw, g_ln_a, b_ln_a, g_ln_v,
      b_ln_v, w_s, bias_rows, *[w_out] * (w_out.shape[1] // DOT_N))


def _ffn_body(sample, tm, tf, rc, blocks_per_seq, *refs):
    refs = list(refs)
    x_ref = refs.pop(0)
    stv_ref, stg_ref = (refs.pop(0), refs.pop(0)) if sample else (None, None)
    (g_ref, wuv_ref, wug_ref, wdv_ref, wdg_ref, bdv_ref,
     bdg_ref) = [refs.pop(0) for _ in range(7)]
    wdn_ref = [refs.pop(0) for _ in range(D_MODEL // DOT_N)]
    y_ref, lastv_ref, lastg_ref, h_ref, upv_ref, upg_ref, gate_ref = refs[:7]
    carry_ref = None if sample else refs[7]
    i = pl.program_id(0)
    j = pl.program_id(1)

    @pl.when(j == 0)
    def _():
        _rms_to(x_ref, g_ref, h_ref, tm)
        y_ref[...] = x_ref[...]
        if not sample:
            @pl.when(i == 0)
            def _():
                carry_ref[...] = jnp.zeros(carry_ref.shape, F32)

    hist = SUBLANES
    starts = list(range(0, tm - rc, rc)) + [tm - rc, tm - rc // 2]
    chunks = [(r0, r1 - r0) for r0, r1 in zip(starts, starts[1:] + [tm])]
    ups = ((upv_ref, wuv_ref, lastv_ref), (upg_ref, wug_ref, lastg_ref))

    if not sample:
        first = (i % blocks_per_seq) == 0
        for half, (up_ref, _, _) in enumerate(ups):
            up_ref[0:hist, :] = jnp.where(first, 0.0, carry_ref[j, half])

    def up_proj(c):
        r0, n = chunks[c]
        hc = h_ref[r0:r0 + n, :]
        for up_ref, wu_ref, _ in ups:
            up_ref[hist + r0:hist + r0 + n, :] = jnp.dot(hc, wu_ref[...],
                                                         preferred_element_type=F32)

    def conv(up_ref, st_ref, wd_ref, bd_ref, row, lanes):
        width = lanes.stop - lanes.start
        w0 = wd_ref[0:1, lanes]
        w1 = wd_ref[1:2, lanes]
        w2 = wd_ref[2:3, lanes]
        n = ROWS // SUBLANES
        if sample:
            s0 = row // SUBLANES
            cur3 = up_ref[hist + row:hist + row + ROWS, lanes].reshape(n, SUBLANES, width)
            sl = lax.broadcasted_iota(jnp.int32, (n, SUBLANES, width), 1)
            st0 = st_ref[s0:s0 + n, 0:1, lanes]
            st1 = st_ref[s0:s0 + n, 1:2, lanes]
            m1 = jnp.where(sl == 0, st1, pltpu.roll(cur3, 1, 1))
            m2 = jnp.where(sl == 0, st0, jnp.where(sl == 1, st1, pltpu.roll(cur3, 2, 1)))
            out = w0 * m2 + w1 * m1 + w2 * cur3 + bd_ref[:, lanes]
            return out.reshape(ROWS, width)
        win3 = up_ref[row:row + ROWS + hist, lanes].reshape(n + 1, SUBLANES, width)
        top = lax.broadcasted_iota(jnp.int32, (n, SUBLANES, width), 1) == 0
        r0 = pltpu.roll(w0 * win3, 1, 1)
        s0 = jnp.concatenate([r0[0:1], jnp.where(top, r0[0:n], r0[1:n + 1])], axis=0)
        r1 = pltpu.roll(w1 * win3 + s0, 1, 1)
        s1 = jnp.where(top, r1[0:n], r1[1:n + 1])
        return (w2 * win3[1:n + 1] + s1 + bd_ref[:, lanes]).reshape(ROWS, width)

    def gate_down(c, kh):
        r0, n = chunks[c]
        kw = tf // FFN_K_SPLIT
        lanes = slice(kh * kw, (kh + 1) * kw)
        for r in range(0, n, ROWS):
            cv = conv(upv_ref, stv_ref if sample else None, wdv_ref, bdv_ref, r0 + r, lanes)
            cg = conv(upg_ref, stg_ref if sample else None, wdg_ref, bdg_ref, r0 + r, lanes)
            gate_ref[r:r + ROWS, lanes] = (cv * jax.nn.sigmoid(cv) * cg).astype(BF16)
        gv = gate_ref[0:n, lanes]
        for g in range(len(wdn_ref)):
            cols = slice(g * DOT_N, (g + 1) * DOT_N)
            y_ref[r0:r0 + n, cols] += jnp.dot(gv, wdn_ref[g][lanes, :],
                                              preferred_element_type=F32)

    up_proj(0)
    for c in range(len(chunks)):
        if c + 1 < len(chunks):
            up_proj(c + 1)
        for kh in range(FFN_K_SPLIT):
            gate_down(c, kh)

    for half, (up_ref, _, last_ref) in enumerate(ups):
        if sample:
            up3 = up_ref[hist:hist + tm, :].reshape(tm // SUBLANES, SUBLANES, tf)
            last_ref[...] = up3[:, SUBLANES - (FFN_CONV - 1):, :]
        else:
            tail = up_ref[tm:tm + hist, :]
            carry_ref[j, half] = tail
            last_ref[0] = tail


def _ffn(x, state, seq_len, params, tm, tf, rc):
    g_ffn, w_up, w_dw, b_dw, w_down = params
    rows = x.shape[0]
    sample = state is not None
    nj = D_FF // tf
    grid = (rows // tm, nj)
    row_map = lambda i, j: (i, 0)
    val_map = lambda i, j: (0, j)
    gate_map = lambda i, j: (0, nj + j)
    in_specs = [pl.BlockSpec((tm, D_MODEL), row_map)]
    args = [x]
    if sample:
        spt = tm // seq_len
        in_specs += [pl.BlockSpec((spt, FFN_CONV - 1, tf), lambda i, j: (i, 0, j)),
                     pl.BlockSpec((spt, FFN_CONV - 1, tf), lambda i, j: (i, 0, nj + j))]
        args += [state, state]
    in_specs += [pl.BlockSpec((1, D_MODEL), lambda i, j: (0, 0)),
                 pl.BlockSpec((D_MODEL, tf), val_map), pl.BlockSpec((D_MODEL, tf), gate_map),
                 pl.BlockSpec((FFN_CONV, tf), val_map), pl.BlockSpec((FFN_CONV, tf), gate_map),
                 pl.BlockSpec((1, tf), val_map), pl.BlockSpec((1, tf), gate_map)]
    in_specs += _panel_specs(tf, D_MODEL, index=lambda g: (lambda i, j: (j, g)), resident=False)
    args += [g_ffn, w_up, w_up, w_dw, w_dw, b_dw, b_dw] + [w_down] * (D_MODEL // DOT_N)
    scratch = [pltpu.VMEM((tm, D_MODEL), BF16),
               pltpu.VMEM((tm + SUBLANES, tf), F32), pltpu.VMEM((tm + SUBLANES, tf), F32),
               pltpu.VMEM((rc, tf), BF16)]
    if sample:
        last_specs = [pl.BlockSpec((spt, FFN_CONV - 1, tf), lambda i, j: (i, 0, j))] * 2
        last_shape = [jax.ShapeDtypeStruct((rows // seq_len, FFN_CONV - 1, D_FF), F32)] * 2
        blocks_per_seq = 1
    else:
        blocks_per_seq = seq_len // tm
        last_specs = [pl.BlockSpec((1, SUBLANES, tf), lambda i, j: (i, 0, j))] * 2
        last_shape = [jax.ShapeDtypeStruct((rows // tm, SUBLANES, D_FF), F32)] * 2
        scratch += [pltpu.VMEM((nj, 2, SUBLANES, tf), F32)]
    return pl.pallas_call(
        functools.partial(_ffn_body, sample, tm, tf, rc, blocks_per_seq),
        grid=grid, in_specs=in_specs,
        out_specs=[pl.BlockSpec((tm, D_MODEL), row_map)] + last_specs,
        out_shape=[jax.ShapeDtypeStruct((rows, D_MODEL), F32)] + last_shape,
        scratch_shapes=scratch,
        compiler_params=pltpu.CompilerParams(dimension_semantics=("arbitrary", "arbitrary"),
                                             vmem_limit_bytes=VMEM_LIMIT),
        name="ffn_sample" if sample else "ffn_prompt",
    )(*args)


def _ple_body(tm, *refs):
    refs = list(refs)
    x_ref, p_ref, g_ref = refs[:3]
    n_panels = D_MODEL // DOT_N
    wg_ref = refs[3:3 + n_panels]
    wp_ref = refs[3 + n_panels:3 + 2 * n_panels]
    gf_ref, y_ref, h_ref, gate_ref, proj_ref = refs[3 + 2 * n_panels:]
    _ple_pipeline(tm, x_ref, p_ref, g_ref, wg_ref, wp_ref, gf_ref, y_ref, h_ref, gate_ref, proj_ref)


def _ple_pipeline(tm, x_ref, p_ref, g_ref, wg_ref, wp_ref, gf_ref, y_ref, h_ref, gate_ref, proj_ref):
    n_chunks = tm // CHUNK
    never = pl.program_id(0) < 0
    pieces = CHUNK // ROWS
    anchors = {}

    def norm(c):
        slot = (c % PLE_SLOTS) * CHUNK
        for k in range(pieces):
            r = c * CHUNK + k * ROWS
            xv = x_ref[r:r + ROWS, :]
            ms = jnp.mean(xv * xv, axis=-1, keepdims=True)
            hv = xv * lax.rsqrt(ms + EPS) * g_ref[...]
            anchor = anchors.pop((c - PLE_SKEW, k), None)
            if anchor is not None:
                head = jnp.where(never, anchor, hv[:, :HEAD_DIM])
                hv = jnp.concatenate([head, hv[:, HEAD_DIM:]], axis=-1)
            h_ref[slot + k * ROWS:slot + (k + 1) * ROWS, :] = hv.astype(BF16)

    def projections(c):
        slot = (c % PLE_SLOTS) * CHUNK
        rows = slice(c * CHUNK, (c + 1) * CHUNK)
        pv = p_ref[rows, :].astype(BF16)
        hv = h_ref[slot:slot + CHUNK, :]
        for g in range(D_MODEL // DOT_N):
            cols = slice(g * DOT_N, (g + 1) * DOT_N)
            gate_ref[slot:slot + CHUNK, cols] = jnp.dot(hv, wg_ref[g][...],
                                                        preferred_element_type=F32)
            proj_ref[slot:slot + CHUNK, cols] = jnp.dot(pv, wp_ref[g][...],
                                                        preferred_element_type=F32)

    def finish(c):
        slot = (c % PLE_SLOTS) * CHUNK
        for k in range(pieces):
            r = c * CHUNK + k * ROWS
            s = slot + k * ROWS
            xv = x_ref[r:r + ROWS, :] + jax.nn.sigmoid(gate_ref[s:s + ROWS, :]) * proj_ref[s:s + ROWS, :]
            ms = jnp.mean(xv * xv, axis=-1, keepdims=True)
            yv = xv * lax.rsqrt(ms + EPS) * gf_ref[...]
            y_ref[r:r + ROWS, :] = yv
            anchors[(c, k)] = yv[:, :HEAD_DIM]

    norm(0)
    for c in range(n_chunks):
        projections(c)
        if c >= 1:
            finish(c - 1)
        if c + 1 < n_chunks:
            norm(c + 1)
    finish(n_chunks - 1)


def _ple(x, p, params, tm, name):
    g_ple, w_gate, w_proj, g_final = params
    rows = x.shape[0]
    row_map = lambda i: (i, 0)
    return pl.pallas_call(
        functools.partial(_ple_body, tm),
        grid=(rows // tm,),
        in_specs=([pl.BlockSpec((tm, D_MODEL), row_map), pl.BlockSpec((tm, D_PLE), row_map),
                   _resident((1, D_MODEL))] + _panel_specs(D_MODEL, D_MODEL)
                  + _panel_specs(D_PLE, D_MODEL) + [_resident((1, D_MODEL))]),
        out_specs=pl.BlockSpec((tm, D_MODEL), row_map),
        out_shape=jax.ShapeDtypeStruct((rows, D_MODEL), F32),
        scratch_shapes=[pltpu.VMEM((PLE_SLOTS * CHUNK, D_MODEL), BF16),
                        pltpu.VMEM((PLE_SLOTS * CHUNK, D_MODEL), F32),
                        pltpu.VMEM((PLE_SLOTS * CHUNK, D_MODEL), F32)],
        compiler_params=pltpu.CompilerParams(dimension_semantics=("arbitrary",),
                                             vmem_limit_bytes=VMEM_LIMIT),
        name=name,
    )(x, p, g_ple, *[w_gate] * (D_MODEL // DOT_N), *[w_proj] * (D_MODEL // DOT_N), g_final)


def _conv_coef(w):
    c = w.shape[1]
    off = HIST_A - (CONV_A - 1)
    n = SUBLANES * (CONV_TILES + 1)
    wp = jnp.zeros((n, c), w.dtype).at[off:off + CONV_A].set(w)
    wp_prev = jnp.concatenate([jnp.zeros((SUBLANES, c), w.dtype), wp[:n - SUBLANES]], axis=0)
    r = jnp.arange(SUBLANES)[:, None]
    q = jnp.arange(CONV_TILES)[None, :]
    d = SUBLANES * q + r
    s = jnp.arange(SUBLANES)[None, None, :, None]
    coef = jnp.where(s >= r[:, :, None, None], wp[d][:, :, None, :], wp_prev[d][:, :, None, :])
    return coef.reshape(SUBLANES * CONV_TILES * SUBLANES, c)


def kernel(x_prompt, x_sample, p_prompt, p_sample, state_conv_a, state_ffn_conv, g_mix, w_in, w_dw_a, b_dw_a, g_ln_a, b_ln_a, g_ln_v, b_ln_v, w_s, b_s, w_out, g_ffn, w_up, w_dw_f, b_dw_f, w_down, g_ple, w_ple_gate, w_ple_proj, g_final):
    depth = w_in.shape[0]
    assert depth == 1, "single-layer step"
    batch, seq, _ = x_prompt.shape
    dec_batch, dec_seq, _ = x_sample.shape
    row = lambda v: v.reshape(1, -1)

    def bias_rows(length):
        b = jnp.tile(b_s[0][:, :length], (1, CHUNK // length))
        return jnp.repeat(b.T, HEAD_DIM, axis=1)

    def mix_weights(length):
        if length == CHUNK:
            return w_s[0]
        sel = (jnp.arange(CHUNK)[:, None] % length == jnp.arange(length)[None, :]).astype(F32)
        return jnp.einsum('ti,hij,sj->hts', sel, w_s[0][:, :length, :length], sel,
                          precision=lax.Precision.HIGHEST)

    mixer_common = (row(g_mix[0]), w_in[0].astype(BF16), _conv_coef(w_dw_a[0]), row(b_dw_a[0]),
                    row(g_ln_a[0]), row(b_ln_a[0]), row(g_ln_v[0]), row(b_ln_v[0]))
    w_out_b = w_out[0].astype(BF16)
    ffn_params = (row(g_ffn[0]), w_up[0].astype(BF16), w_dw_f[0], row(b_dw_f[0]),
                  w_down[0].astype(BF16))
    ple_params = (row(g_ple[0]), w_ple_gate[0].astype(BF16), w_ple_proj[0].astype(BF16),
                  row(g_final))

    xp = x_prompt.reshape(batch * seq, D_MODEL)
    lp = min(seq, CHUNK)
    xp, conv_p, cv_p = _mixer(xp, None, seq,
                              mixer_common + (mix_weights(lp), bias_rows(lp), w_out_b), tm=256)
    xp, lastv_p, lastg_p = _ffn(xp, None, seq, ffn_params, tm=1024, tf=512, rc=256)
    yp = _ple(xp, p_prompt[0].reshape(batch * seq, D_PLE), ple_params, 1024, "ple_prompt")

    xs = x_sample.reshape(dec_batch * dec_seq, D_MODEL)
    ls = min(dec_seq, CHUNK)
    state_pad = jnp.pad(state_conv_a[0], ((0, 0), (HIST_A - (CONV_A - 1), 0), (0, 0)))
    xs, a_s, cv_s = _mixer(xs, state_pad, dec_seq,
                           mixer_common + (mix_weights(ls), bias_rows(ls), w_out_b), tm=256)
    xs, upv_s, upg_s = _ffn(xs, state_ffn_conv[0], dec_seq, ffn_params, tm=1024, tf=512, rc=256)
    ys = _ple(xs, p_sample[0].reshape(dec_batch * dec_seq, D_PLE), ple_params, 1024, "ple_sample")

    keep = FFN_CONV - 1
    conv_a_prompt = conv_p[:, HIST_A - (CONV_A - 1):]
    conv_a_sample = a_s
    per_seq = lastv_p.shape[0] // batch
    ffn_prompt = jnp.concatenate([lastv_p, lastg_p], axis=-1)[per_seq - 1::per_seq, SUBLANES - keep:]
    ffn_sample = jnp.concatenate([upv_s, upg_s], axis=-1)
    return (yp.reshape(batch, seq, D_MODEL), ys.reshape(dec_batch, dec_seq, D_MODEL),
            conv_a_prompt[None], conv_a_sample[None], ffn_prompt[None], ffn_sample[None],
            cv_p[None], cv_s.reshape(dec_batch, dec_seq, C_B)[None])
```

```python
import functools

import jax
import jax.numpy as jnp
from jax import lax
from jax.experimental import pallas as pl
from jax.experimental.pallas import tpu as pltpu

D_MODEL = 2048
HEAD_DIM = 128
C_A = D_MODEL // 2
C_B = D_MODEL - C_A
N_HEADS = C_B // HEAD_DIM
CONV_A = 31
HIST_A = 32
CONV_TILES = HIST_A // 8 + 1
CHUNK = 128
FFN_CONV = 3
D_FF = 5632
D_PLE = 256
EPS = 1e-6

SUBLANES = 8
ROWS = 32
LANES_EW = 128
FFN_ROWS = 64
FFN_K_SPLIT = 2
DOT_N = 512
MIXER_SLOTS = 2
PLE_SKEW = 2
PLE_SLOTS = 3
VMEM_LIMIT = 56 * 1024 * 1024

F32 = jnp.float32
BF16 = jnp.bfloat16


def _resident(shape):
    return pl.BlockSpec(shape, lambda *_: (0,) * len(shape), pipeline_mode=pl.Buffered(1))


def _panel_specs(k, n, index=None, resident=True):
    specs = []
    for g in range(n // DOT_N):
        imap = index(g) if index is not None else (lambda *_, g=g: (0, g))
        specs.append(pl.BlockSpec((k, DOT_N), imap,
                                  pipeline_mode=pl.Buffered(1) if resident else None))
    return specs


def _aligned(x, m):
    return x if isinstance(x, int) else pl.multiple_of(x, m)


def _row_loop(n_rows, step, fn):
    def body(c, carry):
        fn(pl.multiple_of(c * step, step))
        return carry
    lax.fori_loop(0, n_rows // step, body, 0, unroll=16)


def _rms_to(x_ref, g_ref, dst_ref, n_rows):
    def piece(r):
        xv = x_ref[pl.ds(r, ROWS), :]
        ms = jnp.mean(xv * xv, axis=-1, keepdims=True)
        dst_ref[pl.ds(r, ROWS), :] = (xv * lax.rsqrt(ms + EPS) * g_ref[...]).astype(BF16)
    _row_loop(n_rows, ROWS, piece)


def _head_ln(x, g, b):
    outs = []
    for h in range(x.shape[-1] // HEAD_DIM):
        xh = x[:, h * HEAD_DIM:(h + 1) * HEAD_DIM]
        mu = jnp.mean(xh, axis=-1, keepdims=True)
        xc = xh - mu
        var = jnp.mean(xc * xc, axis=-1, keepdims=True)
        outs.append(xc * lax.rsqrt(var + EPS))
    return jnp.concatenate(outs, axis=-1) * g + b


def _conv31(win3, n_out, coef_ref, lanes):
    acc = None
    for r in range(SUBLANES):
        inner = None
        for q in range(CONV_TILES):
            if q == 0 and r < HIST_A - (CONV_A - 1):
                continue
            k = (r * CONV_TILES + q) * SUBLANES
            term = coef_ref[k:k + SUBLANES, lanes] * win3[q:q + n_out]
            inner = term if inner is None else inner + term
        if r:
            inner = pltpu.roll(inner, SUBLANES - r, 1)
        acc = inner if acc is None else acc + inner
    return acc


def _mixer_body(sample, tm, seq_len, *refs):
    refs = list(refs)
    x_ref = refs.pop(0)
    st_ref = refs.pop(0) if sample else None
    gmix_ref = refs.pop(0)
    win_ref = [refs.pop(0) for _ in range((2 * C_A + 2 * C_B) // DOT_N)]
    (coef_ref, bdw_ref, glna_ref, blna_ref, glnv_ref, blnv_ref, ws_ref,
     bsb_ref) = [refs.pop(0) for _ in range(8)]
    wout_ref = [refs.pop(0) for _ in range(D_MODEL // DOT_N)]
    y_ref, aout_ref, vout_ref, h_ref, z_ref, cat_ref, a_ref, vn_ref, wm_ref = refs
    n_chunks = tm // CHUNK
    lane_groups = [slice(g * LANES_EW, (g + 1) * LANES_EW) for g in range(C_A // LANES_EW)]

    def norm(c):
        for r in range(c * CHUNK, (c + 1) * CHUNK, ROWS):
            xv = x_ref[r:r + ROWS, :]
            ms = jnp.mean(xv * xv, axis=-1, keepdims=True)
            h_ref[r:r + ROWS, :] = (xv * lax.rsqrt(ms + EPS) * gmix_ref[...]).astype(BF16)

    ti = lax.broadcasted_iota(jnp.int32, (CHUNK, CHUNK), 0)
    si = lax.broadcasted_iota(jnp.int32, (CHUNK, CHUNK), 1)
    if seq_len >= CHUNK:
        mask = si <= ti
    else:
        mask = ((si // seq_len) == (ti // seq_len)) & ((si % seq_len) <= (ti % seq_len))
    for h in range(N_HEADS):
        wm_ref[h] = jnp.where(mask, ws_ref[h], 0.0).astype(BF16)

    if not sample:
        t = pl.program_id(1)

        @pl.when(t == 0)
        def _():
            a_ref[0:HIST_A, :] = jnp.zeros((HIST_A, C_A), F32)

        @pl.when(t > 0)
        def _():
            a_ref[0:HIST_A, :] = a_ref[tm:tm + HIST_A, :]

    a_base = 0 if sample else HIST_A

    def slot_rows(c):
        lo = (c % MIXER_SLOTS) * CHUNK
        return slice(lo, lo + CHUNK)

    def in_proj(c):
        rows = slice(c * CHUNK, (c + 1) * CHUNK)
        hv = h_ref[rows, :]
        for g in range(len(win_ref)):
            z_ref[slot_rows(c), g * DOT_N:(g + 1) * DOT_N] = jnp.dot(
                hv, win_ref[g][...], preferred_element_type=F32)

    def out_proj(c):
        rows = slice(c * CHUNK, (c + 1) * CHUNK)
        cv = cat_ref[slot_rows(c), :]
        for g in range(len(wout_ref)):
            cols = slice(g * DOT_N, (g + 1) * DOT_N)
            y_ref[rows, cols] = x_ref[rows, cols] + jnp.dot(cv, wout_ref[g][...],
                                                            preferred_element_type=F32)

    def elementwise(c):
        r0 = c * CHUNK
        s0 = slot_rows(c).start - r0
        for r in range(r0, r0 + CHUNK, ROWS):
            for lanes in lane_groups:
                zv = z_ref[s0 + r:s0 + r + ROWS, lanes]
                zg = z_ref[s0 + r:s0 + r + ROWS, C_A + lanes.start:C_A + lanes.stop]
                a = zv * jax.nn.sigmoid(zg)
                a_ref[a_base + r:a_base + r + ROWS, lanes] = a
                if sample:
                    keep = CONV_A - 1 - seq_len
                    for q in range(ROWS // seq_len):
                        s = r // seq_len + q
                        aout_ref[s, 0:keep, lanes] = st_ref[s, HIST_A - keep:HIST_A, lanes]
                        aout_ref[s, keep:CONV_A - 1, lanes] = a[q * seq_len:(q + 1) * seq_len]
        for r in range(r0, r0 + CHUNK, ROWS):
            for lanes in lane_groups:
                width = lanes.stop - lanes.start
                if sample:
                    outs = []
                    for rs in range(r, r + ROWS, seq_len):
                        hist = st_ref[rs // seq_len, :, lanes].reshape(
                            CONV_TILES - 1, SUBLANES, width)
                        new = a_ref[rs:rs + seq_len, lanes].reshape(1, SUBLANES, width)
                        win3 = jnp.concatenate([hist, new], axis=0)
                        outs.append(_conv31(win3, 1, coef_ref, lanes))
                    conv = jnp.concatenate(outs, axis=0).reshape(ROWS, width)
                else:
                    n_out = ROWS // SUBLANES
                    win3 = a_ref[r:r + ROWS + HIST_A, lanes].reshape(
                        n_out + CONV_TILES - 1, SUBLANES, width)
                    conv = _conv31(win3, n_out, coef_ref, lanes).reshape(ROWS, width)
                conv = conv + bdw_ref[:, lanes]
                yv = _head_ln(conv, glna_ref[:, lanes], blna_ref[:, lanes])
                cat_ref[s0 + r:s0 + r + ROWS, lanes] = (yv * jax.nn.sigmoid(yv)).astype(BF16)

        v_off = 2 * C_A + C_B
        for r in range(r0, r0 + CHUNK, ROWS):
            for lanes in lane_groups:
                v = jax.nn.gelu(z_ref[s0 + r:s0 + r + ROWS, v_off + lanes.start:v_off + lanes.stop])
                vn = _head_ln(v, glnv_ref[:, lanes], blnv_ref[:, lanes])
                vn_ref[r - r0:r - r0 + ROWS, lanes] = vn.astype(BF16)
                if sample:
                    vout_ref[r:r + ROWS, lanes] = vn
                else:
                    vout_ref[0, r - r0:r - r0 + ROWS, lanes] = vn
        for h in range(N_HEADS):
            lo = h * HEAD_DIM
            mixed = jnp.dot(wm_ref[h], vn_ref[:, lo:lo + HEAD_DIM], preferred_element_type=F32)
            mixed = mixed + bsb_ref[:, lo:lo + HEAD_DIM]
            u = jax.nn.gelu(z_ref[slot_rows(c), 2 * C_A + lo:2 * C_A + lo + HEAD_DIM])
            cat_ref[slot_rows(c), C_A + lo:C_A + lo + HEAD_DIM] = (u * mixed).astype(BF16)

    norm(0)
    in_proj(0)
    for c in range(n_chunks):
        if c + 1 < n_chunks:
            norm(c + 1)
            in_proj(c + 1)
        elementwise(c)
        out_proj(c)

    if not sample:
        aout_ref[0] = a_ref[tm:tm + HIST_A, :]


def _mixer(x, state, seq_len, params, tm):
    (g_mix, w_in, w_dw, b_dw, g_ln_a, b_ln_a, g_ln_v, b_ln_v, w_s, bias_rows, w_out) = params
    rows = x.shape[0]
    sample = state is not None
    vec = lambda n: _resident((1, n))
    common_in = ([vec(D_MODEL)] + _panel_specs(D_MODEL, 2 * C_A + 2 * C_B)
                 + [_resident((SUBLANES * CONV_TILES * SUBLANES, C_A)),
                    vec(C_A), vec(C_A), vec(C_A), vec(C_B), vec(C_B),
                    _resident((N_HEADS, CHUNK, CHUNK)), _resident((CHUNK, C_B))]
                 + _panel_specs(C_A + C_B, D_MODEL))
    ring = MIXER_SLOTS * CHUNK
    common_scratch = [pltpu.VMEM((tm, D_MODEL), BF16),
                      pltpu.VMEM((ring, 2 * C_A + 2 * C_B), F32),
                      pltpu.VMEM((ring, C_A + C_B), BF16)]
    tail_scratch = [pltpu.VMEM((CHUNK, C_B), BF16),
                    pltpu.VMEM((N_HEADS, CHUNK, CHUNK), BF16)]
    if sample:
        n_seq = rows // seq_len
        spt = tm // seq_len
        grid = (rows // tm,)
        row_map = lambda i: (i, 0)
        in_specs = [pl.BlockSpec((tm, D_MODEL), row_map),
                    pl.BlockSpec((spt, HIST_A, C_A), lambda i: (i, 0, 0),
                                 pipeline_mode=pl.Buffered(1))] + common_in
        out_specs = [pl.BlockSpec((tm, D_MODEL), row_map),
                     pl.BlockSpec((spt, CONV_A - 1, C_A), lambda i: (i, 0, 0)),
                     pl.BlockSpec((tm, C_B), row_map)]
        out_shape = [jax.ShapeDtypeStruct((rows, D_MODEL), F32),
                     jax.ShapeDtypeStruct((n_seq, CONV_A - 1, C_A), F32),
                     jax.ShapeDtypeStruct((rows, C_B), F32)]
        scratch = common_scratch + [pltpu.VMEM((tm, C_A), F32)] + tail_scratch
        args = (x, state)
        sem = ("arbitrary",)
        del n_seq
    else:
        n_seq = rows // seq_len
        nt = seq_len // tm
        grid = (n_seq, nt)
        row_map = lambda b, t: (b * nt + t, 0)
        in_specs = [pl.BlockSpec((tm, D_MODEL), row_map)] + common_in
        out_specs = [pl.BlockSpec((tm, D_MODEL), row_map),
                     pl.BlockSpec((1, HIST_A, C_A), lambda b, t: (b, 0, 0)),
                     pl.BlockSpec((1, CHUNK, C_B), lambda b, t: (b, 0, 0))]
        out_shape = [jax.ShapeDtypeStruct((rows, D_MODEL), F32),
                     jax.ShapeDtypeStruct((n_seq, HIST_A, C_A), F32),
                     jax.ShapeDtypeStruct((n_seq, CHUNK, C_B), F32)]
        scratch = common_scratch + [pltpu.VMEM((tm + HIST_A, C_A), F32)] + tail_scratch
        args = (x,)
        sem = ("arbitrary", "arbitrary")
    return pl.pallas_call(
        functools.partial(_mixer_body, sample, tm, seq_len),
        grid=grid, in_specs=in_specs, out_specs=out_specs, out_shape=out_shape,
        scratch_shapes=scratch,
        compiler_params=pltpu.CompilerParams(dimension_semantics=sem,
                                             vmem_limit_bytes=VMEM_LIMIT),
        name="mixer_sample" if sample else "mixer_prompt",
    )(*args, g_mix, *[w_in] * (w_in.shape[1] // DOT_N), w_dw, b_dw, g_ln_a, b_ln_a, g_ln_v,
      b_ln_v, w_s, bias_rows, *[w_out] * (w_out.shape[1] // DOT_N))


def _ffn_body(sample, tm, tf, rc, blocks_per_seq, *refs):
    refs = list(refs)
    x_ref = refs.pop(0)
    stv_ref, stg_ref = (refs.pop(0), refs.pop(0)) if sample else (None, None)
    (g_ref, wuv_ref, wug_ref, wdv_ref, wdg_ref, bdv_ref,
     bdg_ref) = [refs.pop(0) for _ in range(7)]
    wdn_ref = [refs.pop(0) for _ in range(D_MODEL // DOT_N)]
    y_ref, lastv_ref, lastg_ref, h_ref, upv_ref, upg_ref, gate_ref = refs[:7]
    carry_ref = None if sample else refs[7]
    i = pl.program_id(0)
    j = pl.program_id(1)

    @pl.when(j == 0)
    def _():
        _rms_to(x_ref, g_ref, h_ref, tm)
        y_ref[...] = x_ref[...]
        if not sample:
            @pl.when(i == 0)
            def _():
                carry_ref[...] = jnp.zeros(carry_ref.shape, F32)

    hist = SUBLANES
    starts = list(range(0, tm - rc, rc)) + [tm - rc, tm - rc // 2]
    chunks = [(r0, r1 - r0) for r0, r1 in zip(starts, starts[1:] + [tm])]
    ups = ((upv_ref, wuv_ref, lastv_ref), (upg_ref, wug_ref, lastg_ref))

    if not sample:
        first = (i % blocks_per_seq) == 0
        for half, (up_ref, _, _) in enumerate(ups):
            up_ref[0:hist, :] = jnp.where(first, 0.0, carry_ref[j, half])

    def up_proj(c):
        r0, n = chunks[c]
        hc = h_ref[r0:r0 + n, :]
        for up_ref, wu_ref, _ in ups:
            up_ref[hist + r0:hist + r0 + n, :] = jnp.dot(hc, wu_ref[...],
                                                         preferred_element_type=F32)

    def conv(up_ref, st_ref, wd_ref, bd_ref, row, lanes):
        width = lanes.stop - lanes.start
        w0 = wd_ref[0:1, lanes]
        w1 = wd_ref[1:2, lanes]
        w2 = wd_ref[2:3, lanes]
        n = FFN_ROWS // SUBLANES
        if sample:
            s0 = row // SUBLANES
            cur3 = up_ref[hist + row:hist + row + FFN_ROWS, lanes].reshape(n, SUBLANES, width)
            sl = lax.broadcasted_iota(jnp.int32, (n, SUBLANES, width), 1)
            st0 = st_ref[s0:s0 + n, 0:1, lanes]
            st1 = st_ref[s0:s0 + n, 1:2, lanes]
            m1 = jnp.where(sl == 0, st1, pltpu.roll(cur3, 1, 1))
            m2 = jnp.where(sl == 0, st0, jnp.where(sl == 1, st1, pltpu.roll(cur3, 2, 1)))
            out = w0 * m2 + w1 * m1 + w2 * cur3 + bd_ref[:, lanes]
            return out.reshape(FFN_ROWS, width)
        win3 = up_ref[row:row + FFN_ROWS + hist, lanes].reshape(n + 1, SUBLANES, width)
        top = lax.broadcasted_iota(jnp.int32, (n, SUBLANES, width), 1) == 0
        r0 = pltpu.roll(w0 * win3, 1, 1)
        s0 = jnp.concatenate([r0[0:1], jnp.where(top, r0[0:n], r0[1:n + 1])], axis=0)
        r1 = pltpu.roll(w1 * win3 + s0, 1, 1)
        s1 = jnp.where(top, r1[0:n], r1[1:n + 1])
        return (w2 * win3[1:n + 1] + s1 + bd_ref[:, lanes]).reshape(FFN_ROWS, width)

    def gate_down(c, kh):
        r0, n = chunks[c]
        kw = tf // FFN_K_SPLIT
        lanes = slice(kh * kw, (kh + 1) * kw)
        for r in range(0, n, FFN_ROWS):
            cv = conv(upv_ref, stv_ref if sample else None, wdv_ref, bdv_ref, r0 + r, lanes)
            cg = conv(upg_ref, stg_ref if sample else None, wdg_ref, bdg_ref, r0 + r, lanes)
            gate_ref[r:r + FFN_ROWS, lanes] = (cv * jax.nn.sigmoid(cv) * cg).astype(BF16)
        gv = gate_ref[0:n, lanes]
        for g in range(len(wdn_ref)):
            cols = slice(g * DOT_N, (g + 1) * DOT_N)
            y_ref[r0:r0 + n, cols] += jnp.dot(gv, wdn_ref[g][lanes, :],
                                              preferred_element_type=F32)

    up_proj(0)
    for c in range(len(chunks)):
        if c + 1 < len(chunks):
            up_proj(c + 1)
        for kh in range(FFN_K_SPLIT):
            gate_down(c, kh)

    for half, (up_ref, _, last_ref) in enumerate(ups):
        if sample:
            up3 = up_ref[hist:hist + tm, :].reshape(tm // SUBLANES, SUBLANES, tf)
            last_ref[...] = up3[:, SUBLANES - (FFN_CONV - 1):, :]
        else:
            tail = up_ref[tm:tm + hist, :]
            carry_ref[j, half] = tail
            last_ref[0] = tail


def _ffn(x, state, seq_len, params, tm, tf, rc):
    g_ffn, w_up, w_dw, b_dw, w_down = params
    rows = x.shape[0]
    sample = state is not None
    nj = D_FF // tf
    grid = (rows // tm, nj)
    row_map = lambda i, j: (i, 0)
    val_map = lambda i, j: (0, j)
    gate_map = lambda i, j: (0, nj + j)
    in_specs = [pl.BlockSpec((tm, D_MODEL), row_map)]
    args = [x]
    if sample:
        spt = tm // seq_len
        in_specs += [pl.BlockSpec((spt, FFN_CONV - 1, tf), lambda i, j: (i, 0, j)),
                     pl.BlockSpec((spt, FFN_CONV - 1, tf), lambda i, j: (i, 0, nj + j))]
        args += [state, state]
    in_specs += [pl.BlockSpec((1, D_MODEL), lambda i, j: (0, 0)),
                 pl.BlockSpec((D_MODEL, tf), val_map), pl.BlockSpec((D_MODEL, tf), gate_map),
                 pl.BlockSpec((FFN_CONV, tf), val_map), pl.BlockSpec((FFN_CONV, tf), gate_map),
                 pl.BlockSpec((1, tf), val_map), pl.BlockSpec((1, tf), gate_map)]
    in_specs += _panel_specs(tf, D_MODEL, index=lambda g: (lambda i, j: (j, g)), resident=False)
    args += [g_ffn, w_up, w_up, w_dw, w_dw, b_dw, b_dw] + [w_down] * (D_MODEL // DOT_N)
    scratch = [pltpu.VMEM((tm, D_MODEL), BF16),
               pltpu.VMEM((tm + SUBLANES, tf), F32), pltpu.VMEM((tm + SUBLANES, tf), F32),
               pltpu.VMEM((rc, tf), BF16)]
    if sample:
        last_specs = [pl.BlockSpec((spt, FFN_CONV - 1, tf), lambda i, j: (i, 0, j))] * 2
        last_shape = [jax.ShapeDtypeStruct((rows // seq_len, FFN_CONV - 1, D_FF), F32)] * 2
        blocks_per_seq = 1
    else:
        blocks_per_seq = seq_len // tm
        last_specs = [pl.BlockSpec((1, SUBLANES, tf), lambda i, j: (i, 0, j))] * 2
        last_shape = [jax.ShapeDtypeStruct((rows // tm, SUBLANES, D_FF), F32)] * 2
        scratch += [pltpu.VMEM((nj, 2, SUBLANES, tf), F32)]
    return pl.pallas_call(
        functools.partial(_ffn_body, sample, tm, tf, rc, blocks_per_seq),
        grid=grid, in_specs=in_specs,
        out_specs=[pl.BlockSpec((tm, D_MODEL), row_map)] + last_specs,
        out_shape=[jax.ShapeDtypeStruct((rows, D_MODEL), F32)] + last_shape,
        scratch_shapes=scratch,
        compiler_params=pltpu.CompilerParams(dimension_semantics=("arbitrary", "arbitrary"),
                                             vmem_limit_bytes=VMEM_LIMIT),
        name="ffn_sample" if sample else "ffn_prompt",
    )(*args)


def _ple_body(tm, *refs):
    refs = list(refs)
    x_ref, p_ref, g_ref = refs[:3]
    n_panels = D_MODEL // DOT_N
    wg_ref = refs[3:3 + n_panels]
    wp_ref = refs[3 + n_panels:3 + 2 * n_panels]
    gf_ref, y_ref, h_ref, gate_ref, proj_ref = refs[3 + 2 * n_panels:]
    _ple_pipeline(tm, x_ref, p_ref, g_ref, wg_ref, wp_ref, gf_ref, y_ref, h_ref, gate_ref, proj_ref)


def _ple_pipeline(tm, x_ref, p_ref, g_ref, wg_ref, wp_ref, gf_ref, y_ref, h_ref, gate_ref, proj_ref):
    n_chunks = tm // CHUNK
    never = pl.program_id(0) < 0
    pieces = CHUNK // ROWS
    anchors = {}

    def norm(c):
        slot = (c % PLE_SLOTS) * CHUNK
        for k in range(pieces):
            r = c * CHUNK + k * ROWS
            xv = x_ref[r:r + ROWS, :]
            ms = jnp.mean(xv * xv, axis=-1, keepdims=True)
            hv = xv * lax.rsqrt(ms + EPS) * g_ref[...]
            anchor = anchors.pop((c - PLE_SKEW, k), None)
            if anchor is not None:
                head = jnp.where(never, anchor, hv[:, :HEAD_DIM])
                hv = jnp.concatenate([head, hv[:, HEAD_DIM:]], axis=-1)
            h_ref[slot + k * ROWS:slot + (k + 1) * ROWS, :] = hv.astype(BF16)

    def projections(c):
        slot = (c % PLE_SLOTS) * CHUNK
        rows = slice(c * CHUNK, (c + 1) * CHUNK)
        pv = p_ref[rows, :].astype(BF16)
        hv = h_ref[slot:slot + CHUNK, :]
        for g in range(D_MODEL // DOT_N):
            cols = slice(g * DOT_N, (g + 1) * DOT_N)
            gate_ref[slot:slot + CHUNK, cols] = jnp.dot(hv, wg_ref[g][...],
                                                        preferred_element_type=F32)
            proj_ref[slot:slot + CHUNK, cols] = jnp.dot(pv, wp_ref[g][...],
                                                        preferred_element_type=F32)

    def finish(c):
        slot = (c % PLE_SLOTS) * CHUNK
        for k in range(pieces):
            r = c * CHUNK + k * ROWS
            s = slot + k * ROWS
            xv = x_ref[r:r + ROWS, :] + jax.nn.sigmoid(gate_ref[s:s + ROWS, :]) * proj_ref[s:s + ROWS, :]
            ms = jnp.mean(xv * xv, axis=-1, keepdims=True)
            yv = xv * lax.rsqrt(ms + EPS) * gf_ref[...]
            y_ref[r:r + ROWS, :] = yv
            anchors[(c, k)] = yv[:, :HEAD_DIM]

    norm(0)
    for c in range(n_chunks):
        projections(c)
        if c >= 1:
            finish(c - 1)
        if c + 1 < n_chunks:
            norm(c + 1)
    finish(n_chunks - 1)


def _ple(x, p, params, tm, name):
    g_ple, w_gate, w_proj, g_final = params
    rows = x.shape[0]
    row_map = lambda i: (i, 0)
    return pl.pallas_call(
        functools.partial(_ple_body, tm),
        grid=(rows // tm,),
        in_specs=([pl.BlockSpec((tm, D_MODEL), row_map), pl.BlockSpec((tm, D_PLE), row_map),
                   _resident((1, D_MODEL))] + _panel_specs(D_MODEL, D_MODEL)
                  + _panel_specs(D_PLE, D_MODEL) + [_resident((1, D_MODEL))]),
        out_specs=pl.BlockSpec((tm, D_MODEL), row_map),
        out_shape=jax.ShapeDtypeStruct((rows, D_MODEL), F32),
        scratch_shapes=[pltpu.VMEM((PLE_SLOTS * CHUNK, D_MODEL), BF16),
                        pltpu.VMEM((PLE_SLOTS * CHUNK, D_MODEL), F32),
                        pltpu.VMEM((PLE_SLOTS * CHUNK, D_MODEL), F32)],
        compiler_params=pltpu.CompilerParams(dimension_semantics=("arbitrary",),
                                             vmem_limit_bytes=VMEM_LIMIT),
        name=name,
    )(x, p, g_ple, *[w_gate] * (D_MODEL // DOT_N), *[w_proj] * (D_MODEL // DOT_N), g_final)


def _conv_coef(w):
    c = w.shape[1]
    off = HIST_A - (CONV_A - 1)
    n = SUBLANES * (CONV_TILES + 1)
    wp = jnp.zeros((n, c), w.dtype).at[off:off + CONV_A].set(w)
    wp_prev = jnp.concatenate([jnp.zeros((SUBLANES, c), w.dtype), wp[:n - SUBLANES]], axis=0)
    r = jnp.arange(SUBLANES)[:, None]
    q = jnp.arange(CONV_TILES)[None, :]
    d = SUBLANES * q + r
    s = jnp.arange(SUBLANES)[None, None, :, None]
    coef = jnp.where(s >= r[:, :, None, None], wp[d][:, :, None, :], wp_prev[d][:, :, None, :])
    return coef.reshape(SUBLANES * CONV_TILES * SUBLANES, c)


def kernel(x_prompt, x_sample, p_prompt, p_sample, state_conv_a, state_ffn_conv, g_mix, w_in, w_dw_a, b_dw_a, g_ln_a, b_ln_a, g_ln_v, b_ln_v, w_s, b_s, w_out, g_ffn, w_up, w_dw_f, b_dw_f, w_down, g_ple, w_ple_gate, w_ple_proj, g_final):
    depth = w_in.shape[0]
    assert depth == 1, "single-layer step"
    batch, seq, _ = x_prompt.shape
    dec_batch, dec_seq, _ = x_sample.shape
    row = lambda v: v.reshape(1, -1)

    def bias_rows(length):
        b = jnp.tile(b_s[0][:, :length], (1, CHUNK // length))
        return jnp.repeat(b.T, HEAD_DIM, axis=1)

    def mix_weights(length):
        if length == CHUNK:
            return w_s[0]
        sel = (jnp.arange(CHUNK)[:, None] % length == jnp.arange(length)[None, :]).astype(F32)
        return jnp.einsum('ti,hij,sj->hts', sel, w_s[0][:, :length, :length], sel,
                          precision=lax.Precision.HIGHEST)

    mixer_common = (row(g_mix[0]), w_in[0].astype(BF16), _conv_coef(w_dw_a[0]), row(b_dw_a[0]),
                    row(g_ln_a[0]), row(b_ln_a[0]), row(g_ln_v[0]), row(b_ln_v[0]))
    w_out_b = w_out[0].astype(BF16)
    ffn_params = (row(g_ffn[0]), w_up[0].astype(BF16), w_dw_f[0], row(b_dw_f[0]),
                  w_down[0].astype(BF16))
    ple_params = (row(g_ple[0]), w_ple_gate[0].astype(BF16), w_ple_proj[0].astype(BF16),
                  row(g_final))

    xp = x_prompt.reshape(batch * seq, D_MODEL)
    lp = min(seq, CHUNK)
    xp, conv_p, cv_p = _mixer(xp, None, seq,
                              mixer_common + (mix_weights(lp), bias_rows(lp), w_out_b), tm=256)
    xp, lastv_p, lastg_p = _ffn(xp, None, seq, ffn_params, tm=1024, tf=512, rc=256)
    yp = _ple(xp, p_prompt[0].reshape(batch * seq, D_PLE), ple_params, 1024, "ple_prompt")

    xs = x_sample.reshape(dec_batch * dec_seq, D_MODEL)
    ls = min(dec_seq, CHUNK)
    state_pad = jnp.pad(state_conv_a[0], ((0, 0), (HIST_A - (CONV_A - 1), 0), (0, 0)))
    xs, a_s, cv_s = _mixer(xs, state_pad, dec_seq,
                           mixer_common + (mix_weights(ls), bias_rows(ls), w_out_b), tm=256)
    xs, upv_s, upg_s = _ffn(xs, state_ffn_conv[0], dec_seq, ffn_params, tm=1024, tf=512, rc=256)
    ys = _ple(xs, p_sample[0].reshape(dec_batch * dec_seq, D_PLE), ple_params, 1024, "ple_sample")

    keep = FFN_CONV - 1
    conv_a_prompt = conv_p[:, HIST_A - (CONV_A - 1):]
    conv_a_sample = a_s
    per_seq = lastv_p.shape[0] // batch
    ffn_prompt = jnp.concatenate([lastv_p, lastg_p], axis=-1)[per_seq - 1::per_seq, SUBLANES - keep:]
    ffn_sample = jnp.concatenate([upv_s, upg_s], axis=-1)
    return (yp.reshape(batch, seq, D_MODEL), ys.reshape(dec_batch, dec_seq, D_MODEL),
            conv_a_prompt[None], conv_a_sample[None], ffn_prompt[None], ffn_sample[None],
            cv_p[None], cv_s.reshape(dec_batch, dec_seq, C_B)[None])
```

```python
import functools

import jax
import jax.numpy as jnp
from jax import lax
from jax.experimental import pallas as pl
from jax.experimental.pallas import tpu as pltpu

D_MODEL = 2048
HEAD_DIM = 128
C_A = D_MODEL // 2
C_B = D_MODEL - C_A
N_HEADS = C_B // HEAD_DIM
CONV_A = 31
HIST_A = 32
CONV_TILES = HIST_A // 8 + 1
CHUNK = 128
FFN_CONV = 3
D_FF = 5632
D_PLE = 256
EPS = 1e-6

SUBLANES = 8
ROWS = 32
LANES_EW = 128
FFN_ROWS = 128
FFN_K_SPLIT = 2
DOT_N = 512
MIXER_SLOTS = 2
PLE_SKEW = 2
PLE_SLOTS = 3
VMEM_LIMIT = 56 * 1024 * 1024

F32 = jnp.float32
BF16 = jnp.bfloat16


def _resident(shape):
    return pl.BlockSpec(shape, lambda *_: (0,) * len(shape), pipeline_mode=pl.Buffered(1))


def _panel_specs(k, n, index=None, resident=True):
    specs = []
    for g in range(n // DOT_N):
        imap = index(g) if index is not None else (lambda *_, g=g: (0, g))
        specs.append(pl.BlockSpec((k, DOT_N), imap,
                                  pipeline_mode=pl.Buffered(1) if resident else None))
    return specs


def _aligned(x, m):
    return x if isinstance(x, int) else pl.multiple_of(x, m)


def _row_loop(n_rows, step, fn):
    def body(c, carry):
        fn(pl.multiple_of(c * step, step))
        return carry
    lax.fori_loop(0, n_rows // step, body, 0, unroll=16)


def _rms_to(x_ref, g_ref, dst_ref, n_rows):
    def piece(r):
        xv = x_ref[pl.ds(r, ROWS), :]
        ms = jnp.mean(xv * xv, axis=-1, keepdims=True)
        dst_ref[pl.ds(r, ROWS), :] = (xv * lax.rsqrt(ms + EPS) * g_ref[...]).astype(BF16)
    _row_loop(n_rows, ROWS, piece)


def _head_ln(x, g, b):
    outs = []
    for h in range(x.shape[-1] // HEAD_DIM):
        xh = x[:, h * HEAD_DIM:(h + 1) * HEAD_DIM]
        mu = jnp.mean(xh, axis=-1, keepdims=True)
        xc = xh - mu
        var = jnp.mean(xc * xc, axis=-1, keepdims=True)
        outs.append(xc * lax.rsqrt(var + EPS))
    return jnp.concatenate(outs, axis=-1) * g + b


def _conv31(win3, n_out, coef_ref, lanes):
    acc = None
    for r in range(SUBLANES):
        inner = None
        for q in range(CONV_TILES):
            if q == 0 and r < HIST_A - (CONV_A - 1):
                continue
            k = (r * CONV_TILES + q) * SUBLANES
            term = coef_ref[k:k + SUBLANES, lanes] * win3[q:q + n_out]
            inner = term if inner is None else inner + term
        if r:
            inner = pltpu.roll(inner, SUBLANES - r, 1)
        acc = inner if acc is None else acc + inner
    return acc


def _mixer_body(sample, tm, seq_len, *refs):
    refs = list(refs)
    x_ref = refs.pop(0)
    st_ref = refs.pop(0) if sample else None
    gmix_ref = refs.pop(0)
    win_ref = [refs.pop(0) for _ in range((2 * C_A + 2 * C_B) // DOT_N)]
    (coef_ref, bdw_ref, glna_ref, blna_ref, glnv_ref, blnv_ref, ws_ref,
     bsb_ref) = [refs.pop(0) for _ in range(8)]
    wout_ref = [refs.pop(0) for _ in range(D_MODEL // DOT_N)]
    y_ref, aout_ref, vout_ref, h_ref, z_ref, cat_ref, a_ref, vn_ref, wm_ref = refs
    n_chunks = tm // CHUNK
    lane_groups = [slice(g * LANES_EW, (g + 1) * LANES_EW) for g in range(C_A // LANES_EW)]

    def norm(c):
        for r in range(c * CHUNK, (c + 1) * CHUNK, ROWS):
            xv = x_ref[r:r + ROWS, :]
            ms = jnp.mean(xv * xv, axis=-1, keepdims=True)
            h_ref[r:r + ROWS, :] = (xv * lax.rsqrt(ms + EPS) * gmix_ref[...]).astype(BF16)

    ti = lax.broadcasted_iota(jnp.int32, (CHUNK, CHUNK), 0)
    si = lax.broadcasted_iota(jnp.int32, (CHUNK, CHUNK), 1)
    if seq_len >= CHUNK:
        mask = si <= ti
    else:
        mask = ((si // seq_len) == (ti // seq_len)) & ((si % seq_len) <= (ti % seq_len))
    for h in range(N_HEADS):
        wm_ref[h] = jnp.where(mask, ws_ref[h], 0.0).astype(BF16)

    if not sample:
        t = pl.program_id(1)

        @pl.when(t == 0)
        def _():
            a_ref[0:HIST_A, :] = jnp.zeros((HIST_A, C_A), F32)

        @pl.when(t > 0)
        def _():
            a_ref[0:HIST_A, :] = a_ref[tm:tm + HIST_A, :]

    a_base = 0 if sample else HIST_A

    def slot_rows(c):
        lo = (c % MIXER_SLOTS) * CHUNK
        return slice(lo, lo + CHUNK)

    def in_proj(c):
        rows = slice(c * CHUNK, (c + 1) * CHUNK)
        hv = h_ref[rows, :]
        for g in range(len(win_ref)):
            z_ref[slot_rows(c), g * DOT_N:(g + 1) * DOT_N] = jnp.dot(
                hv, win_ref[g][...], preferred_element_type=F32)

    def out_proj(c):
        rows = slice(c * CHUNK, (c + 1) * CHUNK)
        cv = cat_ref[slot_rows(c), :]
        for g in range(len(wout_ref)):
            cols = slice(g * DOT_N, (g + 1) * DOT_N)
            y_ref[rows, cols] = x_ref[rows, cols] + jnp.dot(cv, wout_ref[g][...],
                                                            preferred_element_type=F32)

    def elementwise(c):
        r0 = c * CHUNK
        s0 = slot_rows(c).start - r0
        for r in range(r0, r0 + CHUNK, ROWS):
            for lanes in lane_groups:
                zv = z_ref[s0 + r:s0 + r + ROWS, lanes]
                zg = z_ref[s0 + r:s0 + r + ROWS, C_A + lanes.start:C_A + lanes.stop]
                a = zv * jax.nn.sigmoid(zg)
                a_ref[a_base + r:a_base + r + ROWS, lanes] = a
                if sample:
                    keep = CONV_A - 1 - seq_len
                    for q in range(ROWS // seq_len):
                        s = r // seq_len + q
                        aout_ref[s, 0:keep, lanes] = st_ref[s, HIST_A - keep:HIST_A, lanes]
                        aout_ref[s, keep:CONV_A - 1, lanes] = a[q * seq_len:(q + 1) * seq_len]
        for r in range(r0, r0 + CHUNK, ROWS):
            for lanes in lane_groups:
                width = lanes.stop - lanes.start
                if sample:
                    outs = []
                    for rs in range(r, r + ROWS, seq_len):
                        hist = st_ref[rs // seq_len, :, lanes].reshape(
                            CONV_TILES - 1, SUBLANES, width)
                        new = a_ref[rs:rs + seq_len, lanes].reshape(1, SUBLANES, width)
                        win3 = jnp.concatenate([hist, new], axis=0)
                        outs.append(_conv31(win3, 1, coef_ref, lanes))
                    conv = jnp.concatenate(outs, axis=0).reshape(ROWS, width)
                else:
                    n_out = ROWS // SUBLANES
                    win3 = a_ref[r:r + ROWS + HIST_A, lanes].reshape(
                        n_out + CONV_TILES - 1, SUBLANES, width)
                    conv = _conv31(win3, n_out, coef_ref, lanes).reshape(ROWS, width)
                conv = conv + bdw_ref[:, lanes]
                yv = _head_ln(conv, glna_ref[:, lanes], blna_ref[:, lanes])
                cat_ref[s0 + r:s0 + r + ROWS, lanes] = (yv * jax.nn.sigmoid(yv)).astype(BF16)

        v_off = 2 * C_A + C_B
        for r in range(r0, r0 + CHUNK, ROWS):
            for lanes in lane_groups:
                v = jax.nn.gelu(z_ref[s0 + r:s0 + r + ROWS, v_off + lanes.start:v_off + lanes.stop])
                vn = _head_ln(v, glnv_ref[:, lanes], blnv_ref[:, lanes])
                vn_ref[r - r0:r - r0 + ROWS, lanes] = vn.astype(BF16)
                if sample:
                    vout_ref[r:r + ROWS, lanes] = vn
                else:
                    vout_ref[0, r - r0:r - r0 + ROWS, lanes] = vn
        for h in range(N_HEADS):
            lo = h * HEAD_DIM
            mixed = jnp.dot(wm_ref[h], vn_ref[:, lo:lo + HEAD_DIM], preferred_element_type=F32)
            mixed = mixed + bsb_ref[:, lo:lo + HEAD_DIM]
            u = jax.nn.gelu(z_ref[slot_rows(c), 2 * C_A + lo:2 * C_A + lo + HEAD_DIM])
            cat_ref[slot_rows(c), C_A + lo:C_A + lo + HEAD_DIM] = (u * mixed).astype(BF16)

    norm(0)
    in_proj(0)
    for c in range(n_chunks):
        if c + 1 < n_chunks:
            norm(c + 1)
            in_proj(c + 1)
        elementwise(c)
        out_proj(c)

    if not sample:
        aout_ref[0] = a_ref[tm:tm + HIST_A, :]


def _mixer(x, state, seq_len, params, tm):
    (g_mix, w_in, w_dw, b_dw, g_ln_a, b_ln_a, g_ln_v, b_ln_v, w_s, bias_rows, w_out) = params
    rows = x.shape[0]
    sample = state is not None
    vec = lambda n: _resident((1, n))
    common_in = ([vec(D_MODEL)] + _panel_specs(D_MODEL, 2 * C_A + 2 * C_B)
                 + [_resident((SUBLANES * CONV_TILES * SUBLANES, C_A)),
                    vec(C_A), vec(C_A), vec(C_A), vec(C_B), vec(C_B),
                    _resident((N_HEADS, CHUNK, CHUNK)), _resident((CHUNK, C_B))]
                 + _panel_specs(C_A + C_B, D_MODEL))
    ring = MIXER_SLOTS * CHUNK
    common_scratch = [pltpu.VMEM((tm, D_MODEL), BF16),
                      pltpu.VMEM((ring, 2 * C_A + 2 * C_B), F32),
                      pltpu.VMEM((ring, C_A + C_B), BF16)]
    tail_scratch = [pltpu.VMEM((CHUNK, C_B), BF16),
                    pltpu.VMEM((N_HEADS, CHUNK, CHUNK), BF16)]
    if sample:
        n_seq = rows // seq_len
        spt = tm // seq_len
        grid = (rows // tm,)
        row_map = lambda i: (i, 0)
        in_specs = [pl.BlockSpec((tm, D_MODEL), row_map),
                    pl.BlockSpec((spt, HIST_A, C_A), lambda i: (i, 0, 0),
                                 pipeline_mode=pl.Buffered(1))] + common_in
        out_specs = [pl.BlockSpec((tm, D_MODEL), row_map),
                     pl.BlockSpec((spt, CONV_A - 1, C_A), lambda i: (i, 0, 0)),
                     pl.BlockSpec((tm, C_B), row_map)]
        out_shape = [jax.ShapeDtypeStruct((rows, D_MODEL), F32),
                     jax.ShapeDtypeStruct((n_seq, CONV_A - 1, C_A), F32),
                     jax.ShapeDtypeStruct((rows, C_B), F32)]
        scratch = common_scratch + [pltpu.VMEM((tm, C_A), F32)] + tail_scratch
        args = (x, state)
        sem = ("arbitrary",)
        del n_seq
    else:
        n_seq = rows // seq_len
        nt = seq_len // tm
        grid = (n_seq, nt)
        row_map = lambda b, t: (b * nt + t, 0)
        in_specs = [pl.BlockSpec((tm, D_MODEL), row_map)] + common_in
        out_specs = [pl.BlockSpec((tm, D_MODEL), row_map),
                     pl.BlockSpec((1, HIST_A, C_A), lambda b, t: (b, 0, 0)),
                     pl.BlockSpec((1, CHUNK, C_B), lambda b, t: (b, 0, 0))]
        out_shape = [jax.ShapeDtypeStruct((rows, D_MODEL), F32),
                     jax.ShapeDtypeStruct((n_seq, HIST_A, C_A), F32),
                     jax.ShapeDtypeStruct((n_seq, CHUNK, C_B), F32)]
        scratch = common_scratch + [pltpu.VMEM((tm + HIST_A, C_A), F32)] + tail_scratch
        args = (x,)
        sem = ("arbitrary", "arbitrary")
    return pl.pallas_call(
        functools.partial(_mixer_body, sample, tm, seq_len),
        grid=grid, in_specs=in_specs, out_specs=out_specs, out_shape=out_shape,
        scratch_shapes=scratch,
        compiler_params=pltpu.CompilerParams(dimension_semantics=sem,
                                             vmem_limit_bytes=VMEM_LIMIT),
        name="mixer_sample" if sample else "mixer_prompt",
    )(*args, g_mix, *[w_in] * (w_in.shape[1] // DOT_N), w_dw, b_dw, g_ln_a, b_ln_a, g_ln_v,
      b_ln_v, w_s, bias_rows, *[w_out] * (w_out.shape[1] // DOT_N))


def _ffn_body(sample, tm, tf, rc, blocks_per_seq, *refs):
    refs = list(refs)
    x_ref = refs.pop(0)
    stv_ref, stg_ref = (refs.pop(0), refs.pop(0)) if sample else (None, None)
    (g_ref, wuv_ref, wug_ref, wdv_ref, wdg_ref, bdv_ref,
     bdg_ref) = [refs.pop(0) for _ in range(7)]
    wdn_ref = [refs.pop(0) for _ in range(D_MODEL // DOT_N)]
    y_ref, lastv_ref, lastg_ref, h_ref, upv_ref, upg_ref, gate_ref = refs[:7]
    carry_ref = None if sample else refs[7]
    i = pl.program_id(0)
    j = pl.program_id(1)

    @pl.when(j == 0)
    def _():
        _rms_to(x_ref, g_ref, h_ref, tm)
        y_ref[...] = x_ref[...]
        if not sample:
            @pl.when(i == 0)
            def _():
                carry_ref[...] = jnp.zeros(carry_ref.shape, F32)

    hist = SUBLANES
    starts = list(range(0, tm - rc, rc)) + [tm - rc, tm - rc // 2]
    chunks = [(r0, r1 - r0) for r0, r1 in zip(starts, starts[1:] + [tm])]
    ups = ((upv_ref, wuv_ref, lastv_ref), (upg_ref, wug_ref, lastg_ref))

    if not sample:
        first = (i % blocks_per_seq) == 0
        for half, (up_ref, _, _) in enumerate(ups):
            up_ref[0:hist, :] = jnp.where(first, 0.0, carry_ref[j, half])

    def up_proj(c):
        r0, n = chunks[c]
        hc = h_ref[r0:r0 + n, :]
        for up_ref, wu_ref, _ in ups:
            up_ref[hist + r0:hist + r0 + n, :] = jnp.dot(hc, wu_ref[...],
                                                         preferred_element_type=F32)

    def conv(up_ref, st_ref, wd_ref, bd_ref, row, lanes):
        width = lanes.stop - lanes.start
        w0 = wd_ref[0:1, lanes]
        w1 = wd_ref[1:2, lanes]
        w2 = wd_ref[2:3, lanes]
        n = FFN_ROWS // SUBLANES
        if sample:
            s0 = row // SUBLANES
            cur3 = up_ref[hist + row:hist + row + FFN_ROWS, lanes].reshape(n, SUBLANES, width)
            sl = lax.broadcasted_iota(jnp.int32, (n, SUBLANES, width), 1)
            st0 = st_ref[s0:s0 + n, 0:1, lanes]
            st1 = st_ref[s0:s0 + n, 1:2, lanes]
            m1 = jnp.where(sl == 0, st1, pltpu.roll(cur3, 1, 1))
            m2 = jnp.where(sl == 0, st0, jnp.where(sl == 1, st1, pltpu.roll(cur3, 2, 1)))
            out = w0 * m2 + w1 * m1 + w2 * cur3 + bd_ref[:, lanes]
            return out.reshape(FFN_ROWS, width)
        win3 = up_ref[row:row + FFN_ROWS + hist, lanes].reshape(n + 1, SUBLANES, width)
        top = lax.broadcasted_iota(jnp.int32, (n, SUBLANES, width), 1) == 0
        r0 = pltpu.roll(w0 * win3, 1, 1)
        s0 = jnp.concatenate([r0[0:1], jnp.where(top, r0[0:n], r0[1:n + 1])], axis=0)
        r1 = pltpu.roll(w1 * win3 + s0, 1, 1)
        s1 = jnp.where(top, r1[0:n], r1[1:n + 1])
        return (w2 * win3[1:n + 1] + s1 + bd_ref[:, lanes]).reshape(FFN_ROWS, width)

    def gate_down(c, kh):
        r0, n = chunks[c]
        kw = tf // FFN_K_SPLIT
        lanes = slice(kh * kw, (kh + 1) * kw)
        for r in range(0, n, FFN_ROWS):
            cv = conv(upv_ref, stv_ref if sample else None, wdv_ref, bdv_ref, r0 + r, lanes)
            cg = conv(upg_ref, stg_ref if sample else None, wdg_ref, bdg_ref, r0 + r, lanes)
            gate_ref[r:r + FFN_ROWS, lanes] = (cv * jax.nn.sigmoid(cv) * cg).astype(BF16)
        gv = gate_ref[0:n, lanes]
        for g in range(len(wdn_ref)):
            cols = slice(g * DOT_N, (g + 1) * DOT_N)
            y_ref[r0:r0 + n, cols] += jnp.dot(gv, wdn_ref[g][lanes, :],
                                              preferred_element_type=F32)

    up_proj(0)
    for c in range(len(chunks)):
        if c + 1 < len(chunks):
            up_proj(c + 1)
        for kh in range(FFN_K_SPLIT):
            gate_down(c, kh)

    for half, (up_ref, _, last_ref) in enumerate(ups):
        if sample:
            up3 = up_ref[hist:hist + tm, :].reshape(tm // SUBLANES, SUBLANES, tf)
            last_ref[...] = up3[:, SUBLANES - (FFN_CONV - 1):, :]
        else:
            tail = up_ref[tm:tm + hist, :]
            carry_ref[j, half] = tail
            last_ref[0] = tail


def _ffn(x, state, seq_len, params, tm, tf, rc):
    g_ffn, w_up, w_dw, b_dw, w_down = params
    rows = x.shape[0]
    sample = state is not None
    nj = D_FF // tf
    grid = (rows // tm, nj)
    row_map = lambda i, j: (i, 0)
    val_map = lambda i, j: (0, j)
    gate_map = lambda i, j: (0, nj + j)
    in_specs = [pl.BlockSpec((tm, D_MODEL), row_map)]
    args = [x]
    if sample:
        spt = tm // seq_len
        in_specs += [pl.BlockSpec((spt, FFN_CONV - 1, tf), lambda i, j: (i, 0, j)),
                     pl.BlockSpec((spt, FFN_CONV - 1, tf), lambda i, j: (i, 0, nj + j))]
        args += [state, state]
    in_specs += [pl.BlockSpec((1, D_MODEL), lambda i, j: (0, 0)),
                 pl.BlockSpec((D_MODEL, tf), val_map), pl.BlockSpec((D_MODEL, tf), gate_map),
                 pl.BlockSpec((FFN_CONV, tf), val_map), pl.BlockSpec((FFN_CONV, tf), gate_map),
                 pl.BlockSpec((1, tf), val_map), pl.BlockSpec((1, tf), gate_map)]
    in_specs += _panel_specs(tf, D_MODEL, index=lambda g: (lambda i, j: (j, g)), resident=False)
    args += [g_ffn, w_up, w_up, w_dw, w_dw, b_dw, b_dw] + [w_down] * (D_MODEL // DOT_N)
    scratch = [pltpu.VMEM((tm, D_MODEL), BF16),
               pltpu.VMEM((tm + SUBLANES, tf), F32), pltpu.VMEM((tm + SUBLANES, tf), F32),
               pltpu.VMEM((rc, tf), BF16)]
    if sample:
        last_specs = [pl.BlockSpec((spt, FFN_CONV - 1, tf), lambda i, j: (i, 0, j))] * 2
        last_shape = [jax.ShapeDtypeStruct((rows // seq_len, FFN_CONV - 1, D_FF), F32)] * 2
        blocks_per_seq = 1
    else:
        blocks_per_seq = seq_len // tm
        last_specs = [pl.BlockSpec((1, SUBLANES, tf), lambda i, j: (i, 0, j))] * 2
        last_shape = [jax.ShapeDtypeStruct((rows // tm, SUBLANES, D_FF), F32)] * 2
        scratch += [pltpu.VMEM((nj, 2, SUBLANES, tf), F32)]
    return pl.pallas_call(
        functools.partial(_ffn_body, sample, tm, tf, rc, blocks_per_seq),
        grid=grid, in_specs=in_specs,
        out_specs=[pl.BlockSpec((tm, D_MODEL), row_map)] + last_specs,
        out_shape=[jax.ShapeDtypeStruct((rows, D_MODEL), F32)] + last_shape,
        scratch_shapes=scratch,
        compiler_params=pltpu.CompilerParams(dimension_semantics=("arbitrary", "arbitrary"),
                                             vmem_limit_bytes=VMEM_LIMIT),
        name="ffn_sample" if sample else "ffn_prompt",
    )(*args)


def _ple_body(tm, *refs):
    refs = list(refs)
    x_ref, p_ref, g_ref = refs[:3]
    n_panels = D_MODEL // DOT_N
    wg_ref = refs[3:3 + n_panels]
    wp_ref = refs[3 + n_panels:3 + 2 * n_panels]
    gf_ref, y_ref, h_ref, gate_ref, proj_ref = refs[3 + 2 * n_panels:]
    _ple_pipeline(tm, x_ref, p_ref, g_ref, wg_ref, wp_ref, gf_ref, y_ref, h_ref, gate_ref, proj_ref)


def _ple_pipeline(tm, x_ref, p_ref, g_ref, wg_ref, wp_ref, gf_ref, y_ref, h_ref, gate_ref, proj_ref):
    n_chunks = tm // CHUNK
    never = pl.program_id(0) < 0
    pieces = CHUNK // ROWS
    anchors = {}

    def norm(c):
        slot = (c % PLE_SLOTS) * CHUNK
        for k in range(pieces):
            r = c * CHUNK + k * ROWS
            xv = x_ref[r:r + ROWS, :]
            ms = jnp.mean(xv * xv, axis=-1, keepdims=True)
            hv = xv * lax.rsqrt(ms + EPS) * g_ref[...]
            anchor = anchors.pop((c - PLE_SKEW, k), None)
            if anchor is not None:
                head = jnp.where(never, anchor, hv[:, :HEAD_DIM])
                hv = jnp.concatenate([head, hv[:, HEAD_DIM:]], axis=-1)
            h_ref[slot + k * ROWS:slot + (k + 1) * ROWS, :] = hv.astype(BF16)

    def projections(c):
        slot = (c % PLE_SLOTS) * CHUNK
        rows = slice(c * CHUNK, (c + 1) * CHUNK)
        pv = p_ref[rows, :].astype(BF16)
        hv = h_ref[slot:slot + CHUNK, :]
        for g in range(D_MODEL // DOT_N):
            cols = slice(g * DOT_N, (g + 1) * DOT_N)
            gate_ref[slot:slot + CHUNK, cols] = jnp.dot(hv, wg_ref[g][...],
                                                        preferred_element_type=F32)
            proj_ref[slot:slot + CHUNK, cols] = jnp.dot(pv, wp_ref[g][...],
                                                        preferred_element_type=F32)

    def finish(c):
        slot = (c % PLE_SLOTS) * CHUNK
        for k in range(pieces):
            r = c * CHUNK + k * ROWS
            s = slot + k * ROWS
            xv = x_ref[r:r + ROWS, :] + jax.nn.sigmoid(gate_ref[s:s + ROWS, :]) * proj_ref[s:s + ROWS, :]
            ms = jnp.mean(xv * xv, axis=-1, keepdims=True)
            yv = xv * lax.rsqrt(ms + EPS) * gf_ref[...]
            y_ref[r:r + ROWS, :] = yv
            anchors[(c, k)] = yv[:, :HEAD_DIM]

    norm(0)
    for c in range(n_chunks):
        projections(c)
        if c >= 1:
            finish(c - 1)
        if c + 1 < n_chunks:
            norm(c + 1)
    finish(n_chunks - 1)


def _ple(x, p, params, tm, name):
    g_ple, w_gate, w_proj, g_final = params
    rows = x.shape[0]
    row_map = lambda i: (i, 0)
    return pl.pallas_call(
        functools.partial(_ple_body, tm),
        grid=(rows // tm,),
        in_specs=([pl.BlockSpec((tm, D_MODEL), row_map), pl.BlockSpec((tm, D_PLE), row_map),
                   _resident((1, D_MODEL))] + _panel_specs(D_MODEL, D_MODEL)
                  + _panel_specs(D_PLE, D_MODEL) + [_resident((1, D_MODEL))]),
        out_specs=pl.BlockSpec((tm, D_MODEL), row_map),
        out_shape=jax.ShapeDtypeStruct((rows, D_MODEL), F32),
        scratch_shapes=[pltpu.VMEM((PLE_SLOTS * CHUNK, D_MODEL), BF16),
                        pltpu.VMEM((PLE_SLOTS * CHUNK, D_MODEL), F32),
                        pltpu.VMEM((PLE_SLOTS * CHUNK, D_MODEL), F32)],
        compiler_params=pltpu.CompilerParams(dimension_semantics=("arbitrary",),
                                             vmem_limit_bytes=VMEM_LIMIT),
        name=name,
    )(x, p, g_ple, *[w_gate] * (D_MODEL // DOT_N), *[w_proj] * (D_MODEL // DOT_N), g_final)


def _conv_coef(w):
    c = w.shape[1]
    off = HIST_A - (CONV_A - 1)
    n = SUBLANES * (CONV_TILES + 1)
    wp = jnp.zeros((n, c), w.dtype).at[off:off + CONV_A].set(w)
    wp_prev = jnp.concatenate([jnp.zeros((SUBLANES, c), w.dtype), wp[:n - SUBLANES]], axis=0)
    r = jnp.arange(SUBLANES)[:, None]
    q = jnp.arange(CONV_TILES)[None, :]
    d = SUBLANES * q + r
    s = jnp.arange(SUBLANES)[None, None, :, None]
    coef = jnp.where(s >= r[:, :, None, None], wp[d][:, :, None, :], wp_prev[d][:, :, None, :])
    return coef.reshape(SUBLANES * CONV_TILES * SUBLANES, c)


def kernel(x_prompt, x_sample, p_prompt, p_sample, state_conv_a, state_ffn_conv, g_mix, w_in, w_dw_a, b_dw_a, g_ln_a, b_ln_a, g_ln_v, b_ln_v, w_s, b_s, w_out, g_ffn, w_up, w_dw_f, b_dw_f, w_down, g_ple, w_ple_gate, w_ple_proj, g_final):
    depth = w_in.shape[0]
    assert depth == 1, "single-layer step"
    batch, seq, _ = x_prompt.shape
    dec_batch, dec_seq, _ = x_sample.shape
    row = lambda v: v.reshape(1, -1)

    def bias_rows(length):
        b = jnp.tile(b_s[0][:, :length], (1, CHUNK // length))
        return jnp.repeat(b.T, HEAD_DIM, axis=1)

    def mix_weights(length):
        if length == CHUNK:
            return w_s[0]
        sel = (jnp.arange(CHUNK)[:, None] % length == jnp.arange(length)[None, :]).astype(F32)
        return jnp.einsum('ti,hij,sj->hts', sel, w_s[0][:, :length, :length], sel,
                          precision=lax.Precision.HIGHEST)

    mixer_common = (row(g_mix[0]), w_in[0].astype(BF16), _conv_coef(w_dw_a[0]), row(b_dw_a[0]),
                    row(g_ln_a[0]), row(b_ln_a[0]), row(g_ln_v[0]), row(b_ln_v[0]))
    w_out_b = w_out[0].astype(BF16)
    ffn_params = (row(g_ffn[0]), w_up[0].astype(BF16), w_dw_f[0], row(b_dw_f[0]),
                  w_down[0].astype(BF16))
    ple_params = (row(g_ple[0]), w_ple_gate[0].astype(BF16), w_ple_proj[0].astype(BF16),
                  row(g_final))

    xp = x_prompt.reshape(batch * seq, D_MODEL)
    lp = min(seq, CHUNK)
    xp, conv_p, cv_p = _mixer(xp, None, seq,
                              mixer_common + (mix_weights(lp), bias_rows(lp), w_out_b), tm=256)
    xp, lastv_p, lastg_p = _ffn(xp, None, seq, ffn_params, tm=1024, tf=512, rc=256)
    yp = _ple(xp, p_prompt[0].reshape(batch * seq, D_PLE), ple_params, 1024, "ple_prompt")

    xs = x_sample.reshape(dec_batch * dec_seq, D_MODEL)
    ls = min(dec_seq, CHUNK)
    state_pad = jnp.pad(state_conv_a[0], ((0, 0), (HIST_A - (CONV_A - 1), 0), (0, 0)))
    xs, a_s, cv_s = _mixer(xs, state_pad, dec_seq,
                           mixer_common + (mix_weights(ls), bias_rows(ls), w_out_b), tm=256)
    xs, upv_s, upg_s = _ffn(xs, state_ffn_conv[0], dec_seq, ffn_params, tm=1024, tf=512, rc=256)
    ys = _ple(xs, p_sample[0].reshape(dec_batch * dec_seq, D_PLE), ple_params, 1024, "ple_sample")

    keep = FFN_CONV - 1
    conv_a_prompt = conv_p[:, HIST_A - (CONV_A - 1):]
    conv_a_sample = a_s
    per_seq = lastv_p.shape[0] // batch
    ffn_prompt = jnp.concatenate([lastv_p, lastg_p], axis=-1)[per_seq - 1::per_seq, SUBLANES - keep:]
    ffn_sample = jnp.concatenate([upv_s, upg_s], axis=-1)
    return (yp.reshape(batch, seq, D_MODEL), ys.reshape(dec_batch, dec_seq, D_MODEL),
            conv_a_prompt[None], conv_a_sample[None], ffn_prompt[None], ffn_sample[None],
            cv_p[None], cv_s.reshape(dec_batch, dec_seq, C_B)[None])
```

```python
import functools

import jax
import jax.numpy as jnp
from jax import lax
from jax.experimental import pallas as pl
from jax.experimental.pallas import tpu as pltpu

D_MODEL = 2048
HEAD_DIM = 128
C_A = D_MODEL // 2
C_B = D_MODEL - C_A
N_HEADS = C_B // HEAD_DIM
CONV_A = 31
HIST_A = 32
CONV_TILES = HIST_A // 8 + 1
CHUNK = 128
FFN_CONV = 3
D_FF = 5632
D_PLE = 256
EPS = 1e-6

SUBLANES = 8
ROWS = 32
LANES_EW = 128
FFN_ROWS = 256
FFN_K_SPLIT = 2
DOT_N = 512
MIXER_SLOTS = 2
PLE_SKEW = 2
PLE_SLOTS = 3
VMEM_LIMIT = 56 * 1024 * 1024

F32 = jnp.float32
BF16 = jnp.bfloat16


def _resident(shape):
    return pl.BlockSpec(shape, lambda *_: (0,) * len(shape), pipeline_mode=pl.Buffered(1))


def _panel_specs(k, n, index=None, resident=True):
    specs = []
    for g in range(n // DOT_N):
        imap = index(g) if index is not None else (lambda *_, g=g: (0, g))
        specs.append(pl.BlockSpec((k, DOT_N), imap,
                                  pipeline_mode=pl.Buffered(1) if resident else None))
    return specs


def _aligned(x, m):
    return x if isinstance(x, int) else pl.multiple_of(x, m)


def _row_loop(n_rows, step, fn):
    def body(c, carry):
        fn(pl.multiple_of(c * step, step))
        return carry
    lax.fori_loop(0, n_rows // step, body, 0, unroll=16)


def _rms_to(x_ref, g_ref, dst_ref, n_rows):
    def piece(r):
        xv = x_ref[pl.ds(r, ROWS), :]
        ms = jnp.mean(xv * xv, axis=-1, keepdims=True)
        dst_ref[pl.ds(r, ROWS), :] = (xv * lax.rsqrt(ms + EPS) * g_ref[...]).astype(BF16)
    _row_loop(n_rows, ROWS, piece)


def _head_ln(x, g, b):
    outs = []
    for h in range(x.shape[-1] // HEAD_DIM):
        xh = x[:, h * HEAD_DIM:(h + 1) * HEAD_DIM]
        mu = jnp.mean(xh, axis=-1, keepdims=True)
        xc = xh - mu
        var = jnp.mean(xc * xc, axis=-1, keepdims=True)
        outs.append(xc * lax.rsqrt(var + EPS))
    return jnp.concatenate(outs, axis=-1) * g + b


def _conv31(win3, n_out, coef_ref, lanes):
    acc = None
    for r in range(SUBLANES):
        inner = None
        for q in range(CONV_TILES):
            if q == 0 and r < HIST_A - (CONV_A - 1):
                continue
            k = (r * CONV_TILES + q) * SUBLANES
            term = coef_ref[k:k + SUBLANES, lanes] * win3[q:q + n_out]
            inner = term if inner is None else inner + term
        if r:
            inner = pltpu.roll(inner, SUBLANES - r, 1)
        acc = inner if acc is None else acc + inner
    return acc


def _mixer_body(sample, tm, seq_len, *refs):
    refs = list(refs)
    x_ref = refs.pop(0)
    st_ref = refs.pop(0) if sample else None
    gmix_ref = refs.pop(0)
    win_ref = [refs.pop(0) for _ in range((2 * C_A + 2 * C_B) // DOT_N)]
    (coef_ref, bdw_ref, glna_ref, blna_ref, glnv_ref, blnv_ref, ws_ref,
     bsb_ref) = [refs.pop(0) for _ in range(8)]
    wout_ref = [refs.pop(0) for _ in range(D_MODEL // DOT_N)]
    y_ref, aout_ref, vout_ref, h_ref, z_ref, cat_ref, a_ref, vn_ref, wm_ref = refs
    n_chunks = tm // CHUNK
    lane_groups = [slice(g * LANES_EW, (g + 1) * LANES_EW) for g in range(C_A // LANES_EW)]

    def norm(c):
        for r in range(c * CHUNK, (c + 1) * CHUNK, ROWS):
            xv = x_ref[r:r + ROWS, :]
            ms = jnp.mean(xv * xv, axis=-1, keepdims=True)
            h_ref[r:r + ROWS, :] = (xv * lax.rsqrt(ms + EPS) * gmix_ref[...]).astype(BF16)

    ti = lax.broadcasted_iota(jnp.int32, (CHUNK, CHUNK), 0)
    si = lax.broadcasted_iota(jnp.int32, (CHUNK, CHUNK), 1)
    if seq_len >= CHUNK:
        mask = si <= ti
    else:
        mask = ((si // seq_len) == (ti // seq_len)) & ((si % seq_len) <= (ti % seq_len))
    for h in range(N_HEADS):
        wm_ref[h] = jnp.where(mask, ws_ref[h], 0.0).astype(BF16)

    if not sample:
        t = pl.program_id(1)

        @pl.when(t == 0)
        def _():
            a_ref[0:HIST_A, :] = jnp.zeros((HIST_A, C_A), F32)

        @pl.when(t > 0)
        def _():
            a_ref[0:HIST_A, :] = a_ref[tm:tm + HIST_A, :]

    a_base = 0 if sample else HIST_A

    def slot_rows(c):
        lo = (c % MIXER_SLOTS) * CHUNK
        return slice(lo, lo + CHUNK)

    def in_proj(c):
        rows = slice(c * CHUNK, (c + 1) * CHUNK)
        hv = h_ref[rows, :]
        for g in range(len(win_ref)):
            z_ref[slot_rows(c), g * DOT_N:(g + 1) * DOT_N] = jnp.dot(
                hv, win_ref[g][...], preferred_element_type=F32)

    def out_proj(c):
        rows = slice(c * CHUNK, (c + 1) * CHUNK)
        cv = cat_ref[slot_rows(c), :]
        for g in range(len(wout_ref)):
            cols = slice(g * DOT_N, (g + 1) * DOT_N)
            y_ref[rows, cols] = x_ref[rows, cols] + jnp.dot(cv, wout_ref[g][...],
                                                            preferred_element_type=F32)

    def elementwise(c):
        r0 = c * CHUNK
        s0 = slot_rows(c).start - r0
        for r in range(r0, r0 + CHUNK, ROWS):
            for lanes in lane_groups:
                zv = z_ref[s0 + r:s0 + r + ROWS, lanes]
                zg = z_ref[s0 + r:s0 + r + ROWS, C_A + lanes.start:C_A + lanes.stop]
                a = zv * jax.nn.sigmoid(zg)
                a_ref[a_base + r:a_base + r + ROWS, lanes] = a
                if sample:
                    keep = CONV_A - 1 - seq_len
                    for q in range(ROWS // seq_len):
                        s = r // seq_len + q
                        aout_ref[s, 0:keep, lanes] = st_ref[s, HIST_A - keep:HIST_A, lanes]
                        aout_ref[s, keep:CONV_A - 1, lanes] = a[q * seq_len:(q + 1) * seq_len]
        for r in range(r0, r0 + CHUNK, ROWS):
            for lanes in lane_groups:
                width = lanes.stop - lanes.start
                if sample:
                    outs = []
                    for rs in range(r, r + ROWS, seq_len):
                        hist = st_ref[rs // seq_len, :, lanes].reshape(
                            CONV_TILES - 1, SUBLANES, width)
                        new = a_ref[rs:rs + seq_len, lanes].reshape(1, SUBLANES, width)
                        win3 = jnp.concatenate([hist, new], axis=0)
                        outs.append(_conv31(win3, 1, coef_ref, lanes))
                    conv = jnp.concatenate(outs, axis=0).reshape(ROWS, width)
                else:
                    n_out = ROWS // SUBLANES
                    win3 = a_ref[r:r + ROWS + HIST_A, lanes].reshape(
                        n_out + CONV_TILES - 1, SUBLANES, width)
                    conv = _conv31(win3, n_out, coef_ref, lanes).reshape(ROWS, width)
                conv = conv + bdw_ref[:, lanes]
                yv = _head_ln(conv, glna_ref[:, lanes], blna_ref[:, lanes])
                cat_ref[s0 + r:s0 + r + ROWS, lanes] = (yv * jax.nn.sigmoid(yv)).astype(BF16)

        v_off = 2 * C_A + C_B
        for r in range(r0, r0 + CHUNK, ROWS):
            for lanes in lane_groups:
                v = jax.nn.gelu(z_ref[s0 + r:s0 + r + ROWS, v_off + lanes.start:v_off + lanes.stop])
                vn = _head_ln(v, glnv_ref[:, lanes], blnv_ref[:, lanes])
                vn_ref[r - r0:r - r0 + ROWS, lanes] = vn.astype(BF16)
                if sample:
                    vout_ref[r:r + ROWS, lanes] = vn
                else:
                    vout_ref[0, r - r0:r - r0 + ROWS, lanes] = vn
        for h in range(N_HEADS):
            lo = h * HEAD_DIM
            mixed = jnp.dot(wm_ref[h], vn_ref[:, lo:lo + HEAD_DIM], preferred_element_type=F32)
            mixed = mixed + bsb_ref[:, lo:lo + HEAD_DIM]
            u = jax.nn.gelu(z_ref[slot_rows(c), 2 * C_A + lo:2 * C_A + lo + HEAD_DIM])
            cat_ref[slot_rows(c), C_A + lo:C_A + lo + HEAD_DIM] = (u * mixed).astype(BF16)

    norm(0)
    in_proj(0)
    for c in range(n_chunks):
        if c + 1 < n_chunks:
            norm(c + 1)
            in_proj(c + 1)
        elementwise(c)
        out_proj(c)

    if not sample:
        aout_ref[0] = a_ref[tm:tm + HIST_A, :]


def _mixer(x, state, seq_len, params, tm):
    (g_mix, w_in, w_dw, b_dw, g_ln_a, b_ln_a, g_ln_v, b_ln_v, w_s, bias_rows, w_out) = params
    rows = x.shape[0]
    sample = state is not None
    vec = lambda n: _resident((1, n))
    common_in = ([vec(D_MODEL)] + _panel_specs(D_MODEL, 2 * C_A + 2 * C_B)
                 + [_resident((SUBLANES * CONV_TILES * SUBLANES, C_A)),
                    vec(C_A), vec(C_A), vec(C_A), vec(C_B), vec(C_B),
                    _resident((N_HEADS, CHUNK, CHUNK)), _resident((CHUNK, C_B))]
                 + _panel_specs(C_A + C_B, D_MODEL))
    ring = MIXER_SLOTS * CHUNK
    common_scratch = [pltpu.VMEM((tm, D_MODEL), BF16),
                      pltpu.VMEM((ring, 2 * C_A + 2 * C_B), F32),
                      pltpu.VMEM((ring, C_A + C_B), BF16)]
    tail_scratch = [pltpu.VMEM((CHUNK, C_B), BF16),
                    pltpu.VMEM((N_HEADS, CHUNK, CHUNK), BF16)]
    if sample:
        n_seq = rows // seq_len
        spt = tm // seq_len
        grid = (rows // tm,)
        row_map = lambda i: (i, 0)
        in_specs = [pl.BlockSpec((tm, D_MODEL), row_map),
                    pl.BlockSpec((spt, HIST_A, C_A), lambda i: (i, 0, 0),
                                 pipeline_mode=pl.Buffered(1))] + common_in
        out_specs = [pl.BlockSpec((tm, D_MODEL), row_map),
                     pl.BlockSpec((spt, CONV_A - 1, C_A), lambda i: (i, 0, 0)),
                     pl.BlockSpec((tm, C_B), row_map)]
        out_shape = [jax.ShapeDtypeStruct((rows, D_MODEL), F32),
                     jax.ShapeDtypeStruct((n_seq, CONV_A - 1, C_A), F32),
                     jax.ShapeDtypeStruct((rows, C_B), F32)]
        scratch = common_scratch + [pltpu.VMEM((tm, C_A), F32)] + tail_scratch
        args = (x, state)
        sem = ("arbitrary",)
        del n_seq
    else:
        n_seq = rows // seq_len
        nt = seq_len // tm
        grid = (n_seq, nt)
        row_map = lambda b, t: (b * nt + t, 0)
        in_specs = [pl.BlockSpec((tm, D_MODEL), row_map)] + common_in
        out_specs = [pl.BlockSpec((tm, D_MODEL), row_map),
                     pl.BlockSpec((1, HIST_A, C_A), lambda b, t: (b, 0, 0)),
                     pl.BlockSpec((1, CHUNK, C_B), lambda b, t: (b, 0, 0))]
        out_shape = [jax.ShapeDtypeStruct((rows, D_MODEL), F32),
                     jax.ShapeDtypeStruct((n_seq, HIST_A, C_A), F32),
                     jax.ShapeDtypeStruct((n_seq, CHUNK, C_B), F32)]
        scratch = common_scratch + [pltpu.VMEM((tm + HIST_A, C_A), F32)] + tail_scratch
        args = (x,)
        sem = ("arbitrary", "arbitrary")
    return pl.pallas_call(
        functools.partial(_mixer_body, sample, tm, seq_len),
        grid=grid, in_specs=in_specs, out_specs=out_specs, out_shape=out_shape,
        scratch_shapes=scratch,
        compiler_params=pltpu.CompilerParams(dimension_semantics=sem,
                                             vmem_limit_bytes=VMEM_LIMIT),
        name="mixer_sample" if sample else "mixer_prompt",
    )(*args, g_mix, *[w_in] * (w_in.shape[1] // DOT_N), w_dw, b_dw, g_ln_a, b_ln_a, g_ln_v,
      b_ln_v, w_s, bias_rows, *[w_out] * (w_out.shape[1] // DOT_N))


def _ffn_body(sample, tm, tf, rc, blocks_per_seq, *refs):
    refs = list(refs)
    x_ref = refs.pop(0)
    stv_ref, stg_ref = (refs.pop(0), refs.pop(0)) if sample else (None, None)
    (g_ref, wuv_ref, wug_ref, wdv_ref, wdg_ref, bdv_ref,
     bdg_ref) = [refs.pop(0) for _ in range(7)]
    wdn_ref = [refs.pop(0) for _ in range(D_MODEL // DOT_N)]
    y_ref, lastv_ref, lastg_ref, h_ref, upv_ref, upg_ref, gate_ref = refs[:7]
    carry_ref = None if sample else refs[7]
    i = pl.program_id(0)
    j = pl.program_id(1)

    @pl.when(j == 0)
    def _():
        _rms_to(x_ref, g_ref, h_ref, tm)
        y_ref[...] = x_ref[...]
        if not sample:
            @pl.when(i == 0)
            def _():
                carry_ref[...] = jnp.zeros(carry_ref.shape, F32)

    hist = SUBLANES
    starts = list(range(0, tm - rc, rc)) + [tm - rc, tm - rc // 2]
    chunks = [(r0, r1 - r0) for r0, r1 in zip(starts, starts[1:] + [tm])]
    ups = ((upv_ref, wuv_ref, lastv_ref), (upg_ref, wug_ref, lastg_ref))

    if not sample:
        first = (i % blocks_per_seq) == 0
        for half, (up_ref, _, _) in enumerate(ups):
            up_ref[0:hist, :] = jnp.where(first, 0.0, carry_ref[j, half])

    def up_proj(c):
        r0, n = chunks[c]
        hc = h_ref[r0:r0 + n, :]
        for up_ref, wu_ref, _ in ups:
            up_ref[hist + r0:hist + r0 + n, :] = jnp.dot(hc, wu_ref[...],
                                                         preferred_element_type=F32)

    def conv(up_ref, st_ref, wd_ref, bd_ref, row, rows, lanes):
        width = lanes.stop - lanes.start
        w0 = wd_ref[0:1, lanes]
        w1 = wd_ref[1:2, lanes]
        w2 = wd_ref[2:3, lanes]
        n = rows // SUBLANES
        if sample:
            s0 = row // SUBLANES
            cur3 = up_ref[hist + row:hist + row + rows, lanes].reshape(n, SUBLANES, width)
            sl = lax.broadcasted_iota(jnp.int32, (n, SUBLANES, width), 1)
            st0 = st_ref[s0:s0 + n, 0:1, lanes]
            st1 = st_ref[s0:s0 + n, 1:2, lanes]
            m1 = jnp.where(sl == 0, st1, pltpu.roll(cur3, 1, 1))
            m2 = jnp.where(sl == 0, st0, jnp.where(sl == 1, st1, pltpu.roll(cur3, 2, 1)))
            out = w0 * m2 + w1 * m1 + w2 * cur3 + bd_ref[:, lanes]
            return out.reshape(rows, width)
        win3 = up_ref[row:row + rows + hist, lanes].reshape(n + 1, SUBLANES, width)
        top = lax.broadcasted_iota(jnp.int32, (n, SUBLANES, width), 1) == 0
        r0 = pltpu.roll(w0 * win3, 1, 1)
        s0 = jnp.concatenate([r0[0:1], jnp.where(top, r0[0:n], r0[1:n + 1])], axis=0)
        r1 = pltpu.roll(w1 * win3 + s0, 1, 1)
        s1 = jnp.where(top, r1[0:n], r1[1:n + 1])
        return (w2 * win3[1:n + 1] + s1 + bd_ref[:, lanes]).reshape(rows, width)

    def gate_down(c, kh):
        r0, n = chunks[c]
        kw = tf // FFN_K_SPLIT
        lanes = slice(kh * kw, (kh + 1) * kw)
        rows = min(n, FFN_ROWS)
        for r in range(0, n, rows):
            cv = conv(upv_ref, stv_ref if sample else None, wdv_ref, bdv_ref, r0 + r, rows, lanes)
            cg = conv(upg_ref, stg_ref if sample else None, wdg_ref, bdg_ref, r0 + r, rows, lanes)
            gate_ref[r:r + rows, lanes] = (cv * jax.nn.sigmoid(cv) * cg).astype(BF16)
        gv = gate_ref[0:n, lanes]
        for g in range(len(wdn_ref)):
            cols = slice(g * DOT_N, (g + 1) * DOT_N)
            y_ref[r0:r0 + n, cols] += jnp.dot(gv, wdn_ref[g][lanes, :],
                                              preferred_element_type=F32)

    up_proj(0)
    for c in range(len(chunks)):
        if c + 1 < len(chunks):
            up_proj(c + 1)
        for kh in range(FFN_K_SPLIT):
            gate_down(c, kh)

    for half, (up_ref, _, last_ref) in enumerate(ups):
        if sample:
            up3 = up_ref[hist:hist + tm, :].reshape(tm // SUBLANES, SUBLANES, tf)
            last_ref[...] = up3[:, SUBLANES - (FFN_CONV - 1):, :]
        else:
            tail = up_ref[tm:tm + hist, :]
            carry_ref[j, half] = tail
            last_ref[0] = tail


def _ffn(x, state, seq_len, params, tm, tf, rc):
    g_ffn, w_up, w_dw, b_dw, w_down = params
    rows = x.shape[0]
    sample = state is not None
    nj = D_FF // tf
    grid = (rows // tm, nj)
    row_map = lambda i, j: (i, 0)
    val_map = lambda i, j: (0, j)
    gate_map = lambda i, j: (0, nj + j)
    in_specs = [pl.BlockSpec((tm, D_MODEL), row_map)]
    args = [x]
    if sample:
        spt = tm // seq_len
        in_specs += [pl.BlockSpec((spt, FFN_CONV - 1, tf), lambda i, j: (i, 0, j)),
                     pl.BlockSpec((spt, FFN_CONV - 1, tf), lambda i, j: (i, 0, nj + j))]
        args += [state, state]
    in_specs += [pl.BlockSpec((1, D_MODEL), lambda i, j: (0, 0)),
                 pl.BlockSpec((D_MODEL, tf), val_map), pl.BlockSpec((D_MODEL, tf), gate_map),
                 pl.BlockSpec((FFN_CONV, tf), val_map), pl.BlockSpec((FFN_CONV, tf), gate_map),
                 pl.BlockSpec((1, tf), val_map), pl.BlockSpec((1, tf), gate_map)]
    in_specs += _panel_specs(tf, D_MODEL, index=lambda g: (lambda i, j: (j, g)), resident=False)
    args += [g_ffn, w_up, w_up, w_dw, w_dw, b_dw, b_dw] + [w_down] * (D_MODEL // DOT_N)
    scratch = [pltpu.VMEM((tm, D_MODEL), BF16),
               pltpu.VMEM((tm + SUBLANES, tf), F32), pltpu.VMEM((tm + SUBLANES, tf), F32),
               pltpu.VMEM((rc, tf), BF16)]
    if sample:
        last_specs = [pl.BlockSpec((spt, FFN_CONV - 1, tf), lambda i, j: (i, 0, j))] * 2
        last_shape = [jax.ShapeDtypeStruct((rows // seq_len, FFN_CONV - 1, D_FF), F32)] * 2
        blocks_per_seq = 1
    else:
        blocks_per_seq = seq_len // tm
        last_specs = [pl.BlockSpec((1, SUBLANES, tf), lambda i, j: (i, 0, j))] * 2
        last_shape = [jax.ShapeDtypeStruct((rows // tm, SUBLANES, D_FF), F32)] * 2
        scratch += [pltpu.VMEM((nj, 2, SUBLANES, tf), F32)]
    return pl.pallas_call(
        functools.partial(_ffn_body, sample, tm, tf, rc, blocks_per_seq),
        grid=grid, in_specs=in_specs,
        out_specs=[pl.BlockSpec((tm, D_MODEL), row_map)] + last_specs,
        out_shape=[jax.ShapeDtypeStruct((rows, D_MODEL), F32)] + last_shape,
        scratch_shapes=scratch,
        compiler_params=pltpu.CompilerParams(dimension_semantics=("arbitrary", "arbitrary"),
                                             vmem_limit_bytes=VMEM_LIMIT),
        name="ffn_sample" if sample else "ffn_prompt",
    )(*args)


def _ple_body(tm, *refs):
    refs = list(refs)
    x_ref, p_ref, g_ref = refs[:3]
    n_panels = D_MODEL // DOT_N
    wg_ref = refs[3:3 + n_panels]
    wp_ref = refs[3 + n_panels:3 + 2 * n_panels]
    gf_ref, y_ref, h_ref, gate_ref, proj_ref = refs[3 + 2 * n_panels:]
    _ple_pipeline(tm, x_ref, p_ref, g_ref, wg_ref, wp_ref, gf_ref, y_ref, h_ref, gate_ref, proj_ref)


def _ple_pipeline(tm, x_ref, p_ref, g_ref, wg_ref, wp_ref, gf_ref, y_ref, h_ref, gate_ref, proj_ref):
    n_chunks = tm // CHUNK
    never = pl.program_id(0) < 0
    pieces = CHUNK // ROWS
    anchors = {}

    def norm(c):
        slot = (c % PLE_SLOTS) * CHUNK
        for k in range(pieces):
            r = c * CHUNK + k * ROWS
            xv = x_ref[r:r + ROWS, :]
            ms = jnp.mean(xv * xv, axis=-1, keepdims=True)
            hv = xv * lax.rsqrt(ms + EPS) * g_ref[...]
            anchor = anchors.pop((c - PLE_SKEW, k), None)
            if anchor is not None:
                head = jnp.where(never, anchor, hv[:, :HEAD_DIM])
                hv = jnp.concatenate([head, hv[:, HEAD_DIM:]], axis=-1)
            h_ref[slot + k * ROWS:slot + (k + 1) * ROWS, :] = hv.astype(BF16)

    def projections(c):
        slot = (c % PLE_SLOTS) * CHUNK
        rows = slice(c * CHUNK, (c + 1) * CHUNK)
        pv = p_ref[rows, :].astype(BF16)
        hv = h_ref[slot:slot + CHUNK, :]
        for g in range(D_MODEL // DOT_N):
            cols = slice(g * DOT_N, (g + 1) * DOT_N)
            gate_ref[slot:slot + CHUNK, cols] = jnp.dot(hv, wg_ref[g][...],
                                                        preferred_element_type=F32)
            proj_ref[slot:slot + CHUNK, cols] = jnp.dot(pv, wp_ref[g][...],
                                                        preferred_element_type=F32)

    def finish(c):
        slot = (c % PLE_SLOTS) * CHUNK
        for k in range(pieces):
            r = c * CHUNK + k * ROWS
            s = slot + k * ROWS
            xv = x_ref[r:r + ROWS, :] + jax.nn.sigmoid(gate_ref[s:s + ROWS, :]) * proj_ref[s:s + ROWS, :]
            ms = jnp.mean(xv * xv, axis=-1, keepdims=True)
            yv = xv * lax.rsqrt(ms + EPS) * gf_ref[...]
            y_ref[r:r + ROWS, :] = yv
            anchors[(c, k)] = yv[:, :HEAD_DIM]

    norm(0)
    for c in range(n_chunks):
        projections(c)
        if c >= 1:
            finish(c - 1)
        if c + 1 < n_chunks:
            norm(c + 1)
    finish(n_chunks - 1)


def _ple(x, p, params, tm, name):
    g_ple, w_gate, w_proj, g_final = params
    rows = x.shape[0]
    row_map = lambda i: (i, 0)
    return pl.pallas_call(
        functools.partial(_ple_body, tm),
        grid=(rows // tm,),
        in_specs=([pl.BlockSpec((tm, D_MODEL), row_map), pl.BlockSpec((tm, D_PLE), row_map),
                   _resident((1, D_MODEL))] + _panel_specs(D_MODEL, D_MODEL)
                  + _panel_specs(D_PLE, D_MODEL) + [_resident((1, D_MODEL))]),
        out_specs=pl.BlockSpec((tm, D_MODEL), row_map),
        out_shape=jax.ShapeDtypeStruct((rows, D_MODEL), F32),
        scratch_shapes=[pltpu.VMEM((PLE_SLOTS * CHUNK, D_MODEL), BF16),
                        pltpu.VMEM((PLE_SLOTS * CHUNK, D_MODEL), F32),
                        pltpu.VMEM((PLE_SLOTS * CHUNK, D_MODEL), F32)],
        compiler_params=pltpu.CompilerParams(dimension_semantics=("arbitrary",),
                                             vmem_limit_bytes=VMEM_LIMIT),
        name=name,
    )(x, p, g_ple, *[w_gate] * (D_MODEL // DOT_N), *[w_proj] * (D_MODEL // DOT_N), g_final)


def _conv_coef(w):
    c = w.shape[1]
    off = HIST_A - (CONV_A - 1)
    n = SUBLANES * (CONV_TILES + 1)
    wp = jnp.zeros((n, c), w.dtype).at[off:off + CONV_A].set(w)
    wp_prev = jnp.concatenate([jnp.zeros((SUBLANES, c), w.dtype), wp[:n - SUBLANES]], axis=0)
    r = jnp.arange(SUBLANES)[:, None]
    q = jnp.arange(CONV_TILES)[None, :]
    d = SUBLANES * q + r
    s = jnp.arange(SUBLANES)[None, None, :, None]
    coef = jnp.where(s >= r[:, :, None, None], wp[d][:, :, None, :], wp_prev[d][:, :, None, :])
    return coef.reshape(SUBLANES * CONV_TILES * SUBLANES, c)


def kernel(x_prompt, x_sample, p_prompt, p_sample, state_conv_a, state_ffn_conv, g_mix, w_in, w_dw_a, b_dw_a, g_ln_a, b_ln_a, g_ln_v, b_ln_v, w_s, b_s, w_out, g_ffn, w_up, w_dw_f, b_dw_f, w_down, g_ple, w_ple_gate, w_ple_proj, g_final):
    depth = w_in.shape[0]
    assert depth == 1, "single-layer step"
    batch, seq, _ = x_prompt.shape
    dec_batch, dec_seq, _ = x_sample.shape
    row = lambda v: v.reshape(1, -1)

    def bias_rows(length):
        b = jnp.tile(b_s[0][:, :length], (1, CHUNK // length))
        return jnp.repeat(b.T, HEAD_DIM, axis=1)

    def mix_weights(length):
        if length == CHUNK:
            return w_s[0]
        sel = (jnp.arange(CHUNK)[:, None] % length == jnp.arange(length)[None, :]).astype(F32)
        return jnp.einsum('ti,hij,sj->hts', sel, w_s[0][:, :length, :length], sel,
                          precision=lax.Precision.HIGHEST)

    mixer_common = (row(g_mix[0]), w_in[0].astype(BF16), _conv_coef(w_dw_a[0]), row(b_dw_a[0]),
                    row(g_ln_a[0]), row(b_ln_a[0]), row(g_ln_v[0]), row(b_ln_v[0]))
    w_out_b = w_out[0].astype(BF16)
    ffn_params = (row(g_ffn[0]), w_up[0].astype(BF16), w_dw_f[0], row(b_dw_f[0]),
                  w_down[0].astype(BF16))
    ple_params = (row(g_ple[0]), w_ple_gate[0].astype(BF16), w_ple_proj[0].astype(BF16),
                  row(g_final))

    xp = x_prompt.reshape(batch * seq, D_MODEL)
    lp = min(seq, CHUNK)
    xp, conv_p, cv_p = _mixer(xp, None, seq,
                              mixer_common + (mix_weights(lp), bias_rows(lp), w_out_b), tm=256)
    xp, lastv_p, lastg_p = _ffn(xp, None, seq, ffn_params, tm=1024, tf=512, rc=256)
    yp = _ple(xp, p_prompt[0].reshape(batch * seq, D_PLE), ple_params, 1024, "ple_prompt")

    xs = x_sample.reshape(dec_batch * dec_seq, D_MODEL)
    ls = min(dec_seq, CHUNK)
    state_pad = jnp.pad(state_conv_a[0], ((0, 0), (HIST_A - (CONV_A - 1), 0), (0, 0)))
    xs, a_s, cv_s = _mixer(xs, state_pad, dec_seq,
                           mixer_common + (mix_weights(ls), bias_rows(ls), w_out_b), tm=256)
    xs, upv_s, upg_s = _ffn(xs, state_ffn_conv[0], dec_seq, ffn_params, tm=1024, tf=512, rc=256)
    ys = _ple(xs, p_sample[0].reshape(dec_batch * dec_seq, D_PLE), ple_params, 1024, "ple_sample")

    keep = FFN_CONV - 1
    conv_a_prompt = conv_p[:, HIST_A - (CONV_A - 1):]
    conv_a_sample = a_s
    per_seq = lastv_p.shape[0] // batch
    ffn_prompt = jnp.concatenate([lastv_p, lastg_p], axis=-1)[per_seq - 1::per_seq, SUBLANES - keep:]
    ffn_sample = jnp.concatenate([upv_s, upg_s], axis=-1)
    return (yp.reshape(batch, seq, D_MODEL), ys.reshape(dec_batch, dec_seq, D_MODEL),
            conv_a_prompt[None], conv_a_sample[None], ffn_prompt[None], ffn_sample[None],
            cv_p[None], cv_s.reshape(dec_batch, dec_seq, C_B)[None])
```

```python
import functools

import jax
import jax.numpy as jnp
from jax import lax
from jax.experimental import pallas as pl
from jax.experimental.pallas import tpu as pltpu

D_MODEL = 2048
HEAD_DIM = 128
C_A = D_MODEL // 2
C_B = D_MODEL - C_A
N_HEADS = C_B // HEAD_DIM
CONV_A = 31
HIST_A = 32
CONV_TILES = HIST_A // 8 + 1
CHUNK = 128
FFN_CONV = 3
D_FF = 5632
D_PLE = 256
EPS = 1e-6

SUBLANES = 8
ROWS = 64
LANES_EW = 128
FFN_ROWS = 256
FFN_K_SPLIT = 2
DOT_N = 512
MIXER_SLOTS = 2
PLE_SKEW = 2
PLE_SLOTS = 3
VMEM_LIMIT = 56 * 1024 * 1024

F32 = jnp.float32
BF16 = jnp.bfloat16


def _resident(shape):
    return pl.BlockSpec(shape, lambda *_: (0,) * len(shape), pipeline_mode=pl.Buffered(1))


def _panel_specs(k, n, index=None, resident=True):
    specs = []
    for g in range(n // DOT_N):
        imap = index(g) if index is not None else (lambda *_, g=g: (0, g))
        specs.append(pl.BlockSpec((k, DOT_N), imap,
                                  pipeline_mode=pl.Buffered(1) if resident else None))
    return specs


def _aligned(x, m):
    return x if isinstance(x, int) else pl.multiple_of(x, m)


def _row_loop(n_rows, step, fn):
    def body(c, carry):
        fn(pl.multiple_of(c * step, step))
        return carry
    lax.fori_loop(0, n_rows // step, body, 0, unroll=16)


def _rms_to(x_ref, g_ref, dst_ref, n_rows):
    def piece(r):
        xv = x_ref[pl.ds(r, ROWS), :]
        ms = jnp.mean(xv * xv, axis=-1, keepdims=True)
        dst_ref[pl.ds(r, ROWS), :] = (xv * lax.rsqrt(ms + EPS) * g_ref[...]).astype(BF16)
    _row_loop(n_rows, ROWS, piece)


def _head_ln(x, g, b):
    outs = []
    for h in range(x.shape[-1] // HEAD_DIM):
        xh = x[:, h * HEAD_DIM:(h + 1) * HEAD_DIM]
        mu = jnp.mean(xh, axis=-1, keepdims=True)
        xc = xh - mu
        var = jnp.mean(xc * xc, axis=-1, keepdims=True)
        outs.append(xc * lax.rsqrt(var + EPS))
    return jnp.concatenate(outs, axis=-1) * g + b


def _conv31(win3, n_out, coef_ref, lanes):
    acc = None
    for r in range(SUBLANES):
        inner = None
        for q in range(CONV_TILES):
            if q == 0 and r < HIST_A - (CONV_A - 1):
                continue
            k = (r * CONV_TILES + q) * SUBLANES
            term = coef_ref[k:k + SUBLANES, lanes] * win3[q:q + n_out]
            inner = term if inner is None else inner + term
        if r:
            inner = pltpu.roll(inner, SUBLANES - r, 1)
        acc = inner if acc is None else acc + inner
    return acc


def _mixer_body(sample, tm, seq_len, *refs):
    refs = list(refs)
    x_ref = refs.pop(0)
    st_ref = refs.pop(0) if sample else None
    gmix_ref = refs.pop(0)
    win_ref = [refs.pop(0) for _ in range((2 * C_A + 2 * C_B) // DOT_N)]
    (coef_ref, bdw_ref, glna_ref, blna_ref, glnv_ref, blnv_ref, ws_ref,
     bsb_ref) = [refs.pop(0) for _ in range(8)]
    wout_ref = [refs.pop(0) for _ in range(D_MODEL // DOT_N)]
    y_ref, aout_ref, vout_ref, h_ref, z_ref, cat_ref, a_ref, vn_ref, wm_ref = refs
    n_chunks = tm // CHUNK
    lane_groups = [slice(g * LANES_EW, (g + 1) * LANES_EW) for g in range(C_A // LANES_EW)]

    def norm(c):
        for r in range(c * CHUNK, (c + 1) * CHUNK, ROWS):
            xv = x_ref[r:r + ROWS, :]
            ms = jnp.mean(xv * xv, axis=-1, keepdims=True)
            h_ref[r:r + ROWS, :] = (xv * lax.rsqrt(ms + EPS) * gmix_ref[...]).astype(BF16)

    ti = lax.broadcasted_iota(jnp.int32, (CHUNK, CHUNK), 0)
    si = lax.broadcasted_iota(jnp.int32, (CHUNK, CHUNK), 1)
    if seq_len >= CHUNK:
        mask = si <= ti
    else:
        mask = ((si // seq_len) == (ti // seq_len)) & ((si % seq_len) <= (ti % seq_len))
    for h in range(N_HEADS):
        wm_ref[h] = jnp.where(mask, ws_ref[h], 0.0).astype(BF16)

    if not sample:
        t = pl.program_id(1)

        @pl.when(t == 0)
        def _():
            a_ref[0:HIST_A, :] = jnp.zeros((HIST_A, C_A), F32)

        @pl.when(t > 0)
        def _():
            a_ref[0:HIST_A, :] = a_ref[tm:tm + HIST_A, :]

    a_base = 0 if sample else HIST_A

    def slot_rows(c):
        lo = (c % MIXER_SLOTS) * CHUNK
        return slice(lo, lo + CHUNK)

    def in_proj(c):
        rows = slice(c * CHUNK, (c + 1) * CHUNK)
        hv = h_ref[rows, :]
        for g in range(len(win_ref)):
            z_ref[slot_rows(c), g * DOT_N:(g + 1) * DOT_N] = jnp.dot(
                hv, win_ref[g][...], preferred_element_type=F32)

    def out_proj(c):
        rows = slice(c * CHUNK, (c + 1) * CHUNK)
        cv = cat_ref[slot_rows(c), :]
        for g in range(len(wout_ref)):
            cols = slice(g * DOT_N, (g + 1) * DOT_N)
            y_ref[rows, cols] = x_ref[rows, cols] + jnp.dot(cv, wout_ref[g][...],
                                                            preferred_element_type=F32)

    def elementwise(c):
        r0 = c * CHUNK
        s0 = slot_rows(c).start - r0
        for r in range(r0, r0 + CHUNK, ROWS):
            for lanes in lane_groups:
                zv = z_ref[s0 + r:s0 + r + ROWS, lanes]
                zg = z_ref[s0 + r:s0 + r + ROWS, C_A + lanes.start:C_A + lanes.stop]
                a = zv * jax.nn.sigmoid(zg)
                a_ref[a_base + r:a_base + r + ROWS, lanes] = a
                if sample:
                    keep = CONV_A - 1 - seq_len
                    for q in range(ROWS // seq_len):
                        s = r // seq_len + q
                        aout_ref[s, 0:keep, lanes] = st_ref[s, HIST_A - keep:HIST_A, lanes]
                        aout_ref[s, keep:CONV_A - 1, lanes] = a[q * seq_len:(q + 1) * seq_len]
        for r in range(r0, r0 + CHUNK, ROWS):
            for lanes in lane_groups:
                width = lanes.stop - lanes.start
                if sample:
                    outs = []
                    for rs in range(r, r + ROWS, seq_len):
                        hist = st_ref[rs // seq_len, :, lanes].reshape(
                            CONV_TILES - 1, SUBLANES, width)
                        new = a_ref[rs:rs + seq_len, lanes].reshape(1, SUBLANES, width)
                        win3 = jnp.concatenate([hist, new], axis=0)
                        outs.append(_conv31(win3, 1, coef_ref, lanes))
                    conv = jnp.concatenate(outs, axis=0).reshape(ROWS, width)
                else:
                    n_out = ROWS // SUBLANES
                    win3 = a_ref[r:r + ROWS + HIST_A, lanes].reshape(
                        n_out + CONV_TILES - 1, SUBLANES, width)
                    conv = _conv31(win3, n_out, coef_ref, lanes).reshape(ROWS, width)
                conv = conv + bdw_ref[:, lanes]
                yv = _head_ln(conv, glna_ref[:, lanes], blna_ref[:, lanes])
                cat_ref[s0 + r:s0 + r + ROWS, lanes] = (yv * jax.nn.sigmoid(yv)).astype(BF16)

        v_off = 2 * C_A + C_B
        for r in range(r0, r0 + CHUNK, ROWS):
            for lanes in lane_groups:
                v = jax.nn.gelu(z_ref[s0 + r:s0 + r + ROWS, v_off + lanes.start:v_off + lanes.stop])
                vn = _head_ln(v, glnv_ref[:, lanes], blnv_ref[:, lanes])
                vn_ref[r - r0:r - r0 + ROWS, lanes] = vn.astype(BF16)
                if sample:
                    vout_ref[r:r + ROWS, lanes] = vn
                else:
                    vout_ref[0, r - r0:r - r0 + ROWS, lanes] = vn
        for h in range(N_HEADS):
            lo = h * HEAD_DIM
            mixed = jnp.dot(wm_ref[h], vn_ref[:, lo:lo + HEAD_DIM], preferred_element_type=F32)
            mixed = mixed + bsb_ref[:, lo:lo + HEAD_DIM]
            u = jax.nn.gelu(z_ref[slot_rows(c), 2 * C_A + lo:2 * C_A + lo + HEAD_DIM])
            cat_ref[slot_rows(c), C_A + lo:C_A + lo + HEAD_DIM] = (u * mixed).astype(BF16)

    norm(0)
    in_proj(0)
    for c in range(n_chunks):
        if c + 1 < n_chunks:
            norm(c + 1)
            in_proj(c + 1)
        elementwise(c)
        out_proj(c)

    if not sample:
        aout_ref[0] = a_ref[tm:tm + HIST_A, :]


def _mixer(x, state, seq_len, params, tm):
    (g_mix, w_in, w_dw, b_dw, g_ln_a, b_ln_a, g_ln_v, b_ln_v, w_s, bias_rows, w_out) = params
    rows = x.shape[0]
    sample = state is not None
    vec = lambda n: _resident((1, n))
    common_in = ([vec(D_MODEL)] + _panel_specs(D_MODEL, 2 * C_A + 2 * C_B)
                 + [_resident((SUBLANES * CONV_TILES * SUBLANES, C_A)),
                    vec(C_A), vec(C_A), vec(C_A), vec(C_B), vec(C_B),
                    _resident((N_HEADS, CHUNK, CHUNK)), _resident((CHUNK, C_B))]
                 + _panel_specs(C_A + C_B, D_MODEL))
    ring = MIXER_SLOTS * CHUNK
    common_scratch = [pltpu.VMEM((tm, D_MODEL), BF16),
                      pltpu.VMEM((ring, 2 * C_A + 2 * C_B), F32),
                      pltpu.VMEM((ring, C_A + C_B), BF16)]
    tail_scratch = [pltpu.VMEM((CHUNK, C_B), BF16),
                    pltpu.VMEM((N_HEADS, CHUNK, CHUNK), BF16)]
    if sample:
        n_seq = rows // seq_len
        spt = tm // seq_len
        grid = (rows // tm,)
        row_map = lambda i: (i, 0)
        in_specs = [pl.BlockSpec((tm, D_MODEL), row_map),
                    pl.BlockSpec((spt, HIST_A, C_A), lambda i: (i, 0, 0),
                                 pipeline_mode=pl.Buffered(1))] + common_in
        out_specs = [pl.BlockSpec((tm, D_MODEL), row_map),
                     pl.BlockSpec((spt, CONV_A - 1, C_A), lambda i: (i, 0, 0)),
                     pl.BlockSpec((tm, C_B), row_map)]
        out_shape = [jax.ShapeDtypeStruct((rows, D_MODEL), F32),
                     jax.ShapeDtypeStruct((n_seq, CONV_A - 1, C_A), F32),
                     jax.ShapeDtypeStruct((rows, C_B), F32)]
        scratch = common_scratch + [pltpu.VMEM((tm, C_A), F32)] + tail_scratch
        args = (x, state)
        sem = ("arbitrary",)
        del n_seq
    else:
        n_seq = rows // seq_len
        nt = seq_len // tm
        grid = (n_seq, nt)
        row_map = lambda b, t: (b * nt + t, 0)
        in_specs = [pl.BlockSpec((tm, D_MODEL), row_map)] + common_in
        out_specs = [pl.BlockSpec((tm, D_MODEL), row_map),
                     pl.BlockSpec((1, HIST_A, C_A), lambda b, t: (b, 0, 0)),
                     pl.BlockSpec((1, CHUNK, C_B), lambda b, t: (b, 0, 0))]
        out_shape = [jax.ShapeDtypeStruct((rows, D_MODEL), F32),
                     jax.ShapeDtypeStruct((n_seq, HIST_A, C_A), F32),
                     jax.ShapeDtypeStruct((n_seq, CHUNK, C_B), F32)]
        scratch = common_scratch + [pltpu.VMEM((tm + HIST_A, C_A), F32)] + tail_scratch
        args = (x,)
        sem = ("arbitrary", "arbitrary")
    return pl.pallas_call(
        functools.partial(_mixer_body, sample, tm, seq_len),
        grid=grid, in_specs=in_specs, out_specs=out_specs, out_shape=out_shape,
        scratch_shapes=scratch,
        compiler_params=pltpu.CompilerParams(dimension_semantics=sem,
                                             vmem_limit_bytes=VMEM_LIMIT),
        name="mixer_sample" if sample else "mixer_prompt",
    )(*args, g_mix, *[w_in] * (w_in.shape[1] // DOT_N), w_dw, b_dw, g_ln_a, b_ln_a, g_ln_v,
      b_ln_v, w_s, bias_rows, *[w_out] * (w_out.shape[1] // DOT_N))


def _ffn_body(sample, tm, tf, rc, blocks_per_seq, *refs):
    refs = list(refs)
    x_ref = refs.pop(0)
    stv_ref, stg_ref = (refs.pop(0), refs.pop(0)) if sample else (None, None)
    (g_ref, wuv_ref, wug_ref, wdv_ref, wdg_ref, bdv_ref,
     bdg_ref) = [refs.pop(0) for _ in range(7)]
    wdn_ref = [refs.pop(0) for _ in range(D_MODEL // DOT_N)]
    y_ref, lastv_ref, lastg_ref, h_ref, upv_ref, upg_ref, gate_ref = refs[:7]
    carry_ref = None if sample else refs[7]
    i = pl.program_id(0)
    j = pl.program_id(1)

    @pl.when(j == 0)
    def _():
        _rms_to(x_ref, g_ref, h_ref, tm)
        y_ref[...] = x_ref[...]
        if not sample:
            @pl.when(i == 0)
            def _():
                carry_ref[...] = jnp.zeros(carry_ref.shape, F32)

    hist = SUBLANES
    starts = list(range(0, tm - rc, rc)) + [tm - rc, tm - rc // 2]
    chunks = [(r0, r1 - r0) for r0, r1 in zip(starts, starts[1:] + [tm])]
    ups = ((upv_ref, wuv_ref, lastv_ref), (upg_ref, wug_ref, lastg_ref))

    if not sample:
        first = (i % blocks_per_seq) == 0
        for half, (up_ref, _, _) in enumerate(ups):
            up_ref[0:hist, :] = jnp.where(first, 0.0, carry_ref[j, half])

    def up_proj(c):
        r0, n = chunks[c]
        hc = h_ref[r0:r0 + n, :]
        for up_ref, wu_ref, _ in ups:
            up_ref[hist + r0:hist + r0 + n, :] = jnp.dot(hc, wu_ref[...],
                                                         preferred_element_type=F32)

    def conv(up_ref, st_ref, wd_ref, bd_ref, row, rows, lanes):
        width = lanes.stop - lanes.start
        w0 = wd_ref[0:1, lanes]
        w1 = wd_ref[1:2, lanes]
        w2 = wd_ref[2:3, lanes]
        n = rows // SUBLANES
        if sample:
            s0 = row // SUBLANES
            cur3 = up_ref[hist + row:hist + row + rows, lanes].reshape(n, SUBLANES, width)
            sl = lax.broadcasted_iota(jnp.int32, (n, SUBLANES, width), 1)
            st0 = st_ref[s0:s0 + n, 0:1, lanes]
            st1 = st_ref[s0:s0 + n, 1:2, lanes]
            m1 = jnp.where(sl == 0, st1, pltpu.roll(cur3, 1, 1))
            m2 = jnp.where(sl == 0, st0, jnp.where(sl == 1, st1, pltpu.roll(cur3, 2, 1)))
            out = w0 * m2 + w1 * m1 + w2 * cur3 + bd_ref[:, lanes]
            return out.reshape(rows, width)
        win3 = up_ref[row:row + rows + hist, lanes].reshape(n + 1, SUBLANES, width)
        top = lax.broadcasted_iota(jnp.int32, (n, SUBLANES, width), 1) == 0
        r0 = pltpu.roll(w0 * win3, 1, 1)
        s0 = jnp.concatenate([r0[0:1], jnp.where(top, r0[0:n], r0[1:n + 1])], axis=0)
        r1 = pltpu.roll(w1 * win3 + s0, 1, 1)
        s1 = jnp.where(top, r1[0:n], r1[1:n + 1])
        return (w2 * win3[1:n + 1] + s1 + bd_ref[:, lanes]).reshape(rows, width)

    def gate_down(c, kh):
        r0, n = chunks[c]
        kw = tf // FFN_K_SPLIT
        lanes = slice(kh * kw, (kh + 1) * kw)
        rows = min(n, FFN_ROWS)
        for r in range(0, n, rows):
            cv = conv(upv_ref, stv_ref if sample else None, wdv_ref, bdv_ref, r0 + r, rows, lanes)
            cg = conv(upg_ref, stg_ref if sample else None, wdg_ref, bdg_ref, r0 + r, rows, lanes)
            gate_ref[r:r + rows, lanes] = (cv * jax.nn.sigmoid(cv) * cg).astype(BF16)
        gv = gate_ref[0:n, lanes]
        for g in range(len(wdn_ref)):
            cols = slice(g * DOT_N, (g + 1) * DOT_N)
            y_ref[r0:r0 + n, cols] += jnp.dot(gv, wdn_ref[g][lanes, :],
                                              preferred_element_type=F32)

    up_proj(0)
    for c in range(len(chunks)):
        if c + 1 < len(chunks):
            up_proj(c + 1)
        for kh in range(FFN_K_SPLIT):
            gate_down(c, kh)

    for half, (up_ref, _, last_ref) in enumerate(ups):
        if sample:
            up3 = up_ref[hist:hist + tm, :].reshape(tm // SUBLANES, SUBLANES, tf)
            last_ref[...] = up3[:, SUBLANES - (FFN_CONV - 1):, :]
        else:
            tail = up_ref[tm:tm + hist, :]
            carry_ref[j, half] = tail
            last_ref[0] = tail


def _ffn(x, state, seq_len, params, tm, tf, rc):
    g_ffn, w_up, w_dw, b_dw, w_down = params
    rows = x.shape[0]
    sample = state is not None
    nj = D_FF // tf
    grid = (rows // tm, nj)
    row_map = lambda i, j: (i, 0)
    val_map = lambda i, j: (0, j)
    gate_map = lambda i, j: (0, nj + j)
    in_specs = [pl.BlockSpec((tm, D_MODEL), row_map)]
    args = [x]
    if sample:
        spt = tm // seq_len
        in_specs += [pl.BlockSpec((spt, FFN_CONV - 1, tf), lambda i, j: (i, 0, j)),
                     pl.BlockSpec((spt, FFN_CONV - 1, tf), lambda i, j: (i, 0, nj + j))]
        args += [state, state]
    in_specs += [pl.BlockSpec((1, D_MODEL), lambda i, j: (0, 0)),
                 pl.BlockSpec((D_MODEL, tf), val_map), pl.BlockSpec((D_MODEL, tf), gate_map),
                 pl.BlockSpec((FFN_CONV, tf), val_map), pl.BlockSpec((FFN_CONV, tf), gate_map),
                 pl.BlockSpec((1, tf), val_map), pl.BlockSpec((1, tf), gate_map)]
    in_specs += _panel_specs(tf, D_MODEL, index=lambda g: (lambda i, j: (j, g)), resident=False)
    args += [g_ffn, w_up, w_up, w_dw, w_dw, b_dw, b_dw] + [w_down] * (D_MODEL // DOT_N)
    scratch = [pltpu.VMEM((tm, D_MODEL), BF16),
               pltpu.VMEM((tm + SUBLANES, tf), F32), pltpu.VMEM((tm + SUBLANES, tf), F32),
               pltpu.VMEM((rc, tf), BF16)]
    if sample:
        last_specs = [pl.BlockSpec((spt, FFN_CONV - 1, tf), lambda i, j: (i, 0, j))] * 2
        last_shape = [jax.ShapeDtypeStruct((rows // seq_len, FFN_CONV - 1, D_FF), F32)] * 2
        blocks_per_seq = 1
    else:
        blocks_per_seq = seq_len // tm
        last_specs = [pl.BlockSpec((1, SUBLANES, tf), lambda i, j: (i, 0, j))] * 2
        last_shape = [jax.ShapeDtypeStruct((rows // tm, SUBLANES, D_FF), F32)] * 2
        scratch += [pltpu.VMEM((nj, 2, SUBLANES, tf), F32)]
    return pl.pallas_call(
        functools.partial(_ffn_body, sample, tm, tf, rc, blocks_per_seq),
        grid=grid, in_specs=in_specs,
        out_specs=[pl.BlockSpec((tm, D_MODEL), row_map)] + last_specs,
        out_shape=[jax.ShapeDtypeStruct((rows, D_MODEL), F32)] + last_shape,
        scratch_shapes=scratch,
        compiler_params=pltpu.CompilerParams(dimension_semantics=("arbitrary", "arbitrary"),
                                             vmem_limit_bytes=VMEM_LIMIT),
        name="ffn_sample" if sample else "ffn_prompt",
    )(*args)


def _ple_body(tm, *refs):
    refs = list(refs)
    x_ref, p_ref, g_ref = refs[:3]
    n_panels = D_MODEL // DOT_N
    wg_ref = refs[3:3 + n_panels]
    wp_ref = refs[3 + n_panels:3 + 2 * n_panels]
    gf_ref, y_ref, h_ref, gate_ref, proj_ref = refs[3 + 2 * n_panels:]
    _ple_pipeline(tm, x_ref, p_ref, g_ref, wg_ref, wp_ref, gf_ref, y_ref, h_ref, gate_ref, proj_ref)


def _ple_pipeline(tm, x_ref, p_ref, g_ref, wg_ref, wp_ref, gf_ref, y_ref, h_ref, gate_ref, proj_ref):
    n_chunks = tm // CHUNK
    never = pl.program_id(0) < 0
    pieces = CHUNK // ROWS
    anchors = {}

    def norm(c):
        slot = (c % PLE_SLOTS) * CHUNK
        for k in range(pieces):
            r = c * CHUNK + k * ROWS
            xv = x_ref[r:r + ROWS, :]
            ms = jnp.mean(xv * xv, axis=-1, keepdims=True)
            hv = xv * lax.rsqrt(ms + EPS) * g_ref[...]
            anchor = anchors.pop((c - PLE_SKEW, k), None)
            if anchor is not None:
                head = jnp.where(never, anchor, hv[:, :HEAD_DIM])
                hv = jnp.concatenate([head, hv[:, HEAD_DIM:]], axis=-1)
            h_ref[slot + k * ROWS:slot + (k + 1) * ROWS, :] = hv.astype(BF16)

    def projections(c):
        slot = (c % PLE_SLOTS) * CHUNK
        rows = slice(c * CHUNK, (c + 1) * CHUNK)
        pv = p_ref[rows, :].astype(BF16)
        hv = h_ref[slot:slot + CHUNK, :]
        for g in range(D_MODEL // DOT_N):
            cols = slice(g * DOT_N, (g + 1) * DOT_N)
            gate_ref[slot:slot + CHUNK, cols] = jnp.dot(hv, wg_ref[g][...],
                                                        preferred_element_type=F32)
            proj_ref[slot:slot + CHUNK, cols] = jnp.dot(pv, wp_ref[g][...],
                                                        preferred_element_type=F32)

    def finish(c):
        slot = (c % PLE_SLOTS) * CHUNK
        for k in range(pieces):
            r = c * CHUNK + k * ROWS
            s = slot + k * ROWS
            xv = x_ref[r:r + ROWS, :] + jax.nn.sigmoid(gate_ref[s:s + ROWS, :]) * proj_ref[s:s + ROWS, :]
            ms = jnp.mean(xv * xv, axis=-1, keepdims=True)
            yv = xv * lax.rsqrt(ms + EPS) * gf_ref[...]
            y_ref[r:r + ROWS, :] = yv
            anchors[(c, k)] = yv[:, :HEAD_DIM]

    norm(0)
    for c in range(n_chunks):
        projections(c)
        if c >= 1:
            finish(c - 1)
        if c + 1 < n_chunks:
            norm(c + 1)
    finish(n_chunks - 1)


def _ple(x, p, params, tm, name):
    g_ple, w_gate, w_proj, g_final = params
    rows = x.shape[0]
    row_map = lambda i: (i, 0)
    return pl.pallas_call(
        functools.partial(_ple_body, tm),
        grid=(rows // tm,),
        in_specs=([pl.BlockSpec((tm, D_MODEL), row_map), pl.BlockSpec((tm, D_PLE), row_map),
                   _resident((1, D_MODEL))] + _panel_specs(D_MODEL, D_MODEL)
                  + _panel_specs(D_PLE, D_MODEL) + [_resident((1, D_MODEL))]),
        out_specs=pl.BlockSpec((tm, D_MODEL), row_map),
        out_shape=jax.ShapeDtypeStruct((rows, D_MODEL), F32),
        scratch_shapes=[pltpu.VMEM((PLE_SLOTS * CHUNK, D_MODEL), BF16),
                        pltpu.VMEM((PLE_SLOTS * CHUNK, D_MODEL), F32),
                        pltpu.VMEM((PLE_SLOTS * CHUNK, D_MODEL), F32)],
        compiler_params=pltpu.CompilerParams(dimension_semantics=("arbitrary",),
                                             vmem_limit_bytes=VMEM_LIMIT),
        name=name,
    )(x, p, g_ple, *[w_gate] * (D_MODEL // DOT_N), *[w_proj] * (D_MODEL // DOT_N), g_final)


def _conv_coef(w):
    c = w.shape[1]
    off = HIST_A - (CONV_A - 1)
    n = SUBLANES * (CONV_TILES + 1)
    wp = jnp.zeros((n, c), w.dtype).at[off:off + CONV_A].set(w)
    wp_prev = jnp.concatenate([jnp.zeros((SUBLANES, c), w.dtype), wp[:n - SUBLANES]], axis=0)
    r = jnp.arange(SUBLANES)[:, None]
    q = jnp.arange(CONV_TILES)[None, :]
    d = SUBLANES * q + r
    s = jnp.arange(SUBLANES)[None, None, :, None]
    coef = jnp.where(s >= r[:, :, None, None], wp[d][:, :, None, :], wp_prev[d][:, :, None, :])
    return coef.reshape(SUBLANES * CONV_TILES * SUBLANES, c)


def kernel(x_prompt, x_sample, p_prompt, p_sample, state_conv_a, state_ffn_conv, g_mix, w_in, w_dw_a, b_dw_a, g_ln_a, b_ln_a, g_ln_v, b_ln_v, w_s, b_s, w_out, g_ffn, w_up, w_dw_f, b_dw_f, w_down, g_ple, w_ple_gate, w_ple_proj, g_final):
    depth = w_in.shape[0]
    assert depth == 1, "single-layer step"
    batch, seq, _ = x_prompt.shape
    dec_batch, dec_seq, _ = x_sample.shape
    row = lambda v: v.reshape(1, -1)

    def bias_rows(length):
        b = jnp.tile(b_s[0][:, :length], (1, CHUNK // length))
        return jnp.repeat(b.T, HEAD_DIM, axis=1)

    def mix_weights(length):
        if length == CHUNK:
            return w_s[0]
        sel = (jnp.arange(CHUNK)[:, None] % length == jnp.arange(length)[None, :]).astype(F32)
        return jnp.einsum('ti,hij,sj->hts', sel, w_s[0][:, :length, :length], sel,
                          precision=lax.Precision.HIGHEST)

    mixer_common = (row(g_mix[0]), w_in[0].astype(BF16), _conv_coef(w_dw_a[0]), row(b_dw_a[0]),
                    row(g_ln_a[0]), row(b_ln_a[0]), row(g_ln_v[0]), row(b_ln_v[0]))
    w_out_b = w_out[0].astype(BF16)
    ffn_params = (row(g_ffn[0]), w_up[0].astype(BF16), w_dw_f[0], row(b_dw_f[0]),
                  w_down[0].astype(BF16))
    ple_params = (row(g_ple[0]), w_ple_gate[0].astype(BF16), w_ple_proj[0].astype(BF16),
                  row(g_final))

    xp = x_prompt.reshape(batch * seq, D_MODEL)
    lp = min(seq, CHUNK)
    xp, conv_p, cv_p = _mixer(xp, None, seq,
                              mixer_common + (mix_weights(lp), bias_rows(lp), w_out_b), tm=256)
    xp, lastv_p, lastg_p = _ffn(xp, None, seq, ffn_params, tm=1024, tf=512, rc=256)
    yp = _ple(xp, p_prompt[0].reshape(batch * seq, D_PLE), ple_params, 1024, "ple_prompt")

    xs = x_sample.reshape(dec_batch * dec_seq, D_MODEL)
    ls = min(dec_seq, CHUNK)
    state_pad = jnp.pad(state_conv_a[0], ((0, 0), (HIST_A - (CONV_A - 1), 0), (0, 0)))
    xs, a_s, cv_s = _mixer(xs, state_pad, dec_seq,
                           mixer_common + (mix_weights(ls), bias_rows(ls), w_out_b), tm=256)
    xs, upv_s, upg_s = _ffn(xs, state_ffn_conv[0], dec_seq, ffn_params, tm=1024, tf=512, rc=256)
    ys = _ple(xs, p_sample[0].reshape(dec_batch * dec_seq, D_PLE), ple_params, 1024, "ple_sample")

    keep = FFN_CONV - 1
    conv_a_prompt = conv_p[:, HIST_A - (CONV_A - 1):]
    conv_a_sample = a_s
    per_seq = lastv_p.shape[0] // batch
    ffn_prompt = jnp.concatenate([lastv_p, lastg_p], axis=-1)[per_seq - 1::per_seq, SUBLANES - keep:]
    ffn_sample = jnp.concatenate([upv_s, upg_s], axis=-1)
    return (yp.reshape(batch, seq, D_MODEL), ys.reshape(dec_batch, dec_seq, D_MODEL),
            conv_a_prompt[None], conv_a_sample[None], ffn_prompt[None], ffn_sample[None],
            cv_p[None], cv_s.reshape(dec_batch, dec_seq, C_B)[None])
```
